```python
import math
import jax, jax.numpy as jnp
from jax import lax
import numpy as np

D_MODEL = 1024
BATCH = 2
SEQ = 16384
DEPTH = 4

N_MIXERS = 2
N_NSA_LAYERS = (DEPTH + 1) // 2
N_S5_LAYERS = DEPTH // 2

N_HEADS = 16
HEAD_DIM = D_MODEL // N_HEADS
N_KV_GROUPS = 4
HEADS_PER_GROUP = N_HEADS // N_KV_GROUPS
KV_DIM = N_KV_GROUPS * HEAD_DIM
CMP_BLOCK = 32
CMP_STRIDE = 16
CMP_HIDDEN = 256
SEL_BLOCK = 64
SEL_TOPN = 16
WINDOW = 512
Q_BLOCK = 128
N_BRANCHES = 3
PROJ_COLS = D_MODEL + 6 * KV_DIM + N_HEADS * N_BRANCHES

S5_GROUP = 16
S5_GROUPS = D_MODEL // S5_GROUP
S5_STATE = 64

FFN_HIDDEN = ((8 * D_MODEL + 3 * 256 - 1) // (3 * 256)) * 256

EPS = 1e-6
NEG = -1e30
FORCE = 1e4

kernel_name = "hybrid_nsa_s5_interleaved_trunk"


def rmsnorm(x, g):
    xf = x.astype(jnp.float32)
    var = jnp.mean(xf * xf, axis=-1, keepdims=True)
    return xf * lax.rsqrt(var + EPS) * g.astype(jnp.float32)


def masked_softmax(s, mask):
    s = jnp.where(mask, s.astype(jnp.float32), NEG)
    return jax.nn.softmax(s, axis=-1) * mask


def compress_tokens(kv, pos, w1, b1, w2, b2):
    b, t = kv.shape[0], kv.shape[1]
    n_cmp = (t - CMP_BLOCK) // CMP_STRIDE + 1
    idx = jnp.arange(n_cmp)[:, None] * CMP_STRIDE + jnp.arange(CMP_BLOCK)[None, :]
    blocks = kv[:, idx] + pos[:, None, :]
    blocks = blocks.transpose(0, 1, 3, 2, 4).reshape(b, n_cmp, N_KV_GROUPS, CMP_BLOCK * HEAD_DIM)
    hid = jax.nn.gelu(blocks @ w1 + b1)
    return hid @ w2 + b2


def nsa_mixer(h, w_in, w_out, q_gain, k_gain, cmp_pos, cmp_w1, cmp_b1, cmp_w2, cmp_b2):
    b, t, _ = h.shape
    G, R, dh = N_KV_GROUPS, HEADS_PER_GROUP, HEAD_DIM
    scale = 1.0 / math.sqrt(dh)
    proj = h @ w_in
    cuts = [D_MODEL + i * KV_DIM for i in range(7)]
    q, kc_raw, vc_raw, ks_raw, vs_raw, kw_raw, vw_raw, g = jnp.split(proj, cuts, axis=-1)
    q = rmsnorm(q.reshape(b, t, N_HEADS, dh), q_gain)
    kvs = lambda a: a.reshape(b, t, G, dh)
    kc = rmsnorm(compress_tokens(kvs(kc_raw), cmp_pos[0], cmp_w1[0], cmp_b1[0], cmp_w2[0], cmp_b2[0]), k_gain[0])
    vc = compress_tokens(kvs(vc_raw), cmp_pos[1], cmp_w1[1], cmp_b1[1], cmp_w2[1], cmp_b2[1])
    ks = rmsnorm(kvs(ks_raw), k_gain[1])
    vs = kvs(vs_raw)
    kw = rmsnorm(kvs(kw_raw), k_gain[2])
    vw = kvs(vw_raw)
    gates = jax.nn.sigmoid(g.astype(jnp.float32))

    n_cmp = kc.shape[1]
    n_sblk = t // SEL_BLOCK
    top_n = min(SEL_TOPN, n_sblk)
    cmp_start = jnp.arange(n_cmp) * CMP_STRIDE
    cmp_end = cmp_start + CMP_BLOCK - 1
    sblk = jnp.arange(n_sblk)
    overlap = ((cmp_start[:, None] < (sblk[None, :] + 1) * SEL_BLOCK)
               & (cmp_start[:, None] + CMP_BLOCK > sblk[None, :] * SEL_BLOCK)).astype(jnp.float32)
    ks_blocks = ks.reshape(b, n_sblk, SEL_BLOCK, G, dh).transpose(0, 3, 1, 2, 4)
    vs_blocks = vs.reshape(b, n_sblk, SEL_BLOCK, G, dh).transpose(0, 3, 1, 2, 4)
    pad = ((0, 0), (WINDOW, 0), (0, 0), (0, 0))
    kw_pad = jnp.pad(kw, pad)
    vw_pad = jnp.pad(vw, pad)
    b_idx = jnp.arange(b)[:, None, None, None]
    g_idx = jnp.arange(G)[None, :, None, None]

    def block_fn(args):
        j, qb, gb = args
        tq = j * Q_BLOCK + jnp.arange(Q_BLOCK)
        s = jnp.einsum('bqgrd,bngd->bgrqn', qb, kc).astype(jnp.float32) * scale
        p_cmp = masked_softmax(s, cmp_end[None, :] <= tq[:, None])
        o_cmp = jnp.einsum('bgrqn,bngd->bqgrd', p_cmp, vc)
        p_sel = jnp.einsum('bgqn,ns->bgqs', p_cmp.sum(axis=2), overlap)
        cur = tq // SEL_BLOCK
        valid = sblk[None, :] * SEL_BLOCK <= tq[:, None]
        forced = (sblk[None, :] == 0) | (sblk[None, :] == cur[:, None]) | (sblk[None, :] == cur[:, None] - 1)
        score = jnp.where(forced, FORCE, jnp.where(valid, p_sel, NEG))
        vals, idx = lax.top_k(score, top_n)
        sel_ok = vals > 0.5 * NEG
        k_g = ks_blocks[b_idx, g_idx, idx].reshape(b, G, Q_BLOCK, top_n * SEL_BLOCK, dh)
        v_g = vs_blocks[b_idx, g_idx, idx].reshape(b, G, Q_BLOCK, top_n * SEL_BLOCK, dh)
        kpos = (idx[..., None] * SEL_BLOCK + jnp.arange(SEL_BLOCK)).reshape(b, G, Q_BLOCK, top_n * SEL_BLOCK)
        ok = jnp.repeat(sel_ok, SEL_BLOCK, axis=-1) & (kpos <= tq[:, None])
        s = jnp.einsum('bqgrd,bgqkd->bgrqk', qb, k_g).astype(jnp.float32) * scale
        p = masked_softmax(s, ok[:, :, None])
        o_sel = jnp.einsum('bgrqk,bgqkd->bqgrd', p, v_g)
        kb = lax.dynamic_slice_in_dim(kw_pad, j * Q_BLOCK, Q_BLOCK + WINDOW, axis=1)
        vb = lax.dynamic_slice_in_dim(vw_pad, j * Q_BLOCK, Q_BLOCK + WINDOW, axis=1)
        wpos = j * Q_BLOCK - WINDOW + jnp.arange(Q_BLOCK + WINDOW)
        m = (wpos[None, :] >= 0) & (wpos[None, :] <= tq[:, None]) & (tq[:, None] - wpos[None, :] < WINDOW)
        s = jnp.einsum('bqgrd,bkgd->bgrqk', qb, kb).astype(jnp.float32) * scale
        p = masked_softmax(s, m)
        o_win = jnp.einsum('bgrqk,bkgd->bqgrd', p, vb)
        o = gb[..., 0:1] * o_cmp + gb[..., 1:2] * o_sel + gb[..., 2:3] * o_win
        return o.reshape(b, Q_BLOCK, D_MODEL)

    nqb = t // Q_BLOCK
    q_blocks = q.reshape(b, nqb, Q_BLOCK, G, R, dh).transpose(1, 0, 2, 3, 4, 5)
    g_blocks = gates.reshape(b, nqb, Q_BLOCK, G, R, N_BRANCHES).transpose(1, 0, 2, 3, 4, 5)
    o = lax.map(block_fn, (jnp.arange(nqb), q_blocks, g_blocks))
    o = o.transpose(1, 0, 2, 3).reshape(b, t, D_MODEL)
    return o @ w_out


def complex_affine_combine(e1, e2):
    a1r, a1i, b1r, b1i = e1
    a2r, a2i, b2r, b2i = e2
    ar = a2r * a1r - a2i * a1i
    ai = a2r * a1i + a2i * a1r
    br = a2r * b1r - a2i * b1i + b2r
    bi = a2r * b1i + a2i * b1r + b2i
    return (ar, ai, br, bi)


def s5_mixer(h, w_in, b_re, b_im, c_re, c_im, d_skip, log_dt, a_re, a_im, w_glu):
    b, t, _ = h.shape
    f32 = jnp.float32
    u = (h @ w_in).astype(f32).reshape(b, t, S5_GROUPS, S5_GROUP)
    dt = jnp.exp(log_dt.astype(f32))[:, None]
    ar, ai = a_re.astype(f32), a_im.astype(f32)
    mag = jnp.exp(dt * ar)
    abar_r = mag * jnp.cos(dt * ai)
    abar_i = mag * jnp.sin(dt * ai)
    den = ar * ar + ai * ai
    coef_r = ((abar_r - 1.0) * ar + abar_i * ai) / den
    coef_i = (abar_i * ar - (abar_r - 1.0) * ai) / den
    br, bi = b_re.astype(f32), b_im.astype(f32)
    bbar_r = coef_r[..., None] * br - coef_i[..., None] * bi
    bbar_i = coef_r[..., None] * bi + coef_i[..., None] * br
    bu_r = jnp.einsum('btgc,gpc->btgp', u, bbar_r)
    bu_i = jnp.einsum('btgc,gpc->btgp', u, bbar_i)
    a_full_r = jnp.broadcast_to(abar_r, bu_r.shape)
    a_full_i = jnp.broadcast_to(abar_i, bu_r.shape)
    _, _, x_r, x_i = lax.associative_scan(complex_affine_combine, (a_full_r, a_full_i, bu_r, bu_i), axis=1)
    y = (jnp.einsum('btgp,gcp->btgc', x_r, c_re) - jnp.einsum('btgp,gcp->btgc', x_i, c_im)
         + d_skip.reshape(S5_GROUPS, S5_GROUP) * u)
    z = jax.nn.gelu(y.reshape(b, t, D_MODEL))
    vg = z @ w_glu
    return vg[..., :D_MODEL] * jax.nn.sigmoid(vg[..., D_MODEL:])


def swiglu(h, w_gate, w_up, w_down):
    return (jax.nn.silu(h @ w_gate) * (h @ w_up)) @ w_down


def setup_inputs(seed: int = 0) -> dict:
    key = jax.random.key(seed)
    ks = jax.random.split(key, 32)
    nrm = lambda k, s: jax.random.normal(k, s, jnp.float32)
    L, Ln, Ls = DEPTH, N_NSA_LAYERS, N_S5_LAYERS
    D, G, P, C = D_MODEL, S5_GROUPS, S5_STATE, S5_GROUP
    flat = CMP_BLOCK * HEAD_DIM
    return {
        "x": nrm(ks[0], (BATCH, SEQ, D)),
        "mix_norm": 1.0 + 0.01 * nrm(ks[1], (L, D)),
        "ffn_norm": 1.0 + 0.01 * nrm(ks[2], (L, D)),
        "nsa_w_in": nrm(ks[3], (Ln, D, PROJ_COLS)) * D ** -0.5,
        "nsa_w_out": nrm(ks[4], (Ln, D, D)) * D ** -0.5,
        "nsa_q_gain": 1.0 + 0.01 * nrm(ks[5], (Ln, HEAD_DIM)),
        "nsa_k_gain": 1.0 + 0.01 * nrm(ks[6], (Ln, N_BRANCHES, HEAD_DIM)),
        "nsa_cmp_pos": 0.1 * nrm(ks[7], (Ln, 2, CMP_BLOCK, HEAD_DIM)),
        "nsa_cmp_w1": nrm(ks[8], (Ln, 2, flat, CMP_HIDDEN)) * flat ** -0.5,
        "nsa_cmp_b1": 0.01 * nrm(ks[9], (Ln, 2, CMP_HIDDEN)),
        "nsa_cmp_w2": nrm(ks[10], (Ln, 2, CMP_HIDDEN, HEAD_DIM)) * CMP_HIDDEN ** -0.5,
        "nsa_cmp_b2": 0.01 * nrm(ks[11], (Ln, 2, HEAD_DIM)),
        "s5_w_in": nrm(ks[12], (Ls, D, D)) * D ** -0.5,
        "s5_b_re": nrm(ks[13], (Ls, G, P, C)) * (2 * C) ** -0.5,
        "s5_b_im": nrm(ks[14], (Ls, G, P, C)) * (2 * C) ** -0.5,
        "s5_c_re": nrm(ks[15], (Ls, G, C, P)) * (2 * P) ** -0.5,
        "s5_c_im": nrm(ks[16], (Ls, G, C, P)) * (2 * P) ** -0.5,
        "s5_d": nrm(ks[17], (Ls, D)),
        "s5_log_dt": jax.random.uniform(ks[18], (Ls, G), jnp.float32, math.log(0.001), math.log(0.1)),
        "s5_a_re": -0.5 + 0.01 * nrm(ks[19], (Ls, G, P)),
        "s5_a_im": math.pi * jnp.arange(P, dtype=jnp.float32) + 0.01 * nrm(ks[20], (Ls, G, P)),
        "s5_w_glu": nrm(ks[21], (Ls, D, 2 * D)) * D ** -0.5,
        "ffn_w_gate": nrm(ks[22], (L, D, FFN_HIDDEN)) * D ** -0.5,
        "ffn_w_up": nrm(ks[23], (L, D, FFN_HIDDEN)) * D ** -0.5,
        "ffn_w_down": nrm(ks[24], (L, FFN_HIDDEN, D)) * FFN_HIDDEN ** -0.5,
    }


def reference(x, mix_norm, ffn_norm, nsa_w_in, nsa_w_out, nsa_q_gain, nsa_k_gain, nsa_cmp_pos,
              nsa_cmp_w1, nsa_cmp_b1, nsa_cmp_w2, nsa_cmp_b2, s5_w_in, s5_b_re, s5_b_im, s5_c_re,
              s5_c_im, s5_d, s5_log_dt, s5_a_re, s5_a_im, s5_w_glu, ffn_w_gate, ffn_w_up, ffn_w_down):
    for layer in range(DEPTH):
        h = rmsnorm(x, mix_norm[layer])
        i = layer // N_MIXERS
        if layer % N_MIXERS == 0:
            mix = nsa_mixer(h, nsa_w_in[i], nsa_w_out[i], nsa_q_gain[i], nsa_k_gain[i], nsa_cmp_pos[i],
                            nsa_cmp_w1[i], nsa_cmp_b1[i], nsa_cmp_w2[i], nsa_cmp_b2[i])
        else:
            mix = s5_mixer(h, s5_w_in[i], s5_b_re[i], s5_b_im[i], s5_c_re[i], s5_c_im[i], s5_d[i],
                           s5_log_dt[i], s5_a_re[i], s5_a_im[i], s5_w_glu[i])
        x = x + mix.astype(x.dtype)
        h = rmsnorm(x, ffn_norm[layer])
        x = x + swiglu(h, ffn_w_gate[layer], ffn_w_up[layer], ffn_w_down[layer]).astype(x.dtype)
    return x
```

```python
import functools
import math

import jax
import jax.numpy as jnp
from jax import lax
from jax.experimental import pallas as pl
from jax.experimental.pallas import tpu as pltpu

F32 = jnp.float32
BF16 = jnp.bfloat16

EPS = 1e-6
NEG = -1e30
FORCE = 1e4
MASK_BIAS = -1e9

N_HEADS = 16
HEAD_DIM = 64
N_KV_GROUPS = 4
HEADS_PER_GROUP = 4
KV_DIM = N_KV_GROUPS * HEAD_DIM
CMP_BLOCK = 32
CMP_STRIDE = 16
SEL_BLOCK = 64
SEL_TOPN = 16
WINDOW = 512
N_BRANCHES = 3
S5_GROUP = 16
S5_STATE = 64
S5_CHUNK = 16

V7X_VMEM_LIMIT_BYTES = 56 * 1024 * 1024


def _cparams(*semantics):
    return pltpu.CompilerParams(dimension_semantics=semantics,
                                vmem_limit_bytes=V7X_VMEM_LIMIT_BYTES)


def _rms_rows(x, g):
    var = jnp.mean(x * x, axis=-1, keepdims=True)
    return x * lax.rsqrt(var + EPS) * g


def _dot(a, b):
    return jnp.dot(a, b, preferred_element_type=F32)


def _dot_nt(a, b):
    return lax.dot_general(a, b, (((1,), (1,)), ((), ())), preferred_element_type=F32)


def _seg_sumsq(v, ebd):
    sq = v * v
    hi = sq.astype(BF16)
    lo = (sq - hi.astype(F32)).astype(BF16)
    return _dot(hi, ebd) + _dot(lo, ebd)


def _nsa_proj_kernel(x_ref, g_ref, wq_ref, wkv_ref, wg_ref, qg_ref, kg_ref, ebd_ref,
                     q_ref, kcr_ref, vcr_ref, ks_ref, vs_ref, kw_ref, vw_ref, gate_ref):
    h = _rms_rows(x_ref[...], g_ref[...]).astype(BF16)
    ebd = ebd_ref[...]
    inv_dh = 1.0 / HEAD_DIM

    q = _dot(h, wq_ref[...])
    qg = qg_ref[...]
    for i in range(N_KV_GROUPS):
        sl = slice(i * KV_DIM, (i + 1) * KV_DIM)
        qi = q[:, sl]
        var = _seg_sumsq(qi, ebd) * inv_dh
        q_ref[:, sl] = (qi * lax.rsqrt(var + EPS) * qg[:, sl]).astype(BF16)

    kv = _dot(h, wkv_ref[...])
    kg = kg_ref[...]
    kcr_ref[...] = kv[:, 0 * KV_DIM:1 * KV_DIM]
    vcr_ref[...] = kv[:, 1 * KV_DIM:2 * KV_DIM]
    ks = kv[:, 2 * KV_DIM:3 * KV_DIM]
    ks_ref[...] = (ks * lax.rsqrt(_seg_sumsq(ks, ebd) * inv_dh + EPS) * kg[0:1]).astype(BF16)
    vs_ref[...] = kv[:, 3 * KV_DIM:4 * KV_DIM].astype(BF16)
    kw = kv[:, 4 * KV_DIM:5 * KV_DIM]
    kw_ref[...] = (kw * lax.rsqrt(_seg_sumsq(kw, ebd) * inv_dh + EPS) * kg[1:2]).astype(BF16)
    vw_ref[...] = kv[:, 5 * KV_DIM:6 * KV_DIM].astype(BF16)

    gate_ref[...] = jax.nn.sigmoid(_dot(h, wg_ref[...]))


def _nsa_proj(x2, g, wq, wkv, wg, qg_t, kg_t, ebd, tm=512):
    m, d = x2.shape
    full = lambda a: pl.BlockSpec(a.shape, lambda i: (0,) * a.ndim)
    row = lambda n: pl.BlockSpec((tm, n), lambda i: (i, 0))
    outs = [jax.ShapeDtypeStruct((m, d), BF16),
            jax.ShapeDtypeStruct((m, KV_DIM), F32), jax.ShapeDtypeStruct((m, KV_DIM), F32),
            jax.ShapeDtypeStruct((m, KV_DIM), BF16), jax.ShapeDtypeStruct((m, KV_DIM), BF16),
            jax.ShapeDtypeStruct((m, KV_DIM), BF16), jax.ShapeDtypeStruct((m, KV_DIM), BF16),
            jax.ShapeDtypeStruct((m, 128), F32)]
    return pl.pallas_call(
        _nsa_proj_kernel, grid=(m // tm,),
        in_specs=[row(d), full(g), full(wq), full(wkv), full(wg), full(qg_t), full(kg_t), full(ebd)],
        out_specs=[row(d)] + [row(KV_DIM)] * 6 + [row(128)],
        out_shape=outs, compiler_params=_cparams("parallel"), name="nsa_proj",
    )(x2, g, wq, wkv, wg, qg_t, kg_t, ebd)


def _compress_kernel(a_ref, pos_ref, w1a_ref, w1b_ref, b1_ref, w2_ref, b2_ref, gain_ref, o_ref, *, norm):
    a = a_ref[0, 0]
    tc = a.shape[0]
    pos = pos_ref[...]
    h1 = _dot((a + pos[0:1]).astype(BF16), w1a_ref[...])
    h2 = _dot((a + pos[1:2]).astype(BF16), w1b_ref[...])
    h = h1 + pltpu.roll(h2, tc - 1, 0) + b1_ref[...]
    hid = jax.nn.gelu(h).astype(BF16)
    out = _dot(hid, w2_ref[...]) + b2_ref[...]
    if norm:
        out = _rms_rows(out, gain_ref[...])
    o_ref[0, 0] = out.astype(o_ref.dtype)


def _compress(a, pos2, w1a, w1b, b1, w2, b2, gain, norm):
    b, g, tc, w = a.shape
    full = lambda z: pl.BlockSpec(z.shape, lambda i, j: (0,) * z.ndim)
    return pl.pallas_call(
        functools.partial(_compress_kernel, norm=norm), grid=(b, g),
        in_specs=[pl.BlockSpec((1, 1, tc, w), lambda i, j: (i, j, 0, 0)),
                  full(pos2), full(w1a), full(w1b), full(b1), full(w2), full(b2), full(gain)],
        out_specs=pl.BlockSpec((1, 1, tc, HEAD_DIM), lambda i, j: (i, j, 0, 0)),
        out_shape=jax.ShapeDtypeStruct((b, g, tc, HEAD_DIM), BF16),
        compiler_params=_cparams("parallel", "parallel"), name="nsa_compress",
    )(a, pos2, w1a, w1b, b1, w2, b2, gain)


def _cmp_select_kernel(qt_ref, kc_ref, vct_ref, ovt_ref, ocmp_ref, sel_ref, *, top_n):
    tq = qt_ref.shape[-1]
    tc = kc_ref.shape[2]
    nsb = ovt_ref.shape[0]
    t = pl.program_id(1) * tq + lax.broadcasted_iota(jnp.int32, (1, tq), 1)
    n_idx = lax.broadcasted_iota(jnp.int32, (tc, 1), 0)
    vis = (n_idx * CMP_STRIDE + (CMP_BLOCK - 1)) <= t
    j_idx = lax.broadcasted_iota(jnp.int32, (nsb, 1), 0)
    cur = t // SEL_BLOCK
    forced = (j_idx == 0) | (j_idx == cur) | (j_idx == cur - 1)
    valid = j_idx <= cur
    j_f = j_idx.astype(F32)
    ovt = ovt_ref[...]

    for g in range(N_KV_GROUPS):
        kc = kc_ref[0, g]
        vct = vct_ref[0, g]
        psum = jnp.zeros((tc, tq), F32)
        for r in range(HEADS_PER_GROUP):
            s = _dot(kc, qt_ref[0, g, r])
            s = jnp.where(vis, s, NEG)
            mx = jnp.max(s, axis=0, keepdims=True)
            e = jnp.where(vis, jnp.exp(s - mx), 0.0)
            den = jnp.sum(e, axis=0, keepdims=True)
            p = e * jnp.where(den > 0.0, 1.0 / den, 0.0)
            o_t = _dot(vct, p.astype(BF16))
            hd = (g * HEADS_PER_GROUP + r) * HEAD_DIM
            ocmp_ref[0, :, hd:hd + HEAD_DIM] = o_t.T
            psum = psum + p
        hi = psum.astype(BF16)
        lo = (psum - hi.astype(F32)).astype(BF16)
        psel = _dot(ovt, hi) + _dot(ovt, lo)
        score = jnp.where(forced, FORCE, jnp.where(valid, psel, NEG))
        sel = jnp.zeros((nsb, tq), F32)
        for _ in range(top_n):
            mx = jnp.max(score, axis=0, keepdims=True)
            first = jnp.min(jnp.where(score == mx, j_f, float(nsb)), axis=0, keepdims=True)
            pick = j_f == first
            sel = jnp.where(pick & (mx > 0.5 * NEG), 1.0, sel)
            score = jnp.where(pick, -jnp.inf, score)
        sel_ref[0, g] = sel.T.astype(BF16)


def _cmp_select(qt, kc, vct, ovt, t_len, tq=256):
    b, g, r, dh, _ = qt.shape
    tc = kc.shape[2]
    nsb = ovt.shape[0]
    top_n = min(SEL_TOPN, nsb)
    return pl.pallas_call(
        functools.partial(_cmp_select_kernel, top_n=top_n), grid=(b, t_len // tq),
        in_specs=[pl.BlockSpec((1, g, r, dh, tq), lambda i, j: (i, 0, 0, 0, j)),
                  pl.BlockSpec((1, g, tc, dh), lambda i, j: (i, 0, 0, 0)),
                  pl.BlockSpec((1, g, dh, tc), lambda i, j: (i, 0, 0, 0)),
                  pl.BlockSpec(ovt.shape, lambda i, j: (0, 0))],
        out_specs=[pl.BlockSpec((1, tq, N_HEADS * HEAD_DIM), lambda i, j: (i, j, 0)),
                   pl.BlockSpec((1, g, tq, nsb), lambda i, j: (i, 0, j, 0))],
        out_shape=[jax.ShapeDtypeStruct((b, t_len, N_HEADS * HEAD_DIM), F32),
                   jax.ShapeDtypeStruct((b, g, t_len, nsb), BF16)],
        compiler_params=_cparams("parallel", "parallel"), name="nsa_cmp_select",
    )(qt, kc, vct, ovt)


def _sel_win_kernel(q_ref, kvs_ref, kvw_ref, e_ref, sel_ref, ocmp_ref, gate_ref, o_ref, *, kc):
    r_heads, tq, dh = q_ref.shape[2], q_ref.shape[3], q_ref.shape[4] // 2
    rows = r_heads * tq
    t0 = pl.program_id(2) * tq
    qp = q_ref[0, 0].reshape(rows, 2 * dh)
    sel_t = sel_ref[0, 0]
    t_col = t0 + lax.broadcasted_iota(jnp.int32, (tq, 1), 0)

    def chunk(c, carry, causal):
        m, l, acc = carry
        k0 = pl.multiple_of(c * kc, kc)
        kv = kvs_ref[0, 0, pl.ds(k0, kc), :]
        bias = (_dot(sel_t, e_ref[:, pl.ds(k0, kc)]) - 1.0) * (-MASK_BIAS)
        if causal:
            kpos = k0 + lax.broadcasted_iota(jnp.int32, (1, kc), 1)
            bias = jnp.where(kpos <= t_col, bias, MASK_BIAS)
        s = _dot_nt(qp, kv) + jnp.concatenate([bias] * r_heads, axis=0)
        m_new = jnp.maximum(m, jnp.max(s, axis=-1, keepdims=True))
        alpha = jnp.exp(m - m_new)
        p = jnp.exp(s - m_new)
        psum = p[:, 0:128]
        for i in range(1, kc // 128):
            psum = psum + p[:, i * 128:(i + 1) * 128]
        l = alpha * l + psum
        acc = alpha * acc + _dot(p.astype(BF16), kv)
        return m_new, l, acc

    init = (jnp.full((rows, 1), NEG, F32), jnp.zeros((rows, 128), F32), jnp.zeros((rows, 128), F32))
    n_full = t0 // kc
    carry = lax.fori_loop(0, n_full, lambda c, cr: chunk(c, cr, False), init)
    _, l, acc = chunk(n_full, carry, True)
    o_sel = acc[:, dh:] / jnp.sum(l, axis=-1, keepdims=True)

    span = tq + WINDOW
    kvw = kvw_ref[0, 0, pl.ds(pl.multiple_of(t0, tq), span), :]
    rel = lax.broadcasted_iota(jnp.int32, (1, span), 1)
    tr = lax.broadcasted_iota(jnp.int32, (tq, 1), 0)
    in_win = (rel > tr) & (rel <= tr + WINDOW) & (rel + t0 >= WINDOW)
    bias_w = jnp.where(in_win, 0.0, MASK_BIAS)
    s = _dot_nt(qp, kvw) + jnp.concatenate([bias_w] * r_heads, axis=0)
    p = jnp.exp(s - jnp.max(s, axis=-1, keepdims=True))
    o_win = _dot(p.astype(BF16), kvw)[:, dh:] / jnp.sum(p, axis=-1, keepdims=True)

    gates = gate_ref[0, 0]
    ocmp = ocmp_ref[0]
    for r in range(r_heads):
        rs = slice(r * tq, (r + 1) * tq)
        o_r = (gates[:, 3 * r:3 * r + 1] * ocmp[:, r * dh:(r + 1) * dh]
               + gates[:, 3 * r + 1:3 * r + 2] * o_sel[rs]
               + gates[:, 3 * r + 2:3 * r + 3] * o_win[rs])
        o_ref[0, :, r * dh:(r + 1) * dh] = o_r.astype(o_ref.dtype)


def _sel_win(q_t, kvs, kvw, e_mat, sel, ocmp, gates_g, tq=128, kc=512):
    b, g, r, t_len, dh2 = q_t.shape
    dh = dh2 // 2
    nsb = sel.shape[-1]
    kc = min(kc, t_len)
    gw = r * dh
    return pl.pallas_call(
        functools.partial(_sel_win_kernel, kc=kc), grid=(b, g, t_len // tq),
        in_specs=[pl.BlockSpec((1, 1, r, tq, dh2), lambda i, j, k: (i, j, 0, k, 0)),
                  pl.BlockSpec((1, 1, t_len, 2 * dh), lambda i, j, k: (i, j, 0, 0)),
                  pl.BlockSpec((1, 1, t_len + WINDOW, 2 * dh), lambda i, j, k: (i, j, 0, 0)),
                  pl.BlockSpec(e_mat.shape, lambda i, j, k: (0, 0)),
                  pl.BlockSpec((1, 1, tq, nsb), lambda i, j, k: (i, j, k, 0)),
                  pl.BlockSpec((1, tq, gw), lambda i, j, k: (i, k, j)),
                  pl.BlockSpec((1, 1, tq, 128), lambda i, j, k: (i, j, k, 0))],
        out_specs=pl.BlockSpec((1, tq, gw), lambda i, j, k: (i, k, j)),
        out_shape=jax.ShapeDtypeStruct((b, t_len, g * gw), BF16),
        compiler_params=_cparams("parallel", "parallel", "arbitrary"), name="nsa_sel_win",
    )(q_t, kvs, kvw, e_mat, sel, ocmp, gates_g)


def _res_matmul_kernel(a_ref, w_ref, x_ref, o_ref):
    o_ref[...] = x_ref[...] + _dot(a_ref[...], w_ref[...])


def _res_matmul(a, w, x2, tm=512):
    m, k = a.shape
    n = w.shape[1]
    return pl.pallas_call(
        _res_matmul_kernel, grid=(m // tm,),
        in_specs=[pl.BlockSpec((tm, k), lambda i: (i, 0)), pl.BlockSpec((k, n), lambda i: (0, 0)),
                  pl.BlockSpec((tm, n), lambda i: (i, 0))],
        out_specs=pl.BlockSpec((tm, n), lambda i: (i, 0)),
        out_shape=jax.ShapeDtypeStruct((m, n), F32),
        compiler_params=_cparams("parallel"), name="res_matmul",
    )(a, w, x2)


def _ffn_kernel(x_ref, g_ref, wg_ref, wu_ref, wd_ref, o_ref, *, hc):
    x = x_ref[...]
    h = _rms_rows(x, g_ref[...]).astype(BF16)
    acc = x
    for c in range(wg_ref.shape[1] // hc):
        sl = slice(c * hc, (c + 1) * hc)
        a = jax.nn.silu(_dot(h, wg_ref[:, sl])) * _dot(h, wu_ref[:, sl])
        acc = acc + _dot(a.astype(BF16), wd_ref[sl, :])
    o_ref[...] = acc


def _ffn(x2, g, wg, wu, wd, tm=512, hc=256):
    m, d = x2.shape
    full = lambda a: pl.BlockSpec(a.shape, lambda i: (0,) * a.ndim)
    return pl.pallas_call(
        functools.partial(_ffn_kernel, hc=hc), grid=(m // tm,),
        in_specs=[pl.BlockSpec((tm, d), lambda i: (i, 0)), full(g), full(wg), full(wu), full(wd)],
        out_specs=pl.BlockSpec((tm, d), lambda i: (i, 0)),
        out_shape=jax.ShapeDtypeStruct((m, d), F32),
        compiler_params=_cparams("parallel"), name="ffn_swiglu",
    )(x2, g, wg, wu, wd)


def _norm_matmul_kernel(x_ref, g_ref, w_ref, o_ref):
    o_ref[...] = _dot(_rms_rows(x_ref[...], g_ref[...]).astype(BF16), w_ref[...])


def _norm_matmul(x2, g, w, tm=512):
    m, d = x2.shape
    n = w.shape[1]
    return pl.pallas_call(
        _norm_matmul_kernel, grid=(m // tm,),
        in_specs=[pl.BlockSpec((tm, d), lambda i: (i, 0)), pl.BlockSpec(g.shape, lambda i: (0, 0)),
                  pl.BlockSpec((d, n), lambda i: (0, 0))],
        out_specs=pl.BlockSpec((tm, n), lambda i: (i, 0)),
        out_shape=jax.ShapeDtypeStruct((m, n), F32),
        compiler_params=_cparams("parallel"), name="norm_matmul",
    )(x2, g, w)


def _dot_f32(a, b):
    return jnp.dot(a, b, preferred_element_type=F32, precision=lax.Precision.HIGHEST)


def _s5_state_in_kernel(u_ref, mz_ref, z_ref):
    z_ref[0] = _dot_f32(u_ref[0], mz_ref[0])


def _s5_state_in(ug, mz):
    g, rows, w = ug.shape
    n = mz.shape[-1]
    return pl.pallas_call(
        _s5_state_in_kernel, grid=(g,),
        in_specs=[pl.BlockSpec((1, rows, w), lambda i: (i, 0, 0)), pl.BlockSpec((1, w, n), lambda i: (i, 0, 0))],
        out_specs=pl.BlockSpec((1, rows, n), lambda i: (i, 0, 0)),
        out_shape=jax.ShapeDtypeStruct((g, rows, n), F32),
        compiler_params=_cparams("parallel"), name="s5_state_in",
    )(ug, mz)


def _s5_scan_kernel(z_ref, a_ref, x_ref, st_ref):
    @pl.when(pl.program_id(0) == 0)
    def _():
        st_ref[...] = jnp.zeros_like(st_ref)

    ar, ai = a_ref[0], a_ref[1]

    def step(k, carry):
        xr, xi = carry
        x_ref[k, 0] = xr
        x_ref[k, 1] = xi
        zr, zi = z_ref[k, 0], z_ref[k, 1]
        return ar * xr - ai * xi + zr, ar * xi + ai * xr + zi

    xr, xi = lax.fori_loop(0, z_ref.shape[0], step, (st_ref[0], st_ref[1]))
    st_ref[0] = xr
    st_ref[1] = xi


def _s5_scan(zt, a16, tk=128):
    nc = zt.shape[0]
    tk = min(tk, nc)
    blk = (tk,) + zt.shape[1:]
    return pl.pallas_call(
        _s5_scan_kernel, grid=(nc // tk,),
        in_specs=[pl.BlockSpec(blk, lambda i: (i, 0, 0, 0)), pl.BlockSpec(a16.shape, lambda i: (0, 0, 0))],
        out_specs=pl.BlockSpec(blk, lambda i: (i, 0, 0, 0)),
        out_shape=jax.ShapeDtypeStruct(zt.shape, F32),
        scratch_shapes=[pltpu.VMEM(zt.shape[1:], F32)],
        compiler_params=_cparams("arbitrary"), name="s5_scan",
    )(zt, a16)


def _s5_out_kernel(u_ref, xin_ref, m_ref, mc_ref, d_ref, z_ref):
    u = u_ref[0]
    y = _dot_f32(u, m_ref[0]) + _dot_f32(xin_ref[0], mc_ref[0]) + d_ref[0] * u
    z_ref[0] = jax.nn.gelu(y).astype(z_ref.dtype)


def _s5_out(ug, xin, m_mat, mc, d_t):
    g, rows, w = ug.shape
    n = xin.shape[-1]
    return pl.pallas_call(
        _s5_out_kernel, grid=(g,),
        in_specs=[pl.BlockSpec((1, rows, w), lambda i: (i, 0, 0)), pl.BlockSpec((1, rows, n), lambda i: (i, 0, 0)),
                  pl.BlockSpec((1, w, w), lambda i: (i, 0, 0)), pl.BlockSpec((1, n, w), lambda i: (i, 0, 0)),
                  pl.BlockSpec((1, 1, w), lambda i: (i, 0, 0))],
        out_specs=pl.BlockSpec((1, rows, w), lambda i: (i, 0, 0)),
        out_shape=jax.ShapeDtypeStruct((g, rows, w), BF16),
        compiler_params=_cparams("parallel"), name="s5_out",
    )(ug, xin, m_mat, mc, d_t)


def _glu_res_kernel(z_ref, w_ref, x_ref, o_ref):
    d = x_ref.shape[1]
    vg = _dot(z_ref[...], w_ref[...])
    o_ref[...] = x_ref[...] + vg[:, :d] * jax.nn.sigmoid(vg[:, d:])


def _glu_res(z, w, x2, tm=512):
    m, d = x2.shape
    return pl.pallas_call(
        _glu_res_kernel, grid=(m // tm,),
        in_specs=[pl.BlockSpec((tm, d), lambda i: (i, 0)), pl.BlockSpec(w.shape, lambda i: (0, 0)),
                  pl.BlockSpec((tm, d), lambda i: (i, 0))],
        out_specs=pl.BlockSpec((tm, d), lambda i: (i, 0)),
        out_shape=jax.ShapeDtypeStruct((m, d), F32),
        compiler_params=_cparams("parallel"), name="s5_glu_res",
    )(z, w, x2)


def _nsa_layer(x, g_norm, w_in, w_out, q_gain, k_gain, cmp_pos, cmp_w1, cmp_b1, cmp_w2, cmp_b2):
    b, t, d = x.shape
    gn, rh, dh = N_KV_GROUPS, HEADS_PER_GROUP, HEAD_DIM
    x2 = x.reshape(b * t, d)
    scale = 1.0 / math.sqrt(dh)

    wq = w_in[:, :d].astype(BF16)
    wkv = w_in[:, d:d + 6 * KV_DIM].astype(BF16)
    wg_raw = w_in[:, d + 6 * KV_DIM:]
    wg = jnp.pad(wg_raw, ((0, 0), (0, 128 - wg_raw.shape[1]))).astype(BF16)
    qg_t = jnp.tile(q_gain * scale, N_HEADS)[None, :]
    kg_t = jnp.stack([jnp.tile(k_gain[1], gn), jnp.tile(k_gain[2], gn)])
    seg = jnp.arange(KV_DIM) // dh
    ebd = (seg[:, None] == seg[None, :]).astype(BF16)

    qn, kc_raw, vc_raw, ksn, vs, kwn, vw, gates = _nsa_proj(x2, g_norm[None, :], wq, wkv, wg, qg_t, kg_t, ebd)

    q5 = qn.reshape(b, t, gn, rh, dh)
    q_t = jnp.pad(q5.transpose(0, 2, 3, 1, 4), ((0, 0),) * 4 + ((0, dh),))
    qt = q5.transpose(0, 2, 3, 4, 1)
    hm = lambda a: a.reshape(b, t, gn, dh).transpose(0, 2, 1, 3)
    kvs = jnp.concatenate([hm(ksn), hm(vs)], axis=-1)
    kvw = jnp.pad(jnp.concatenate([hm(kwn), hm(vw)], axis=-1), ((0, 0), (0, 0), (WINDOW, 0), (0, 0)))
    gates_g = jnp.pad(gates[:, :gn * rh * N_BRANCHES].reshape(b, t, gn, rh * N_BRANCHES).transpose(0, 2, 1, 3),
                      ((0, 0), (0, 0), (0, 0), (0, 128 - rh * N_BRANCHES)))

    tc = t // CMP_STRIDE
    half = CMP_STRIDE * dh
    blk = lambda a: a.reshape(b, tc, CMP_STRIDE, gn, dh).transpose(0, 3, 1, 2, 4).reshape(b, gn, tc, half)
    comp = []
    for i, (raw, norm) in enumerate(((kc_raw, True), (vc_raw, False))):
        pos2 = cmp_pos[i].reshape(2, half)
        w1 = cmp_w1[i].astype(BF16)
        comp.append(_compress(blk(raw), pos2, w1[:half], w1[half:], cmp_b1[i][None, :],
                              cmp_w2[i].astype(BF16), cmp_b2[i][None, :], k_gain[0][None, :], norm))
    kc, vc = comp
    vct = vc.transpose(0, 1, 3, 2)

    nsb = t // SEL_BLOCK
    n_i = jnp.arange(tc)
    j_i = jnp.arange(nsb)
    n_cmp = (t - CMP_BLOCK) // CMP_STRIDE + 1
    ovt = ((n_i[None, :] * CMP_STRIDE < (j_i[:, None] + 1) * SEL_BLOCK)
           & (n_i[None, :] * CMP_STRIDE + CMP_BLOCK > j_i[:, None] * SEL_BLOCK)
           & (n_i[None, :] < n_cmp)).astype(BF16)
    ocmp, sel = _cmp_select(qt, kc, vct, ovt, t, tq=min(256, t))

    e_mat = (j_i[:, None] == (jnp.arange(t)[None, :] // SEL_BLOCK)).astype(BF16)
    o = _sel_win(q_t, kvs, kvw, e_mat, sel, ocmp, gates_g)
    return _res_matmul(o.reshape(b * t, d), w_out.astype(BF16), x2).reshape(b, t, d)


def _s5_operators(b_re, b_im, c_re, c_im, log_dt, a_re, a_im):
    lc = S5_CHUNK
    dt = jnp.exp(log_dt)[:, None]
    mag = jnp.exp(dt * a_re)
    abar_r = mag * jnp.cos(dt * a_im)
    abar_i = mag * jnp.sin(dt * a_im)
    den = a_re * a_re + a_im * a_im
    coef_r = ((abar_r - 1.0) * a_re + abar_i * a_im) / den
    coef_i = (abar_i * a_re - (abar_r - 1.0) * a_im) / den
    bbar_r = coef_r[..., None] * b_re - coef_i[..., None] * b_im
    bbar_i = coef_r[..., None] * b_im + coef_i[..., None] * b_re
    tau = jnp.arange(lc + 1, dtype=F32)[:, None, None]
    pmag = jnp.exp(tau * (dt * a_re)[None])
    pw_r = pmag * jnp.cos(tau * (dt * a_im)[None])
    pw_i = pmag * jnp.sin(tau * (dt * a_im)[None])
    hp = lax.Precision.HIGHEST
    cp_r = c_re[None] * pw_r[:, :, None, :] - c_im[None] * pw_i[:, :, None, :]
    cp_i = c_re[None] * pw_i[:, :, None, :] + c_im[None] * pw_r[:, :, None, :]
    kern = (jnp.einsum('tgcp,gpd->tgcd', cp_r, bbar_r, precision=hp)
            - jnp.einsum('tgcp,gpd->tgcd', cp_i, bbar_i, precision=hp))
    s_i = jnp.arange(lc)
    lag = s_i[None, :] - s_i[:, None]
    kt = jnp.where((lag >= 0)[:, :, None, None, None], kern[jnp.clip(lag, 0, lc)], 0.0)
    m_mat = kt.transpose(2, 0, 4, 1, 3).reshape(-1, lc * S5_GROUP, lc * S5_GROUP)
    rev_r, rev_i = pw_r[lc - 1 - s_i], pw_i[lc - 1 - s_i]
    zr = rev_r[..., None] * bbar_r[None] - rev_i[..., None] * bbar_i[None]
    zi = rev_r[..., None] * bbar_i[None] + rev_i[..., None] * bbar_r[None]
    mz = jnp.concatenate([zr, zi], axis=2).transpose(1, 0, 3, 2).reshape(-1, lc * S5_GROUP, 2 * S5_STATE)
    mc = jnp.concatenate([cp_r[1:], -cp_i[1:]], axis=3)
    mc = mc.transpose(1, 3, 0, 2).reshape(-1, 2 * S5_STATE, lc * S5_GROUP)
    return m_mat, mz, mc, pw_r[lc], pw_i[lc]


def _s5_layer(x, g_norm, w_in, b_re, b_im, c_re, c_im, d_skip, log_dt, a_re, a_im, w_glu):
    b, t, d = x.shape
    lc, cg, ps = S5_CHUNK, S5_GROUP, S5_STATE
    ng = d // cg
    nc = t // lc
    x2 = x.reshape(b * t, d)
    u = _norm_matmul(x2, g_norm[None, :], w_in.astype(BF16))
    m_mat, mz, mc, a16_r, a16_i = _s5_operators(b_re, b_im, c_re, c_im, log_dt, a_re, a_im)
    ug = u.reshape(b, nc, lc, ng, cg).transpose(3, 0, 1, 2, 4).reshape(ng, b * nc, lc * cg)
    z = _s5_state_in(ug, mz)
    zt = z.reshape(ng, b, nc, 2, ps).transpose(2, 3, 0, 1, 4).reshape(nc, 2, ng, b * ps)
    a16 = jnp.stack([jnp.tile(a16_r, (1, b)), jnp.tile(a16_i, (1, b))])
    xin_t = _s5_scan(zt, a16)
    xin = xin_t.reshape(nc, 2, ng, b, ps).transpose(2, 3, 0, 1, 4).reshape(ng, b * nc, 2 * ps)
    d_t = jnp.tile(d_skip.reshape(ng, 1, cg), (1, 1, lc))
    zact = _s5_out(ug, xin, m_mat, mc, d_t)
    zz = zact.reshape(ng, b, nc, lc, cg).transpose(1, 2, 3, 0, 4).reshape(b * t, d)
    return _glu_res(zz, w_glu.astype(BF16), x2).reshape(b, t, d)


def kernel(x, mix_norm, ffn_norm, nsa_w_in, nsa_w_out, nsa_q_gain, nsa_k_gain, nsa_cmp_pos, nsa_cmp_w1, nsa_cmp_b1, nsa_cmp_w2, nsa_cmp_b2, s5_w_in, s5_b_re, s5_b_im, s5_c_re, s5_c_im, s5_d, s5_log_dt, s5_a_re, s5_a_im, s5_w_glu, ffn_w_gate, ffn_w_up, ffn_w_down):
    b, t, d = x.shape
    depth = mix_norm.shape[0]
    for layer in range(depth):
        i = layer // 2
        if layer % 2 == 0:
            x = _nsa_layer(x, mix_norm[layer], nsa_w_in[i], nsa_w_out[i], nsa_q_gain[i], nsa_k_gain[i],
                           nsa_cmp_pos[i], nsa_cmp_w1[i], nsa_cmp_b1[i], nsa_cmp_w2[i], nsa_cmp_b2[i])
        else:
            x = _s5_layer(x, mix_norm[layer], s5_w_in[i], s5_b_re[i], s5_b_im[i], s5_c_re[i], s5_c_im[i],
                          s5_d[i], s5_log_dt[i], s5_a_re[i], s5_a_im[i], s5_w_glu[i])
        x = _ffn(x.reshape(b * t, d), ffn_norm[layer][None, :], ffn_w_gate[layer].astype(BF16),
                 ffn_w_up[layer].astype(BF16), ffn_w_down[layer].astype(BF16)).reshape(b, t, d)
    return x
```

```python
import functools
import math

import jax
import jax.numpy as jnp
from jax import lax
from jax.experimental import pallas as pl
from jax.experimental.pallas import tpu as pltpu

F32 = jnp.float32
BF16 = jnp.bfloat16

EPS = 1e-6
NEG = -1e30
FORCE = 1e4
MASK_BIAS = -1e9
LOG2E = 1.4426950408889634

N_HEADS = 16
HEAD_DIM = 64
N_KV_GROUPS = 4
HEADS_PER_GROUP = 4
KV_DIM = N_KV_GROUPS * HEAD_DIM
CMP_BLOCK = 32
CMP_STRIDE = 16
SEL_BLOCK = 64
SEL_TOPN = 16
WINDOW = 512
N_BRANCHES = 3
S5_GROUP = 16
S5_STATE = 64

LANES = 128
SUBLANES = 8
MXU_DIM = 256
SEL_CHUNK = 512
V7X_VMEM_LIMIT_BYTES = 56 * 1024 * 1024


def _cparams(*semantics):
    return pltpu.CompilerParams(dimension_semantics=semantics,
                                vmem_limit_bytes=V7X_VMEM_LIMIT_BYTES)


def _rms_rows(x, g):
    var = jnp.mean(x * x, axis=-1, keepdims=True)
    return x * lax.rsqrt(var + EPS) * g


def _dot(a, b):
    return jnp.dot(a, b, preferred_element_type=F32)


def _dot_nt(a, b):
    return lax.dot_general(a, b, (((1,), (1,)), ((), ())), preferred_element_type=F32)


def _seg_sumsq(v, ebd):
    sq = v * v
    hi = sq.astype(BF16)
    lo = (sq - hi.astype(F32)).astype(BF16)
    return _dot(hi, ebd) + _dot(lo, ebd)


def _nsa_proj_kernel(x_ref, g_ref, wq_ref, wk_ref, wc_ref, wg_ref, qg_ref, kg_ref, ebd_ref,
                     q_ref, ka_ref, va_ref, kw_ref, vw_ref, kcr_ref, vcr_ref, gate_ref):
    tm = x_ref.shape[0]
    h = _rms_rows(x_ref[...], g_ref[...]).astype(BF16)
    ebd = ebd_ref[...]
    inv_dh = 1.0 / HEAD_DIM
    slab = 2 * LANES

    q = _dot(h, wq_ref[...])
    qg = qg_ref[...]
    for pair in range(N_HEADS // 2):
        sl = slice(pair * slab, (pair + 1) * slab)
        qi = q[:, sl]
        qn = (qi * lax.rsqrt(_seg_sumsq(qi, ebd) * inv_dh + EPS) * qg[:, sl]).astype(BF16)
        for hh in range(2):
            g, r = divmod(2 * pair + hh, HEADS_PER_GROUP)
            q_ref[0, g, r] = qn[:, hh * LANES:(hh + 1) * LANES]

    row = lax.broadcasted_iota(jnp.int32, (tm, 1), 0)
    lane = lax.broadcasted_iota(jnp.int32, (1, LANES), 1)
    blk_flag = ((row // SEL_BLOCK) == (lane - HEAD_DIM)).astype(F32)
    one_flag = (lane == HEAD_DIM).astype(F32)

    kv = _dot(h, wk_ref[...])
    kg = kg_ref[...]
    plan = ((ka_ref, 0, blk_flag), (va_ref, None, one_flag), (kw_ref, 1, None), (vw_ref, None, one_flag))
    for part, (ref, gain_row, flag) in enumerate(plan):
        for half in range(2):
            base = part * N_KV_GROUPS * LANES + half * slab
            v = kv[:, base:base + slab]
            if gain_row is not None:
                v = v * lax.rsqrt(_seg_sumsq(v, ebd) * inv_dh + EPS) * kg[gain_row:gain_row + 1, half * slab:(half + 1) * slab]
            for hh in range(2):
                piece = v[:, hh * LANES:(hh + 1) * LANES]
                if flag is not None:
                    piece = piece + flag
                ref[0, 2 * half + hh] = piece.astype(BF16)

    c = _dot(h, wc_ref[...])
    kcr_ref[...] = c[:, :KV_DIM]
    vcr_ref[...] = c[:, KV_DIM:]
    gt = jax.nn.sigmoid(_dot(h, wg_ref[...]))
    for g in range(N_KV_GROUPS):
        gate_ref[0, g] = gt[:, g * LANES:(g + 1) * LANES]


def _nsa_proj(x2, b, t, g, wq, wk, wc, wg, qg_p, kg_p, ebd, tm=SEL_CHUNK):
    m, d = x2.shape
    nt = t // tm
    gn, rh = N_KV_GROUPS, HEADS_PER_GROUP
    full = lambda a: pl.BlockSpec(a.shape, lambda i: (0,) * a.ndim)
    row = lambda n: pl.BlockSpec((tm, n), lambda i: (i, 0))
    head = pl.BlockSpec((1, gn, tm, LANES), lambda i: (i // nt, 0, i % nt, 0))
    outs = [jax.ShapeDtypeStruct((b, gn, rh, t, LANES), BF16)]
    outs += [jax.ShapeDtypeStruct((b, gn, t, LANES), BF16)] * 4
    outs += [jax.ShapeDtypeStruct((m, KV_DIM), F32)] * 2
    outs += [jax.ShapeDtypeStruct((b, gn, t, LANES), F32)]
    return pl.pallas_call(
        _nsa_proj_kernel, grid=(m // tm,),
        in_specs=[row(d), full(g), full(wq), full(wk), full(wc), full(wg), full(qg_p), full(kg_p), full(ebd)],
        out_specs=[pl.BlockSpec((1, gn, rh, tm, LANES), lambda i: (i // nt, 0, 0, i % nt, 0)),
                   head, head, head, head, row(KV_DIM), row(KV_DIM), head],
        out_shape=outs, compiler_params=_cparams("parallel"), name="nsa_proj",
    )(x2, g, wq, wk, wc, wg, qg_p, kg_p, ebd)


def _compress_kernel(a_ref, pos_ref, w1a_ref, w1b_ref, b1_ref, w2_ref, b2_ref, gain_ref, o_ref, *, norm):
    a = a_ref[0, 0]
    tc = a.shape[0]
    pos = pos_ref[...]
    h1 = _dot((a + pos[0:1]).astype(BF16), w1a_ref[...])
    h2 = _dot((a + pos[1:2]).astype(BF16), w1b_ref[...])
    h = h1 + pltpu.roll(h2, tc - 1, 0) + b1_ref[...]
    hid = jax.nn.gelu(h).astype(BF16)
    out = _dot(hid, w2_ref[...]) + b2_ref[...]
    if norm:
        var = jnp.sum(out * out, axis=-1, keepdims=True) * (1.0 / HEAD_DIM)
        out = out * lax.rsqrt(var + EPS) * gain_ref[...]
    o_ref[0, 0] = out.astype(o_ref.dtype)


def _compress(a, pos2, w1a, w1b, b1, w2, b2, gain, norm):
    b, g, tc, w = a.shape
    full = lambda z: pl.BlockSpec(z.shape, lambda i, j: (0,) * z.ndim)
    return pl.pallas_call(
        functools.partial(_compress_kernel, norm=norm), grid=(b, g),
        in_specs=[pl.BlockSpec((1, 1, tc, w), lambda i, j: (i, j, 0, 0)),
                  full(pos2), full(w1a), full(w1b), full(b1), full(w2), full(b2), full(gain)],
        out_specs=pl.BlockSpec((1, 1, tc, LANES), lambda i, j: (i, j, 0, 0)),
        out_shape=jax.ShapeDtypeStruct((b, g, tc, LANES), BF16),
        compiler_params=_cparams("parallel", "parallel"), name="nsa_compress",
    )(a, pos2, w1a, w1b, b1, w2, b2, gain)


def _cmp_select_kernel(q_ref, kc_ref, vct_ref, ovt_ref, ocmp_ref, sel_ref, *, top_n):
    tq = q_ref.shape[3]
    tc = kc_ref.shape[2]
    nsb = ovt_ref.shape[0]
    t = pl.program_id(1) * tq + lax.broadcasted_iota(jnp.int32, (1, tq), 1)
    n_idx = lax.broadcasted_iota(jnp.int32, (tc, 1), 0)
    vis = (n_idx * CMP_STRIDE + (CMP_BLOCK - 1)) <= t
    j_idx = lax.broadcasted_iota(jnp.int32, (nsb, 1), 0)
    cur = t // SEL_BLOCK
    forced = (j_idx == 0) | (j_idx == cur) | (j_idx == cur - 1)
    valid = j_idx <= cur
    j_f = j_idx.astype(F32)
    ovt = ovt_ref[...]

    for g in range(N_KV_GROUPS):
        kc = kc_ref[0, g]
        vct = vct_ref[0, g]
        psum = jnp.zeros((tc, tq), F32)
        for r in range(HEADS_PER_GROUP):
            s = _dot_nt(kc, q_ref[0, g, r])
            s = jnp.where(vis, s, NEG)
            mx = jnp.max(s, axis=0, keepdims=True)
            e = jnp.where(vis, jnp.exp2(s - mx), 0.0)
            den = jnp.sum(e, axis=0, keepdims=True)
            p = e * jnp.where(den > 0.0, 1.0 / den, 0.0)
            o_t = _dot(vct, p.astype(BF16))
            hd = (g * HEADS_PER_GROUP + r) * HEAD_DIM
            ocmp_ref[0, :, hd:hd + HEAD_DIM] = o_t[:HEAD_DIM].T
            psum = psum + p
        hi = psum.astype(BF16)
        lo = (psum - hi.astype(F32)).astype(BF16)
        psel = _dot(ovt, hi) + _dot(ovt, lo)
        score = jnp.where(forced, FORCE, jnp.where(valid, psel, NEG))
        sel = jnp.zeros((nsb, tq), F32)
        for _ in range(top_n):
            mx = jnp.max(score, axis=0, keepdims=True)
            first = jnp.min(jnp.where(score == mx, j_f, float(nsb)), axis=0, keepdims=True)
            pick = j_f == first
            sel = jnp.where(pick & (mx > 0.5 * NEG), 1.0, sel)
            score = jnp.where(pick, -jnp.inf, score)
        sel_ref[0, g] = sel.T.astype(BF16)


def _cmp_select(q, kc, vct, ovt, tq=256):
    b, g, r, t_len, _ = q.shape
    tc = kc.shape[2]
    nsb = ovt.shape[0]
    tq = min(tq, t_len)
    top_n = min(SEL_TOPN, nsb)
    return pl.pallas_call(
        functools.partial(_cmp_select_kernel, top_n=top_n), grid=(b, t_len // tq),
        in_specs=[pl.BlockSpec((1, g, r, tq, LANES), lambda i, j: (i, 0, 0, j, 0)),
                  pl.BlockSpec((1, g, tc, LANES), lambda i, j: (i, 0, 0, 0)),
                  pl.BlockSpec((1, g, LANES, tc), lambda i, j: (i, 0, 0, 0)),
                  pl.BlockSpec(ovt.shape, lambda i, j: (0, 0))],
        out_specs=[pl.BlockSpec((1, tq, N_HEADS * HEAD_DIM), lambda i, j: (i, j, 0)),
                   pl.BlockSpec((1, g, tq, nsb), lambda i, j: (i, 0, j, 0))],
        out_shape=[jax.ShapeDtypeStruct((b, t_len, N_HEADS * HEAD_DIM), F32),
                   jax.ShapeDtypeStruct((b, g, t_len, nsb), BF16)],
        compiler_params=_cparams("parallel", "parallel"), name="nsa_cmp_select",
    )(q, kc, vct, ovt)


def _sel_win_kernel(q_ref, ka_ref, va_ref, kw_ref, vw_ref, f_ref, sel_ref, ocmp_ref, gate_ref, o_ref,
                    sa_scr, sb_scr, *, kc):
    r_heads, tq = q_ref.shape[2], q_ref.shape[3]
    dh = HEAD_DIM
    rows = r_heads * tq
    t_len = ka_ref.shape[2]
    t0 = pl.program_id(2) * tq
    qp = q_ref[0, 0].reshape(rows, LANES)
    sel_t = sel_ref[0, 0]
    t_col = t0 + lax.broadcasted_iota(jnp.int32, (tq, 1), 0)
    lane = lax.broadcasted_iota(jnp.int32, (1, LANES), 1)
    flag_lanes = ((lane >= dh) & (lane < dh + kc // SEL_BLOCK)).astype(F32)

    n_full = t0 // kc
    n_zero = f_ref.shape[1] // LANES - 1
    last_full = jnp.maximum(n_full - 1, 0)

    def scores(c):
        fi = jnp.where(c <= n_full, c, n_zero)
        ci = jnp.minimum(c, n_full)
        fc = f_ref[:, pl.ds(pl.multiple_of(fi * LANES, LANES), LANES)]
        sb = ((_dot(sel_t, fc) - flag_lanes) * (-MASK_BIAS)).astype(BF16)
        qa = qp + jnp.concatenate([sb] * r_heads, axis=0)
        return _dot_nt(qa, ka_ref[0, 0, pl.ds(pl.multiple_of(ci * kc, kc), kc), :])

    def update(s, c, m, acc):
        ci = jnp.minimum(c, n_full)
        m_new = jnp.maximum(m, jnp.max(s, axis=-1, keepdims=True))
        alpha = jnp.exp2(m - m_new)
        p = jnp.exp2(s - m_new).astype(BF16)
        pv = _dot(p, va_ref[0, 0, pl.ds(pl.multiple_of(ci * kc, kc), kc), :])
        return m_new, alpha * acc + pv

    kpos = n_full * kc + lax.broadcasted_iota(jnp.int32, (1, kc), 1)
    causal = jnp.where(kpos <= t_col, 0.0, MASK_BIAS)
    s_diag = scores(n_full) + jnp.concatenate([causal] * r_heads, axis=0)
    init = (jnp.full((rows, 1), NEG, F32), jnp.zeros((rows, LANES), F32))
    carry = update(s_diag, n_full, *init)

    def chunk_or_pad(c):
        return jnp.where(c < n_full, c, n_full + 1)

    sa_scr[...] = scores(chunk_or_pad(0))

    def body(i, carry):
        c = 2 * i
        sb_scr[...] = scores(chunk_or_pad(c + 1))
        carry = update(sa_scr[...], jnp.minimum(c, last_full), *carry)
        sa_scr[...] = scores(chunk_or_pad(c + 2))
        return update(sb_scr[...], jnp.minimum(c + 1, last_full), *carry)

    _, acc = lax.fori_loop(0, (n_full + 1) // 2, body, carry)
    o_sel = acc[:, :dh] / acc[:, dh:dh + 1]

    span = min(tq + WINDOW, t_len)
    w0 = pl.multiple_of(jnp.maximum(t0 + tq - span, 0), tq)
    kpos_w = w0 + lax.broadcasted_iota(jnp.int32, (1, span), 1)
    in_win = (kpos_w <= t_col) & (t_col - kpos_w < WINDOW)
    bias_w = jnp.where(in_win, 0.0, MASK_BIAS)
    s = _dot_nt(qp, kw_ref[0, 0, pl.ds(w0, span), :]) + jnp.concatenate([bias_w] * r_heads, axis=0)
    p = jnp.exp2(s - jnp.max(s, axis=-1, keepdims=True)).astype(BF16)
    ow = _dot(p, vw_ref[0, 0, pl.ds(w0, span), :])
    o_win = ow[:, :dh] / ow[:, dh:dh + 1]

    gates = gate_ref[0, 0]
    ocmp = ocmp_ref[0]
    for r in range(r_heads):
        rs = slice(r * tq, (r + 1) * tq)
        o_r = (gates[:, 3 * r:3 * r + 1] * ocmp[:, r * dh:(r + 1) * dh]
               + gates[:, 3 * r + 1:3 * r + 2] * o_sel[rs]
               + gates[:, 3 * r + 2:3 * r + 3] * o_win[rs])
        o_ref[0, :, r * dh:(r + 1) * dh] = o_r.astype(o_ref.dtype)


def _sel_win(q, ka, va, kw, vw, f_mat, sel, ocmp, gates, tq=256, kc=SEL_CHUNK):
    b, g, r, t_len, _ = q.shape
    nsb = sel.shape[-1]
    gw = r * HEAD_DIM
    kv_spec = pl.BlockSpec((1, 1, t_len, LANES), lambda i, j, k: (i, j, 0, 0))
    return pl.pallas_call(
        functools.partial(_sel_win_kernel, kc=kc), grid=(b, g, t_len // tq),
        in_specs=[pl.BlockSpec((1, 1, r, tq, LANES), lambda i, j, k: (i, j, 0, k, 0)),
                  kv_spec, kv_spec, kv_spec, kv_spec,
                  pl.BlockSpec(f_mat.shape, lambda i, j, k: (0, 0)),
                  pl.BlockSpec((1, 1, tq, nsb), lambda i, j, k: (i, j, k, 0)),
                  pl.BlockSpec((1, tq, gw), lambda i, j, k: (i, k, j)),
                  pl.BlockSpec((1, 1, tq, LANES), lambda i, j, k: (i, j, k, 0))],
        out_specs=pl.BlockSpec((1, tq, gw), lambda i, j, k: (i, k, j)),
        out_shape=jax.ShapeDtypeStruct((b, t_len, g * gw), BF16),
        scratch_shapes=[pltpu.VMEM((r * tq, kc), F32), pltpu.VMEM((r * tq, kc), F32)],
        compiler_params=_cparams("parallel", "parallel", "arbitrary"), name="nsa_sel_win",
    )(q, ka, va, kw, vw, f_mat, sel, ocmp, gates)


def _res_matmul_kernel(a_ref, w_ref, x_ref, o_ref):
    o_ref[...] = x_ref[...] + _dot(a_ref[...], w_ref[...])


def _res_matmul(a, w, x2, tm=512):
    m, k = a.shape
    n = w.shape[1]
    return pl.pallas_call(
        _res_matmul_kernel, grid=(m // tm,),
        in_specs=[pl.BlockSpec((tm, k), lambda i: (i, 0)), pl.BlockSpec((k, n), lambda i: (0, 0)),
                  pl.BlockSpec((tm, n), lambda i: (i, 0))],
        out_specs=pl.BlockSpec((tm, n), lambda i: (i, 0)),
        out_shape=jax.ShapeDtypeStruct((m, n), F32),
        compiler_params=_cparams("parallel"), name="res_matmul",
    )(a, w, x2)


def _ffn_kernel(x_ref, g_ref, wg_ref, wu_ref, wd_ref, o_ref, *, hc):
    x = x_ref[...]
    h = _rms_rows(x, g_ref[...]).astype(BF16)
    acc = x
    for c in range(wg_ref.shape[1] // hc):
        sl = slice(c * hc, (c + 1) * hc)
        a = jax.nn.silu(_dot(h, wg_ref[:, sl])) * _dot(h, wu_ref[:, sl])
        acc = acc + _dot(a.astype(BF16), wd_ref[sl, :])
    o_ref[...] = acc


def _ffn(x2, g, wg, wu, wd, tm=512, hc=MXU_DIM):
    m, d = x2.shape
    full = lambda a: pl.BlockSpec(a.shape, lambda i: (0,) * a.ndim)
    return pl.pallas_call(
        functools.partial(_ffn_kernel, hc=hc), grid=(m // tm,),
        in_specs=[pl.BlockSpec((tm, d), lambda i: (i, 0)), full(g), full(wg), full(wu), full(wd)],
        out_specs=pl.BlockSpec((tm, d), lambda i: (i, 0)),
        out_shape=jax.ShapeDtypeStruct((m, d), F32),
        compiler_params=_cparams("parallel"), name="ffn_swiglu",
    )(x2, g, wg, wu, wd)


def _s5_kernel(x_ref, g_ref, win_ref, bre_ref, bim_ref, ad_ref, ap_ref, cre_ref, cim_ref, d_ref, wglu_ref,
               o_ref, xr_scr, xi_scr, car_scr, *, lane_chunk):
    rows, d = x_ref.shape[1], x_ref.shape[2]
    n_state = xr_scr.shape[1]
    n_kt = bre_ref.shape[0]
    kw = bre_ref.shape[1]
    sw = bre_ref.shape[2]

    @pl.when(pl.program_id(1) == 0)
    def _():
        car_scr[...] = jnp.zeros_like(car_scr)

    x = x_ref[0]
    u = _dot(_rms_rows(x, g_ref[...]).astype(BF16), win_ref[...])
    ub = u.astype(BF16)
    for kt in range(n_kt):
        uk = ub[:, kt * kw:(kt + 1) * kw]
        xr_scr[:, kt * sw:(kt + 1) * sw] = _dot(uk, bre_ref[kt])
        xi_scr[:, kt * sw:(kt + 1) * sw] = _dot(uk, bim_ref[kt])

    def step(j, carry):
        r0 = pl.multiple_of(j * SUBLANES, SUBLANES)
        new = []
        for lc in range(n_state // lane_chunk):
            ls = slice(lc * lane_chunk, (lc + 1) * lane_chunk)
            re = xr_scr[pl.ds(r0, SUBLANES), ls]
            im = xi_scr[pl.ds(r0, SUBLANES), ls]
            for di, dist in enumerate((1, 2, 4)):
                sr = pltpu.roll(re, dist, 0)
                si = pltpu.roll(im, dist, 0)
                ar, ai = ad_ref[di, 0, :, ls], ad_ref[di, 1, :, ls]
                re, im = re + (ar * sr - ai * si), im + (ar * si + ai * sr)
            pr, pi = ap_ref[0, :, ls], ap_ref[1, :, ls]
            cr, ci = carry[2 * lc], carry[2 * lc + 1]
            re = re + (pr * cr - pi * ci)
            im = im + (pr * ci + pi * cr)
            xr_scr[pl.ds(r0, SUBLANES), ls] = re
            xi_scr[pl.ds(r0, SUBLANES), ls] = im
            new += [re[SUBLANES - 1:SUBLANES], im[SUBLANES - 1:SUBLANES]]
        return tuple(new)

    init = []
    for lc in range(n_state // lane_chunk):
        ls = slice(lc * lane_chunk, (lc + 1) * lane_chunk)
        init += [car_scr[0:1, ls], car_scr[1:2, ls]]
    fin = lax.fori_loop(0, rows // SUBLANES, step, tuple(init))
    for lc in range(n_state // lane_chunk):
        ls = slice(lc * lane_chunk, (lc + 1) * lane_chunk)
        car_scr[0:1, ls] = fin[2 * lc]
        car_scr[1:2, ls] = fin[2 * lc + 1]

    ys = []
    for kt in range(n_kt):
        ss = slice(kt * sw, (kt + 1) * sw)
        ys.append(_dot(xr_scr[:, ss].astype(BF16), cre_ref[kt]) + _dot(xi_scr[:, ss].astype(BF16), cim_ref[kt]))
    y = jnp.concatenate(ys, axis=1) + d_ref[...] * u
    vg = _dot(jax.nn.gelu(y).astype(BF16), wglu_ref[...])
    o_ref[0] = x + vg[:, :d] * jax.nn.sigmoid(vg[:, d:])


def _s5_tables(b_re, b_im, c_re, c_im, log_dt, a_re, a_im):
    ng, ps, cg = b_re.shape
    per_tile = MXU_DIM // cg
    n_kt = ng // per_tile
    dt = jnp.exp(log_dt)[:, None]
    lam_r, lam_i = dt * a_re, dt * a_im
    mag = jnp.exp(lam_r)
    abar_r, abar_i = mag * jnp.cos(lam_i), mag * jnp.sin(lam_i)
    den = a_re * a_re + a_im * a_im
    coef_r = ((abar_r - 1.0) * a_re + abar_i * a_im) / den
    coef_i = (abar_i * a_re - (abar_r - 1.0) * a_im) / den
    bbar_r = coef_r[..., None] * b_re - coef_i[..., None] * b_im
    bbar_i = coef_r[..., None] * b_im + coef_i[..., None] * b_re
    eye = jnp.eye(per_tile, dtype=F32)
    bd_in = lambda m: jnp.einsum('kgpc,gh->kgchp', m.reshape(n_kt, per_tile, ps, cg), eye).reshape(
        n_kt, per_tile * cg, per_tile * ps).astype(BF16)
    bd_out = lambda m: jnp.einsum('kgcp,gh->kgphc', m.reshape(n_kt, per_tile, cg, ps), eye).reshape(
        n_kt, per_tile * ps, per_tile * cg).astype(BF16)

    def power(n):
        pm = jnp.exp(n * lam_r)
        return (pm * jnp.cos(n * lam_i)).reshape(-1), (pm * jnp.sin(n * lam_i)).reshape(-1)

    sub = jnp.arange(SUBLANES)[:, None]
    ad = []
    for dist in (1, 2, 4):
        pr, pi = power(float(dist))
        keep = (sub >= dist).astype(F32)
        ad.append(jnp.stack([keep * pr[None, :], keep * pi[None, :]]))
    ad = jnp.stack(ad)
    rows_p = [power(float(i + 1)) for i in range(SUBLANES)]
    ap = jnp.stack([jnp.stack([p[0] for p in rows_p]), jnp.stack([p[1] for p in rows_p])])
    return bd_in(bbar_r), bd_in(bbar_i), ad, ap, bd_out(c_re), bd_out(-c_im)


def _s5_layer(x, g_norm, w_in, b_re, b_im, c_re, c_im, d_skip, log_dt, a_re, a_im, w_glu, rows=256):
    b, t, d = x.shape
    bre, bim, ad, ap, cre, cim = _s5_tables(b_re, b_im, c_re, c_im, log_dt, a_re, a_im)
    n_state = ad.shape[-1]
    rows = min(rows, t)
    full = lambda a: pl.BlockSpec(a.shape, lambda i, j: (0,) * a.ndim)
    args = (g_norm[None, :], w_in.astype(BF16), bre, bim, ad, ap, cre, cim, d_skip[None, :], w_glu.astype(BF16))
    return pl.pallas_call(
        functools.partial(_s5_kernel, lane_chunk=1024), grid=(b, t // rows),
        in_specs=[pl.BlockSpec((1, rows, d), lambda i, j: (i, j, 0))] + [full(a) for a in args],
        out_specs=pl.BlockSpec((1, rows, d), lambda i, j: (i, j, 0)),
        out_shape=jax.ShapeDtypeStruct((b, t, d), F32),
        scratch_shapes=[pltpu.VMEM((rows, n_state), F32), pltpu.VMEM((rows, n_state), F32),
                        pltpu.VMEM((SUBLANES, n_state), F32)],
        compiler_params=_cparams("parallel", "arbitrary"), name="s5_mixer",
    )(x, *args)


def _pad_heads(w, n_heads, width):
    d = w.shape[0]
    return jnp.pad(w.reshape(d, n_heads, width), ((0, 0), (0, 0), (0, LANES - width))).reshape(d, n_heads * LANES)


def _nsa_layer(x, g_norm, w_in, w_out, q_gain, k_gain, cmp_pos, cmp_w1, cmp_b1, cmp_w2, cmp_b2):
    b, t, d = x.shape
    gn, rh, dh = N_KV_GROUPS, HEADS_PER_GROUP, HEAD_DIM
    assert t % SEL_CHUNK == 0 and t >= SEL_CHUNK + WINDOW
    x2 = x.reshape(b * t, d)
    scale = LOG2E / math.sqrt(dh)

    kvw = lambda i: w_in[:, d + i * KV_DIM:d + (i + 1) * KV_DIM]
    wq = _pad_heads(w_in[:, :d], N_HEADS, dh).astype(BF16)
    wk = jnp.concatenate([_pad_heads(kvw(i), gn, dh) for i in (2, 3, 4, 5)], axis=1).astype(BF16)
    wc = jnp.concatenate([kvw(0), kvw(1)], axis=1).astype(BF16)
    wg = _pad_heads(w_in[:, d + 6 * KV_DIM:], gn, rh * N_BRANCHES).astype(BF16)
    pad_gain = lambda v, n: jnp.tile(jnp.pad(v, (0, LANES - dh)), n)[None, :]
    qg_p = pad_gain(q_gain * scale, N_HEADS)
    kg_p = jnp.concatenate([pad_gain(k_gain[1], gn), pad_gain(k_gain[2], gn)], axis=0)
    seg = jnp.arange(2 * LANES) // LANES
    ebd = (seg[:, None] == seg[None, :]).astype(BF16)

    q, ka, va, kw, vw, kc_raw, vc_raw, gates = _nsa_proj(x2, b, t, g_norm[None, :], wq, wk, wc, wg, qg_p, kg_p, ebd)

    tc = t // CMP_STRIDE
    half = CMP_STRIDE * dh
    blk = lambda a: a.reshape(b, tc, CMP_STRIDE, gn, dh).transpose(0, 3, 1, 2, 4).reshape(b, gn, tc, half)
    comp = []
    for i, (raw, norm) in enumerate(((kc_raw, True), (vc_raw, False))):
        w1 = cmp_w1[i].astype(BF16)
        comp.append(_compress(blk(raw), cmp_pos[i].reshape(2, half), w1[:half], w1[half:], cmp_b1[i][None, :],
                              jnp.pad(cmp_w2[i], ((0, 0), (0, LANES - dh))).astype(BF16),
                              jnp.pad(cmp_b2[i], (0, LANES - dh))[None, :],
                              jnp.pad(k_gain[0], (0, LANES - dh))[None, :], norm))
    kc, vc = comp
    vct = vc.transpose(0, 1, 3, 2)

    nsb = t // SEL_BLOCK
    n_i = jnp.arange(tc)
    j_i = jnp.arange(nsb)
    n_cmp = (t - CMP_BLOCK) // CMP_STRIDE + 1
    ovt = ((n_i[None, :] * CMP_STRIDE < (j_i[:, None] + 1) * SEL_BLOCK)
           & (n_i[None, :] * CMP_STRIDE + CMP_BLOCK > j_i[:, None] * SEL_BLOCK)
           & (n_i[None, :] < n_cmp)).astype(BF16)
    ocmp, sel = _cmp_select(q, kc, vct, ovt)

    per_chunk = SEL_CHUNK // SEL_BLOCK
    col = jnp.arange((t // SEL_CHUNK + 1) * LANES)
    f_mat = ((col[None, :] % LANES >= dh) & (col[None, :] % LANES < dh + per_chunk)
             & (j_i[:, None] == (col[None, :] // LANES) * per_chunk + col[None, :] % LANES - dh)).astype(BF16)
    o = _sel_win(q, ka, va, kw, vw, f_mat, sel, ocmp, gates)
    return _res_matmul(o.reshape(b * t, d), w_out.astype(BF16), x2).reshape(b, t, d)


def kernel(x, mix_norm, ffn_norm, nsa_w_in, nsa_w_out, nsa_q_gain, nsa_k_gain, nsa_cmp_pos, nsa_cmp_w1, nsa_cmp_b1, nsa_cmp_w2, nsa_cmp_b2, s5_w_in, s5_b_re, s5_b_im, s5_c_re, s5_c_im, s5_d, s5_log_dt, s5_a_re, s5_a_im, s5_w_glu, ffn_w_gate, ffn_w_up, ffn_w_down):
    b, t, d = x.shape
    depth = mix_norm.shape[0]
    for layer in range(depth):
        i = layer // 2
        if layer % 2 == 0:
            x = _nsa_layer(x, mix_norm[layer], nsa_w_in[i], nsa_w_out[i], nsa_q_gain[i], nsa_k_gain[i],
                           nsa_cmp_pos[i], nsa_cmp_w1[i], nsa_cmp_b1[i], nsa_cmp_w2[i], nsa_cmp_b2[i])
        else:
            x = _s5_layer(x, mix_norm[layer], s5_w_in[i], s5_b_re[i], s5_b_im[i], s5_c_re[i], s5_c_im[i],
                          s5_d[i], s5_log_dt[i], s5_a_re[i], s5_a_im[i], s5_w_glu[i])
        x = _ffn(x.reshape(b * t, d), ffn_norm[layer][None, :], ffn_w_gate[layer].astype(BF16),
                 ffn_w_up[layer].astype(BF16), ffn_w_down[layer].astype(BF16)).reshape(b, t, d)
    return x
```

```python
import functools
import math

import jax
import jax.numpy as jnp
from jax import lax
from jax.experimental import pallas as pl
from jax.experimental.pallas import tpu as pltpu

F32 = jnp.float32
BF16 = jnp.bfloat16

EPS = 1e-6
NEG = -1e30
FORCE = 1e4
MASK_BIAS = -1e9
LOG2E = 1.4426950408889634

N_HEADS = 16
HEAD_DIM = 64
N_KV_GROUPS = 4
HEADS_PER_GROUP = 4
KV_DIM = N_KV_GROUPS * HEAD_DIM
CMP_BLOCK = 32
CMP_STRIDE = 16
SEL_BLOCK = 64
SEL_TOPN = 16
WINDOW = 512
N_BRANCHES = 3
S5_GROUP = 16
S5_STATE = 64

LANES = 128
SUBLANES = 8
MXU_DIM = 256
SEL_CHUNK = 512
V7X_VMEM_LIMIT_BYTES = 56 * 1024 * 1024


def _cparams(*semantics):
    return pltpu.CompilerParams(dimension_semantics=semantics,
                                vmem_limit_bytes=V7X_VMEM_LIMIT_BYTES)


def _rms_rows(x, g):
    var = jnp.mean(x * x, axis=-1, keepdims=True)
    return x * lax.rsqrt(var + EPS) * g


def _dot(a, b):
    return jnp.dot(a, b, preferred_element_type=F32)


def _dot_nt(a, b):
    return lax.dot_general(a, b, (((1,), (1,)), ((), ())), preferred_element_type=F32)


def _seg_sumsq(v, ebd):
    sq = v * v
    hi = sq.astype(BF16)
    lo = (sq - hi.astype(F32)).astype(BF16)
    return _dot(hi, ebd) + _dot(lo, ebd)


def _nsa_proj_kernel(x_ref, g_ref, wq_ref, wk_ref, wc_ref, wg_ref, qg_ref, kg_ref, ebd_ref,
                     q_ref, ka_ref, va_ref, kw_ref, vw_ref, kcr_ref, vcr_ref, gate_ref):
    tm = x_ref.shape[0]
    h = _rms_rows(x_ref[...], g_ref[...]).astype(BF16)
    ebd = ebd_ref[...]
    inv_dh = 1.0 / HEAD_DIM
    slab = 2 * LANES

    q = _dot(h, wq_ref[...])
    qg = qg_ref[...]
    for pair in range(N_HEADS // 2):
        sl = slice(pair * slab, (pair + 1) * slab)
        qi = q[:, sl]
        qn = (qi * lax.rsqrt(_seg_sumsq(qi, ebd) * inv_dh + EPS) * qg[:, sl]).astype(BF16)
        for hh in range(2):
            g, r = divmod(2 * pair + hh, HEADS_PER_GROUP)
            q_ref[0, g, r] = qn[:, hh * LANES:(hh + 1) * LANES]

    row = lax.rem(pl.program_id(0) * tm, SEL_CHUNK) + lax.broadcasted_iota(jnp.int32, (tm, 1), 0)
    lane = lax.broadcasted_iota(jnp.int32, (1, LANES), 1)
    blk_flag = ((row // SEL_BLOCK) == (lane - HEAD_DIM)).astype(F32)
    one_flag = (lane == HEAD_DIM).astype(F32)

    kv = _dot(h, wk_ref[...])
    kg = kg_ref[...]
    plan = ((ka_ref, 0, blk_flag), (va_ref, None, one_flag), (kw_ref, 1, None), (vw_ref, None, one_flag))
    for part, (ref, gain_row, flag) in enumerate(plan):
        for half in range(2):
            base = part * N_KV_GROUPS * LANES + half * slab
            v = kv[:, base:base + slab]
            if gain_row is not None:
                v = v * lax.rsqrt(_seg_sumsq(v, ebd) * inv_dh + EPS) * kg[gain_row:gain_row + 1, half * slab:(half + 1) * slab]
            for hh in range(2):
                piece = v[:, hh * LANES:(hh + 1) * LANES]
                if flag is not None:
                    piece = piece + flag
                ref[0, 2 * half + hh] = piece.astype(BF16)

    c = _dot(h, wc_ref[...])
    kcr_ref[...] = c[:, :KV_DIM]
    vcr_ref[...] = c[:, KV_DIM:]
    gt = jax.nn.sigmoid(_dot(h, wg_ref[...]))
    for g in range(N_KV_GROUPS):
        gate_ref[0, g] = gt[:, g * LANES:(g + 1) * LANES]


def _nsa_proj(x2, b, t, g, wq, wk, wc, wg, qg_p, kg_p, ebd, tm=512):
    m, d = x2.shape
    assert SEL_CHUNK % tm == 0 and t % SEL_CHUNK == 0
    nt = t // tm
    gn, rh = N_KV_GROUPS, HEADS_PER_GROUP
    full = lambda a: pl.BlockSpec(a.shape, lambda i: (0,) * a.ndim)
    row = lambda n: pl.BlockSpec((tm, n), lambda i: (i, 0))
    head = pl.BlockSpec((1, gn, tm, LANES), lambda i: (i // nt, 0, i % nt, 0))
    outs = [jax.ShapeDtypeStruct((b, gn, rh, t, LANES), BF16)]
    outs += [jax.ShapeDtypeStruct((b, gn, t, LANES), BF16)] * 4
    outs += [jax.ShapeDtypeStruct((m, KV_DIM), F32)] * 2
    outs += [jax.ShapeDtypeStruct((b, gn, t, LANES), F32)]
    return pl.pallas_call(
        _nsa_proj_kernel, grid=(m // tm,),
        in_specs=[row(d), full(g), full(wq), full(wk), full(wc), full(wg), full(qg_p), full(kg_p), full(ebd)],
        out_specs=[pl.BlockSpec((1, gn, rh, tm, LANES), lambda i: (i // nt, 0, 0, i % nt, 0)),
                   head, head, head, head, row(KV_DIM), row(KV_DIM), head],
        out_shape=outs, compiler_params=_cparams("parallel"), name="nsa_proj",
    )(x2, g, wq, wk, wc, wg, qg_p, kg_p, ebd)


def _compress_kernel(a_ref, pos_ref, w1a_ref, w1b_ref, b1_ref, w2_ref, b2_ref, gain_ref, o_ref, *, norm):
    a = a_ref[0, 0]
    tc = a.shape[0]
    pos = pos_ref[...]
    h1 = _dot((a + pos[0:1]).astype(BF16), w1a_ref[...])
    h2 = _dot((a + pos[1:2]).astype(BF16), w1b_ref[...])
    h = h1 + pltpu.roll(h2, tc - 1, 0) + b1_ref[...]
    hid = jax.nn.gelu(h).astype(BF16)
    out = _dot(hid, w2_ref[...]) + b2_ref[...]
    if norm:
        var = jnp.sum(out * out, axis=-1, keepdims=True) * (1.0 / HEAD_DIM)
        out = out * lax.rsqrt(var + EPS) * gain_ref[...]
    o_ref[0, 0] = out.astype(o_ref.dtype)


def _compress(a, pos2, w1a, w1b, b1, w2, b2, gain, norm):
    b, g, tc, w = a.shape
    full = lambda z: pl.BlockSpec(z.shape, lambda i, j: (0,) * z.ndim)
    return pl.pallas_call(
        functools.partial(_compress_kernel, norm=norm), grid=(b, g),
        in_specs=[pl.BlockSpec((1, 1, tc, w), lambda i, j: (i, j, 0, 0)),
                  full(pos2), full(w1a), full(w1b), full(b1), full(w2), full(b2), full(gain)],
        out_specs=pl.BlockSpec((1, 1, tc, LANES), lambda i, j: (i, j, 0, 0)),
        out_shape=jax.ShapeDtypeStruct((b, g, tc, LANES), BF16),
        compiler_params=_cparams("parallel", "parallel"), name="nsa_compress",
    )(a, pos2, w1a, w1b, b1, w2, b2, gain)


def _cmp_select_kernel(q_ref, kc_ref, vct_ref, ovt_ref, ocmp_ref, sel_ref, *, top_n, n_tiers):
    tq = q_ref.shape[3]
    tc = kc_ref.shape[2]
    nsb = ovt_ref.shape[0]
    n_tiles = pl.num_programs(1)
    t = pl.program_id(1) * tq + lax.broadcasted_iota(jnp.int32, (1, tq), 1)
    cur = t // SEL_BLOCK
    any_vis = t >= CMP_BLOCK - 1

    def tile_body(rows, blocks):
        n_idx = lax.broadcasted_iota(jnp.int32, (rows, 1), 0)
        bias = jnp.where((n_idx * CMP_STRIDE + (CMP_BLOCK - 1)) <= t, 0.0, NEG)
        j_idx = lax.broadcasted_iota(jnp.int32, (blocks, 1), 0)
        forced = (j_idx == 0) | (j_idx == cur) | (j_idx == cur - 1)
        valid = j_idx <= cur
        j_f = j_idx.astype(F32)
        ovt = ovt_ref[:blocks, :rows]

        for g in range(N_KV_GROUPS):
            kc = kc_ref[0, g, :rows, :]
            vct = vct_ref[0, g, :, :rows]
            psum = jnp.zeros((rows, tq), F32)
            for r in range(HEADS_PER_GROUP):
                s = _dot_nt(kc, q_ref[0, g, r]) + bias
                e = jnp.exp2(s - jnp.max(s, axis=0, keepdims=True))
                den = jnp.sum(e, axis=0, keepdims=True)
                p = e * jnp.where(any_vis, 1.0 / den, 0.0)
                o_t = _dot(vct, p.astype(BF16))
                hd = (g * HEADS_PER_GROUP + r) * HEAD_DIM
                ocmp_ref[0, :, hd:hd + HEAD_DIM] = o_t[:HEAD_DIM].T
                psum = psum + p
            hi = psum.astype(BF16)
            lo = (psum - hi.astype(F32)).astype(BF16)
            psel = _dot(ovt, hi) + _dot(ovt, lo)
            score = jnp.where(forced, FORCE, jnp.where(valid, psel, NEG))
            for _ in range(top_n):
                mx = jnp.max(score, axis=0, keepdims=True)
                first = jnp.min(jnp.where(score == mx, j_f, float(nsb)), axis=0, keepdims=True)
                score = jnp.where(j_f == first, -jnp.inf, score)
            sel_ref[0, g, :blocks, :] = jnp.where((score == -jnp.inf) & valid, 1.0, 0.0)
            if blocks < nsb:
                sel_ref[0, g, blocks:, :] = jnp.zeros((nsb - blocks, tq), F32)

    tier = (pl.program_id(1) * n_tiers) // n_tiles
    for k in range(n_tiers):
        pl.when(tier == k)(functools.partial(tile_body, tc * (k + 1) // n_tiers, nsb * (k + 1) // n_tiers))


def _cmp_select(q, kc, vct, ovt, tq=256):
    b, g, r, t_len, _ = q.shape
    tc = kc.shape[2]
    nsb = ovt.shape[0]
    tq = min(tq, t_len)
    top_n = min(SEL_TOPN, nsb)
    n_tiers = 4
    assert (t_len // tq) % n_tiers == 0 and nsb % (n_tiers * SUBLANES) == 0 and tc % (n_tiers * LANES // 4) == 0
    return pl.pallas_call(
        functools.partial(_cmp_select_kernel, top_n=top_n, n_tiers=n_tiers), grid=(b, t_len // tq),
        in_specs=[pl.BlockSpec((1, g, r, tq, LANES), lambda i, j: (i, 0, 0, j, 0)),
                  pl.BlockSpec((1, g, tc, LANES), lambda i, j: (i, 0, 0, 0)),
                  pl.BlockSpec((1, g, LANES, tc), lambda i, j: (i, 0, 0, 0)),
                  pl.BlockSpec(ovt.shape, lambda i, j: (0, 0))],
        out_specs=[pl.BlockSpec((1, tq, N_HEADS * HEAD_DIM), lambda i, j: (i, j, 0)),
                   pl.BlockSpec((1, g, nsb, tq), lambda i, j: (i, 0, 0, j))],
        out_shape=[jax.ShapeDtypeStruct((b, t_len, N_HEADS * HEAD_DIM), F32),
                   jax.ShapeDtypeStruct((b, g, nsb, t_len), F32)],
        compiler_params=_cparams("parallel", "parallel"), name="nsa_cmp_select",
    )(q, kc, vct, ovt)


def _sel_win_kernel(q_ref, ka_ref, vat_ref, kw_ref, vwt_ref, selt_ref, ocmp_ref, gate_ref, o_ref,
                    qa_scr, sa_scr, sb_scr, *, kc):
    r_heads, tq = q_ref.shape[2], q_ref.shape[3]
    dh = HEAD_DIM
    cols = r_heads * tq
    t_len = ka_ref.shape[2]
    per_chunk = kc // SEL_BLOCK
    aug_rows = -(-per_chunk // (2 * SUBLANES)) * (2 * SUBLANES)
    t0 = pl.program_id(2) * tq
    t_lane = t0 + lax.broadcasted_iota(jnp.int32, (1, tq), 1)

    for r in range(r_heads):
        qa_scr[:, r * tq:(r + 1) * tq] = q_ref[0, 0, r].astype(F32).T.astype(BF16)

    def softmax_t(s, m_new):
        return jnp.exp2((s - m_new).astype(BF16))

    span = min(tq + WINDOW, t_len)
    w0 = pl.multiple_of(jnp.maximum(t0 + tq - span, 0), tq)
    kpos_w = w0 + lax.broadcasted_iota(jnp.int32, (span, 1), 0)
    in_win = (kpos_w <= t_lane) & (t_lane - kpos_w < WINDOW)
    bias_w = jnp.where(in_win, 0.0, MASK_BIAS)
    s = _dot(kw_ref[0, 0, pl.ds(w0, span), :], qa_scr[...]) + jnp.concatenate([bias_w] * r_heads, axis=1)
    ow = _dot(vwt_ref[0, 0, :, pl.ds(w0, span)], softmax_t(s, jnp.max(s, axis=0, keepdims=True)))
    o_win_t = ow[:dh] / ow[dh:dh + 1]

    n_full = t0 // kc
    last_full = jnp.maximum(n_full - 1, 0)

    def scores(c):
        ci = jnp.minimum(c, n_full)
        sel8 = selt_ref[0, 0, pl.ds(pl.multiple_of(ci * per_chunk, per_chunk), per_chunk), :]
        sb = (jnp.where(c <= n_full, sel8, 0.0) - 1.0) * (-MASK_BIAS)
        if aug_rows > per_chunk:
            sb = jnp.concatenate([sb, jnp.zeros((aug_rows - per_chunk, tq), F32)], axis=0)
        qa_scr[dh:dh + aug_rows, :] = jnp.concatenate([sb.astype(BF16)] * r_heads, axis=1)
        return _dot(ka_ref[0, 0, pl.ds(pl.multiple_of(ci * kc, kc), kc), :], qa_scr[...])

    def update(s, c, m, acc):
        ci = jnp.minimum(c, n_full)
        m_new = jnp.maximum(m, jnp.max(s, axis=0, keepdims=True))
        alpha = jnp.exp2(m - m_new)
        pv = _dot(vat_ref[0, 0, :, pl.ds(pl.multiple_of(ci * kc, kc), kc)], softmax_t(s, m_new))
        return m_new, alpha * acc + pv

    kpos = n_full * kc + lax.broadcasted_iota(jnp.int32, (kc, 1), 0)
    causal = jnp.where(kpos <= t_lane, 0.0, MASK_BIAS)
    s_diag = scores(n_full) + jnp.concatenate([causal] * r_heads, axis=1)
    init = (jnp.full((1, cols), NEG, F32), jnp.zeros((vat_ref.shape[2], cols), F32))
    carry = update(s_diag, n_full, *init)

    def chunk_or_pad(c):
        return jnp.where(c < n_full, c, n_full + 1)

    sa_scr[...] = scores(chunk_or_pad(0))

    def body(i, carry):
        c = 2 * i
        sb_scr[...] = scores(chunk_or_pad(c + 1))
        carry = update(sa_scr[...], jnp.minimum(c, last_full), *carry)
        sa_scr[...] = scores(chunk_or_pad(c + 2))
        return update(sb_scr[...], jnp.minimum(c + 1, last_full), *carry)

    _, acc = lax.fori_loop(0, (n_full + 1) // 2, body, carry)
    o_sel_t = acc[:dh] / acc[dh:dh + 1]

    gates = gate_ref[0, 0]
    ocmp = ocmp_ref[0]
    for r in range(r_heads):
        cs = slice(r * tq, (r + 1) * tq)
        o_r = (gates[:, 3 * r:3 * r + 1] * ocmp[:, r * dh:(r + 1) * dh]
               + gates[:, 3 * r + 1:3 * r + 2] * o_sel_t[:, cs].T
               + gates[:, 3 * r + 2:3 * r + 3] * o_win_t[:, cs].T)
        o_ref[0, :, r * dh:(r + 1) * dh] = o_r.astype(o_ref.dtype)


def _sel_win(q, ka, vat, kw, vwt, selt, ocmp, gates, tq=256, kc=SEL_CHUNK):
    b, g, r, t_len, _ = q.shape
    nsb = selt.shape[2]
    vrows = vat.shape[2]
    gw = r * HEAD_DIM
    k_spec = pl.BlockSpec((1, 1, t_len, LANES), lambda i, j, k: (i, j, 0, 0))
    vt_spec = pl.BlockSpec((1, 1, vrows, t_len), lambda i, j, k: (i, j, 0, 0))
    return pl.pallas_call(
        functools.partial(_sel_win_kernel, kc=kc), grid=(b, g, t_len // tq),
        in_specs=[pl.BlockSpec((1, 1, r, tq, LANES), lambda i, j, k: (i, j, 0, k, 0)),
                  k_spec, vt_spec, k_spec, vt_spec,
                  pl.BlockSpec((1, 1, nsb, tq), lambda i, j, k: (i, j, 0, k)),
                  pl.BlockSpec((1, tq, gw), lambda i, j, k: (i, k, j)),
                  pl.BlockSpec((1, 1, tq, LANES), lambda i, j, k: (i, j, k, 0))],
        out_specs=pl.BlockSpec((1, tq, gw), lambda i, j, k: (i, k, j)),
        out_shape=jax.ShapeDtypeStruct((b, t_len, g * gw), BF16),
        scratch_shapes=[pltpu.VMEM((LANES, r * tq), BF16),
                        pltpu.VMEM((kc, r * tq), F32), pltpu.VMEM((kc, r * tq), F32)],
        compiler_params=_cparams("parallel", "parallel", "arbitrary"), name="nsa_sel_win",
    )(q, ka, vat, kw, vwt, selt, ocmp, gates)


def _res_matmul_kernel(a_ref, w_ref, x_ref, o_ref):
    o_ref[...] = x_ref[...] + _dot(a_ref[...], w_ref[...])


def _res_matmul(a, w, x2, tm=512):
    m, k = a.shape
    n = w.shape[1]
    return pl.pallas_call(
        _res_matmul_kernel, grid=(m // tm,),
        in_specs=[pl.BlockSpec((tm, k), lambda i: (i, 0)), pl.BlockSpec((k, n), lambda i: (0, 0)),
                  pl.BlockSpec((tm, n), lambda i: (i, 0))],
        out_specs=pl.BlockSpec((tm, n), lambda i: (i, 0)),
        out_shape=jax.ShapeDtypeStruct((m, n), F32),
        compiler_params=_cparams("parallel"), name="res_matmul",
    )(a, w, x2)


def _ffn_kernel(x_ref, g_ref, wg_ref, wu_ref, wd_ref, o_ref, *, hc):
    x = x_ref[...]
    h = _rms_rows(x, g_ref[...]).astype(BF16)
    acc = x
    for c in range(wg_ref.shape[1] // hc):
        sl = slice(c * hc, (c + 1) * hc)
        a = jax.nn.silu(_dot(h, wg_ref[:, sl])) * _dot(h, wu_ref[:, sl])
        acc = acc + _dot(a.astype(BF16), wd_ref[sl, :])
    o_ref[...] = acc


def _ffn(x2, g, wg, wu, wd, tm=512, hc=MXU_DIM):
    m, d = x2.shape
    full = lambda a: pl.BlockSpec(a.shape, lambda i: (0,) * a.ndim)
    return pl.pallas_call(
        functools.partial(_ffn_kernel, hc=hc), grid=(m // tm,),
        in_specs=[pl.BlockSpec((tm, d), lambda i: (i, 0)), full(g), full(wg), full(wu), full(wd)],
        out_specs=pl.BlockSpec((tm, d), lambda i: (i, 0)),
        out_shape=jax.ShapeDtypeStruct((m, d), F32),
        compiler_params=_cparams("parallel"), name="ffn_swiglu",
    )(x2, g, wg, wu, wd)


def _s5_kernel(x_ref, g_ref, win_ref, bre_ref, bim_ref, ad_ref, ap_ref, cre_ref, cim_ref, d_ref, wglu_ref,
               o_ref, xr_scr, xi_scr, car_scr, *, lane_chunk):
    rows, d = x_ref.shape[1], x_ref.shape[2]
    n_state = xr_scr.shape[1]
    n_kt = bre_ref.shape[0]
    kw = bre_ref.shape[1]
    sw = bre_ref.shape[2]

    @pl.when(pl.program_id(1) == 0)
    def _():
        car_scr[...] = jnp.zeros_like(car_scr)

    x = x_ref[0]
    u = _dot(_rms_rows(x, g_ref[...]).astype(BF16), win_ref[...])
    ub = u.astype(BF16)
    for kt in range(n_kt):
        uk = ub[:, kt * kw:(kt + 1) * kw]
        xr_scr[:, kt * sw:(kt + 1) * sw] = _dot(uk, bre_ref[kt])
        xi_scr[:, kt * sw:(kt + 1) * sw] = _dot(uk, bim_ref[kt])

    def step(j, carry):
        r0 = pl.multiple_of(j * SUBLANES, SUBLANES)
        new = []
        for lc in range(n_state // lane_chunk):
            ls = slice(lc * lane_chunk, (lc + 1) * lane_chunk)
            re = xr_scr[pl.ds(r0, SUBLANES), ls]
            im = xi_scr[pl.ds(r0, SUBLANES), ls]
            for di, dist in enumerate((1, 2, 4)):
                sr = pltpu.roll(re, dist, 0)
                si = pltpu.roll(im, dist, 0)
                ar, ai = ad_ref[di, 0, :, ls], ad_ref[di, 1, :, ls]
                re, im = re + (ar * sr - ai * si), im + (ar * si + ai * sr)
            pr, pi = ap_ref[0, :, ls], ap_ref[1, :, ls]
            cr, ci = carry[2 * lc], carry[2 * lc + 1]
            re = re + (pr * cr - pi * ci)
            im = im + (pr * ci + pi * cr)
            xr_scr[pl.ds(r0, SUBLANES), ls] = re
            xi_scr[pl.ds(r0, SUBLANES), ls] = im
            new += [re[SUBLANES - 1:SUBLANES], im[SUBLANES - 1:SUBLANES]]
        return tuple(new)

    init = []
    for lc in range(n_state // lane_chunk):
        ls = slice(lc * lane_chunk, (lc + 1) * lane_chunk)
        init += [car_scr[0:1, ls], car_scr[1:2, ls]]
    fin = lax.fori_loop(0, rows // SUBLANES, step, tuple(init))
    for lc in range(n_state // lane_chunk):
        ls = slice(lc * lane_chunk, (lc + 1) * lane_chunk)
        car_scr[0:1, ls] = fin[2 * lc]
        car_scr[1:2, ls] = fin[2 * lc + 1]

    ys = []
    for kt in range(n_kt):
        ss = slice(kt * sw, (kt + 1) * sw)
        ys.append(_dot(xr_scr[:, ss].astype(BF16), cre_ref[kt]) + _dot(xi_scr[:, ss].astype(BF16), cim_ref[kt]))
    y = jnp.concatenate(ys, axis=1) + d_ref[...] * u
    vg = _dot(jax.nn.gelu(y).astype(BF16), wglu_ref[...])
    o_ref[0] = x + vg[:, :d] * jax.nn.sigmoid(vg[:, d:])


def _s5_tables(b_re, b_im, c_re, c_im, log_dt, a_re, a_im):
    ng, ps, cg = b_re.shape
    per_tile = MXU_DIM // cg
    n_kt = ng // per_tile
    dt = jnp.exp(log_dt)[:, None]
    lam_r, lam_i = dt * a_re, dt * a_im
    mag = jnp.exp(lam_r)
    abar_r, abar_i = mag * jnp.cos(lam_i), mag * jnp.sin(lam_i)
    den = a_re * a_re + a_im * a_im
    coef_r = ((abar_r - 1.0) * a_re + abar_i * a_im) / den
    coef_i = (abar_i * a_re - (abar_r - 1.0) * a_im) / den
    bbar_r = coef_r[..., None] * b_re - coef_i[..., None] * b_im
    bbar_i = coef_r[..., None] * b_im + coef_i[..., None] * b_re
    eye = jnp.eye(per_tile, dtype=F32)
    bd_in = lambda m: jnp.einsum('kgpc,gh->kgchp', m.reshape(n_kt, per_tile, ps, cg), eye).reshape(
        n_kt, per_tile * cg, per_tile * ps).astype(BF16)
    bd_out = lambda m: jnp.einsum('kgcp,gh->kgphc', m.reshape(n_kt, per_tile, cg, ps), eye).reshape(
        n_kt, per_tile * ps, per_tile * cg).astype(BF16)

    def power(n):
        pm = jnp.exp(n * lam_r)
        return (pm * jnp.cos(n * lam_i)).reshape(-1), (pm * jnp.sin(n * lam_i)).reshape(-1)

    sub = jnp.arange(SUBLANES)[:, None]
    ad = []
    for dist in (1, 2, 4):
        pr, pi = power(float(dist))
        keep = (sub >= dist).astype(F32)
        ad.append(jnp.stack([keep * pr[None, :], keep * pi[None, :]]))
    ad = jnp.stack(ad)
    rows_p = [power(float(i + 1)) for i in range(SUBLANES)]
    ap = jnp.stack([jnp.stack([p[0] for p in rows_p]), jnp.stack([p[1] for p in rows_p])])
    return bd_in(bbar_r), bd_in(bbar_i), ad, ap, bd_out(c_re), bd_out(-c_im)


def _s5_layer(x, g_norm, w_in, b_re, b_im, c_re, c_im, d_skip, log_dt, a_re, a_im, w_glu, rows=256):
    b, t, d = x.shape
    bre, bim, ad, ap, cre, cim = _s5_tables(b_re, b_im, c_re, c_im, log_dt, a_re, a_im)
    n_state = ad.shape[-1]
    rows = min(rows, t)
    full = lambda a: pl.BlockSpec(a.shape, lambda i, j: (0,) * a.ndim)
    args = (g_norm[None, :], w_in.astype(BF16), bre, bim, ad, ap, cre, cim, d_skip[None, :], w_glu.astype(BF16))
    return pl.pallas_call(
        functools.partial(_s5_kernel, lane_chunk=1024), grid=(b, t // rows),
        in_specs=[pl.BlockSpec((1, rows, d), lambda i, j: (i, j, 0))] + [full(a) for a in args],
        out_specs=pl.BlockSpec((1, rows, d), lambda i, j: (i, j, 0)),
        out_shape=jax.ShapeDtypeStruct((b, t, d), F32),
        scratch_shapes=[pltpu.VMEM((rows, n_state), F32), pltpu.VMEM((rows, n_state), F32),
                        pltpu.VMEM((SUBLANES, n_state), F32)],
        compiler_params=_cparams("parallel", "arbitrary"), name="s5_mixer",
    )(x, *args)


def _pad_heads(w, n_heads, width):
    d = w.shape[0]
    return jnp.pad(w.reshape(d, n_heads, width), ((0, 0), (0, 0), (0, LANES - width))).reshape(d, n_heads * LANES)


def _nsa_layer(x, g_norm, w_in, w_out, q_gain, k_gain, cmp_pos, cmp_w1, cmp_b1, cmp_w2, cmp_b2):
    b, t, d = x.shape
    gn, rh, dh = N_KV_GROUPS, HEADS_PER_GROUP, HEAD_DIM
    assert t % SEL_CHUNK == 0 and t >= SEL_CHUNK + WINDOW
    x2 = x.reshape(b * t, d)
    scale = LOG2E / math.sqrt(dh)

    kvw = lambda i: w_in[:, d + i * KV_DIM:d + (i + 1) * KV_DIM]
    wq = _pad_heads(w_in[:, :d], N_HEADS, dh).astype(BF16)
    wk = jnp.concatenate([_pad_heads(kvw(i), gn, dh) for i in (2, 3, 4, 5)], axis=1).astype(BF16)
    wc = jnp.concatenate([kvw(0), kvw(1)], axis=1).astype(BF16)
    wg = _pad_heads(w_in[:, d + 6 * KV_DIM:], gn, rh * N_BRANCHES).astype(BF16)
    pad_gain = lambda v, n: jnp.tile(jnp.pad(v, (0, LANES - dh)), n)[None, :]
    qg_p = pad_gain(q_gain * scale, N_HEADS)
    kg_p = jnp.concatenate([pad_gain(k_gain[1], gn), pad_gain(k_gain[2], gn)], axis=0)
    seg = jnp.arange(2 * LANES) // LANES
    ebd = (seg[:, None] == seg[None, :]).astype(BF16)

    q, ka, va, kw, vw, kc_raw, vc_raw, gates = _nsa_proj(x2, b, t, g_norm[None, :], wq, wk, wc, wg, qg_p, kg_p, ebd)

    tc = t // CMP_STRIDE
    half = CMP_STRIDE * dh
    blk = lambda a: a.reshape(b, tc, CMP_STRIDE, gn, dh).transpose(0, 3, 1, 2, 4).reshape(b, gn, tc, half)
    comp = []
    for i, (raw, norm) in enumerate(((kc_raw, True), (vc_raw, False))):
        w1 = cmp_w1[i].astype(BF16)
        comp.append(_compress(blk(raw), cmp_pos[i].reshape(2, half), w1[:half], w1[half:], cmp_b1[i][None, :],
                              jnp.pad(cmp_w2[i], ((0, 0), (0, LANES - dh))).astype(BF16),
                              jnp.pad(cmp_b2[i], (0, LANES - dh))[None, :],
                              jnp.pad(k_gain[0], (0, LANES - dh))[None, :], norm))
    kc, vc = comp
    vct = vc.transpose(0, 1, 3, 2)

    nsb = t // SEL_BLOCK
    n_i = jnp.arange(tc)
    j_i = jnp.arange(nsb)
    n_cmp = (t - CMP_BLOCK) // CMP_STRIDE + 1
    ovt = ((n_i[None, :] * CMP_STRIDE < (j_i[:, None] + 1) * SEL_BLOCK)
           & (n_i[None, :] * CMP_STRIDE + CMP_BLOCK > j_i[:, None] * SEL_BLOCK)
           & (n_i[None, :] < n_cmp)).astype(BF16)
    ocmp, selt = _cmp_select(q, kc, vct, ovt)

    vrows = dh + 2 * SUBLANES
    vat = va[..., :vrows].transpose(0, 1, 3, 2)
    vwt = vw[..., :vrows].transpose(0, 1, 3, 2)
    o = _sel_win(q, ka, vat, kw, vwt, selt, ocmp, gates)
    return _res_matmul(o.reshape(b * t, d), w_out.astype(BF16), x2).reshape(b, t, d)


def kernel(x, mix_norm, ffn_norm, nsa_w_in, nsa_w_out, nsa_q_gain, nsa_k_gain, nsa_cmp_pos, nsa_cmp_w1, nsa_cmp_b1, nsa_cmp_w2, nsa_cmp_b2, s5_w_in, s5_b_re, s5_b_im, s5_c_re, s5_c_im, s5_d, s5_log_dt, s5_a_re, s5_a_im, s5_w_glu, ffn_w_gate, ffn_w_up, ffn_w_down):
    b, t, d = x.shape
    depth = mix_norm.shape[0]
    for layer in range(depth):
        i = layer // 2
        if layer % 2 == 0:
            x = _nsa_layer(x, mix_norm[layer], nsa_w_in[i], nsa_w_out[i], nsa_q_gain[i], nsa_k_gain[i],
                           nsa_cmp_pos[i], nsa_cmp_w1[i], nsa_cmp_b1[i], nsa_cmp_w2[i], nsa_cmp_b2[i])
        else:
            x = _s5_layer(x, mix_norm[layer], s5_w_in[i], s5_b_re[i], s5_b_im[i], s5_c_re[i], s5_c_im[i],
                          s5_d[i], s5_log_dt[i], s5_a_re[i], s5_a_im[i], s5_w_glu[i])
        x = _ffn(x.reshape(b * t, d), ffn_norm[layer][None, :], ffn_w_gate[layer].astype(BF16),
                 ffn_w_up[layer].astype(BF16), ffn_w_down[layer].astype(BF16)).reshape(b, t, d)
    return x
```

```python
import functools
import math

import jax
import jax.numpy as jnp
from jax import lax
from jax.experimental import pallas as pl
from jax.experimental.pallas import tpu as pltpu

F32 = jnp.float32
BF16 = jnp.bfloat16

EPS = 1e-6
NEG = -1e30
FORCE = 1e4
MASK_BIAS = -1e9
LOG2E = 1.4426950408889634

N_HEADS = 16
HEAD_DIM = 64
N_KV_GROUPS = 4
HEADS_PER_GROUP = 4
KV_DIM = N_KV_GROUPS * HEAD_DIM
CMP_BLOCK = 32
CMP_STRIDE = 16
SEL_BLOCK = 64
SEL_TOPN = 16
WINDOW = 512
N_BRANCHES = 3
S5_GROUP = 16
S5_STATE = 64

LANES = 128
SUBLANES = 8
MXU_DIM = 256
SEL_CHUNK = 512
V7X_VMEM_LIMIT_BYTES = 56 * 1024 * 1024


def _cparams(*semantics):
    return pltpu.CompilerParams(dimension_semantics=semantics,
                                vmem_limit_bytes=V7X_VMEM_LIMIT_BYTES)


def _rms_rows(x, g):
    var = jnp.mean(x * x, axis=-1, keepdims=True)
    return x * lax.rsqrt(var + EPS) * g


def _dot(a, b):
    return jnp.dot(a, b, preferred_element_type=F32)


def _dot_nt(a, b):
    return lax.dot_general(a, b, (((1,), (1,)), ((), ())), preferred_element_type=F32)


def _seg_sumsq(v, ebd):
    sq = v * v
    hi = sq.astype(BF16)
    lo = (sq - hi.astype(F32)).astype(BF16)
    return _dot(hi, ebd) + _dot(lo, ebd)


def _nsa_proj_kernel(x_ref, g_ref, wq_ref, wk_ref, wc_ref, wg_ref, qg_ref, kg_ref, ebd_ref,
                     q_ref, ka_ref, va_ref, kw_ref, vw_ref, kcr_ref, vcr_ref, gate_ref):
    tm = x_ref.shape[0]
    h = _rms_rows(x_ref[...], g_ref[...]).astype(BF16)
    ebd = ebd_ref[...]
    inv_dh = 1.0 / HEAD_DIM
    slab = 2 * LANES

    q = _dot(h, wq_ref[...])
    qg = qg_ref[...]
    for pair in range(N_HEADS // 2):
        sl = slice(pair * slab, (pair + 1) * slab)
        qi = q[:, sl]
        qn = (qi * lax.rsqrt(_seg_sumsq(qi, ebd) * inv_dh + EPS) * qg[:, sl]).astype(BF16)
        for hh in range(2):
            g, r = divmod(2 * pair + hh, HEADS_PER_GROUP)
            q_ref[0, g, r] = qn[:, hh * LANES:(hh + 1) * LANES]

    row = lax.rem(pl.program_id(0) * tm, SEL_CHUNK) + lax.broadcasted_iota(jnp.int32, (tm, 1), 0)
    lane = lax.broadcasted_iota(jnp.int32, (1, LANES), 1)
    blk_flag = ((row // SEL_BLOCK) == (lane - HEAD_DIM)).astype(F32)
    one_flag = (lane == HEAD_DIM).astype(F32)

    kv = _dot(h, wk_ref[...])
    kg = kg_ref[...]
    plan = ((ka_ref, 0, blk_flag), (va_ref, None, one_flag), (kw_ref, 1, None), (vw_ref, None, one_flag))
    for part, (ref, gain_row, flag) in enumerate(plan):
        for half in range(2):
            base = part * N_KV_GROUPS * LANES + half * slab
            v = kv[:, base:base + slab]
            if gain_row is not None:
                v = v * lax.rsqrt(_seg_sumsq(v, ebd) * inv_dh + EPS) * kg[gain_row:gain_row + 1, half * slab:(half + 1) * slab]
            for hh in range(2):
                piece = v[:, hh * LANES:(hh + 1) * LANES]
                if flag is not None:
                    piece = piece + flag
                ref[0, 2 * half + hh] = piece.astype(BF16)

    c = _dot(h, wc_ref[...])
    kcr_ref[...] = c[:, :KV_DIM]
    vcr_ref[...] = c[:, KV_DIM:]
    gt = jax.nn.sigmoid(_dot(h, wg_ref[...]))
    for g in range(N_KV_GROUPS):
        gate_ref[0, g] = gt[:, g * LANES:(g + 1) * LANES]


def _nsa_proj(x2, b, t, g, wq, wk, wc, wg, qg_p, kg_p, ebd, tm=512):
    m, d = x2.shape
    assert SEL_CHUNK % tm == 0 and t % SEL_CHUNK == 0
    nt = t // tm
    gn, rh = N_KV_GROUPS, HEADS_PER_GROUP
    full = lambda a: pl.BlockSpec(a.shape, lambda i: (0,) * a.ndim)
    row = lambda n: pl.BlockSpec((tm, n), lambda i: (i, 0))
    head = pl.BlockSpec((1, gn, tm, LANES), lambda i: (i // nt, 0, i % nt, 0))
    outs = [jax.ShapeDtypeStruct((b, gn, rh, t, LANES), BF16)]
    outs += [jax.ShapeDtypeStruct((b, gn, t, LANES), BF16)] * 4
    outs += [jax.ShapeDtypeStruct((m, KV_DIM), F32)] * 2
    outs += [jax.ShapeDtypeStruct((b, gn, t, LANES), F32)]
    return pl.pallas_call(
        _nsa_proj_kernel, grid=(m // tm,),
        in_specs=[row(d), full(g), full(wq), full(wk), full(wc), full(wg), full(qg_p), full(kg_p), full(ebd)],
        out_specs=[pl.BlockSpec((1, gn, rh, tm, LANES), lambda i: (i // nt, 0, 0, i % nt, 0)),
                   head, head, head, head, row(KV_DIM), row(KV_DIM), head],
        out_shape=outs, compiler_params=_cparams("parallel"), name="nsa_proj",
    )(x2, g, wq, wk, wc, wg, qg_p, kg_p, ebd)


def _compress_kernel(a_ref, pos_ref, w1a_ref, w1b_ref, b1_ref, w2_ref, b2_ref, gain_ref, o_ref, *, norm):
    a = a_ref[0, 0]
    tc = a.shape[0]
    pos = pos_ref[...]
    h1 = _dot((a + pos[0:1]).astype(BF16), w1a_ref[...])
    h2 = _dot((a + pos[1:2]).astype(BF16), w1b_ref[...])
    h = h1 + pltpu.roll(h2, tc - 1, 0) + b1_ref[...]
    hid = jax.nn.gelu(h).astype(BF16)
    out = _dot(hid, w2_ref[...]) + b2_ref[...]
    if norm:
        var = jnp.sum(out * out, axis=-1, keepdims=True) * (1.0 / HEAD_DIM)
        out = out * lax.rsqrt(var + EPS) * gain_ref[...]
    o_ref[0, 0] = out.astype(o_ref.dtype)


def _compress(a, pos2, w1a, w1b, b1, w2, b2, gain, norm):
    b, g, tc, w = a.shape
    full = lambda z: pl.BlockSpec(z.shape, lambda i, j: (0,) * z.ndim)
    return pl.pallas_call(
        functools.partial(_compress_kernel, norm=norm), grid=(b, g),
        in_specs=[pl.BlockSpec((1, 1, tc, w), lambda i, j: (i, j, 0, 0)),
                  full(pos2), full(w1a), full(w1b), full(b1), full(w2), full(b2), full(gain)],
        out_specs=pl.BlockSpec((1, 1, tc, LANES), lambda i, j: (i, j, 0, 0)),
        out_shape=jax.ShapeDtypeStruct((b, g, tc, LANES), BF16),
        compiler_params=_cparams("parallel", "parallel"), name="nsa_compress",
    )(a, pos2, w1a, w1b, b1, w2, b2, gain)


def _cmp_select_kernel(q_ref, kc_ref, vct_ref, ovt_ref, ocmp_ref, sel_ref, bias_scr, *, top_n, n_tiers):
    tq = q_ref.shape[3]
    tc = kc_ref.shape[2]
    nsb = ovt_ref.shape[0]
    n_tiles = pl.num_programs(1)
    t = pl.program_id(1) * tq + lax.broadcasted_iota(jnp.int32, (1, tq), 1)
    cur = t // SEL_BLOCK
    any_vis = t >= CMP_BLOCK - 1

    def tile_body(rows, blocks):
        n_idx = lax.broadcasted_iota(jnp.int32, (rows, 1), 0)
        bias = jnp.where((n_idx * CMP_STRIDE + (CMP_BLOCK - 1)) <= t, 0.0, NEG)
        j_idx = lax.broadcasted_iota(jnp.int32, (blocks, 1), 0)
        forced = (j_idx == 0) | (j_idx == cur) | (j_idx == cur - 1)
        valid = j_idx <= cur
        j_f = j_idx.astype(F32)
        bias_scr[:rows, :] = bias

        def group_body(g, carry):
            kc = kc_ref[0, g, :rows, :]
            vct = vct_ref[0, g, :, :rows]
            psum = jnp.zeros((rows, tq), F32)
            for r in range(HEADS_PER_GROUP):
                s = _dot_nt(kc, q_ref[0, g, r]) + bias_scr[:rows, :]
                e = jnp.exp2(s - jnp.max(s, axis=0, keepdims=True))
                den = jnp.sum(e, axis=0, keepdims=True)
                p = e * jnp.where(any_vis, 1.0 / den, 0.0)
                o_t = _dot(vct, p.astype(BF16))
                ocmp_ref[0, g, r] = o_t[:HEAD_DIM].T
                psum = psum + p
            hi = psum.astype(BF16)
            lo = (psum - hi.astype(F32)).astype(BF16)
            ovt = ovt_ref[:blocks, :rows]
            psel = _dot(ovt, hi) + _dot(ovt, lo)
            score = jnp.where(forced, FORCE, jnp.where(valid, psel, NEG))
            for _ in range(top_n):
                mx = jnp.max(score, axis=0, keepdims=True)
                first = jnp.min(jnp.where(score == mx, j_f, float(nsb)), axis=0, keepdims=True)
                score = jnp.where(j_f == first, -jnp.inf, score)
            sel_ref[0, g, :blocks, :] = jnp.where((score == -jnp.inf) & valid, 1.0, 0.0)
            if blocks < nsb:
                sel_ref[0, g, blocks:, :] = jnp.zeros((nsb - blocks, tq), F32)
            return carry

        lax.fori_loop(0, N_KV_GROUPS, group_body, 0)

    tier = (pl.program_id(1) * n_tiers) // n_tiles
    for k in range(n_tiers):
        pl.when(tier == k)(functools.partial(tile_body, tc * (k + 1) // n_tiers, nsb * (k + 1) // n_tiers))


def _cmp_select(q, kc, vct, ovt, tq=256):
    b, g, r, t_len, _ = q.shape
    tc = kc.shape[2]
    nsb = ovt.shape[0]
    tq = min(tq, t_len)
    top_n = min(SEL_TOPN, nsb)
    n_tiers = 4
    assert (t_len // tq) % n_tiers == 0 and nsb % (n_tiers * SUBLANES) == 0 and tc % (n_tiers * LANES // 4) == 0
    return pl.pallas_call(
        functools.partial(_cmp_select_kernel, top_n=top_n, n_tiers=n_tiers), grid=(b, t_len // tq),
        in_specs=[pl.BlockSpec((1, g, r, tq, LANES), lambda i, j: (i, 0, 0, j, 0)),
                  pl.BlockSpec((1, g, tc, LANES), lambda i, j: (i, 0, 0, 0)),
                  pl.BlockSpec((1, g, LANES, tc), lambda i, j: (i, 0, 0, 0)),
                  pl.BlockSpec(ovt.shape, lambda i, j: (0, 0))],
        out_specs=[pl.BlockSpec((1, g, r, tq, HEAD_DIM), lambda i, j: (i, 0, 0, j, 0)),
                   pl.BlockSpec((1, g, nsb, tq), lambda i, j: (i, 0, 0, j))],
        out_shape=[jax.ShapeDtypeStruct((b, g, r, t_len, HEAD_DIM), F32),
                   jax.ShapeDtypeStruct((b, g, nsb, t_len), F32)],
        scratch_shapes=[pltpu.VMEM((tc, tq), F32)],
        compiler_params=_cparams("parallel", "parallel"), name="nsa_cmp_select",
    )(q, kc, vct, ovt)


def _sel_win_kernel(q_ref, ka_ref, vat_ref, kw_ref, vwt_ref, selt_ref, ocmp_ref, gate_ref, o_ref,
                    qa_scr, sa_scr, sb_scr, ma_scr, mb_scr, *, kc):
    r_heads, tq = q_ref.shape[2], q_ref.shape[3]
    dh = HEAD_DIM
    cols = r_heads * tq
    t_len = ka_ref.shape[2]
    per_chunk = kc // SEL_BLOCK
    aug_rows = -(-per_chunk // (2 * SUBLANES)) * (2 * SUBLANES)
    t0 = pl.program_id(2) * tq
    t_lane = t0 + lax.broadcasted_iota(jnp.int32, (1, tq), 1)

    for r in range(r_heads):
        qa_scr[:, r * tq:(r + 1) * tq] = q_ref[0, 0, r].astype(F32).T.astype(BF16)

    def softmax_t(s, m_new):
        return jnp.exp2((s - m_new).astype(BF16))

    span = min(tq + WINDOW, t_len)
    w0 = pl.multiple_of(jnp.maximum(t0 + tq - span, 0), tq)
    kpos_w = w0 + lax.broadcasted_iota(jnp.int32, (span, 1), 0)
    in_win = (kpos_w <= t_lane) & (t_lane - kpos_w < WINDOW)
    bias_w = jnp.where(in_win, 0.0, MASK_BIAS)
    s = _dot(kw_ref[0, 0, pl.ds(w0, span), :], qa_scr[...]) + jnp.concatenate([bias_w] * r_heads, axis=1)
    ow = _dot(vwt_ref[0, 0, :, pl.ds(w0, span)], softmax_t(s, jnp.max(s, axis=0, keepdims=True)))
    o_win_t = ow[:dh] / ow[dh:dh + 1]

    n_full = t0 // kc
    last_full = jnp.maximum(n_full - 1, 0)

    def scores(c):
        ci = jnp.minimum(c, n_full)
        sel8 = selt_ref[0, 0, pl.ds(pl.multiple_of(ci * per_chunk, per_chunk), per_chunk), :]
        sb = (jnp.where(c <= n_full, sel8, 0.0) - 1.0) * (-MASK_BIAS)
        if aug_rows > per_chunk:
            sb = jnp.concatenate([sb, jnp.zeros((aug_rows - per_chunk, tq), F32)], axis=0)
        qa_scr[dh:dh + aug_rows, :] = jnp.concatenate([sb.astype(BF16)] * r_heads, axis=1)
        return _dot(ka_ref[0, 0, pl.ds(pl.multiple_of(ci * kc, kc), kc), :], qa_scr[...])

    def update(s, s_max, c, m, acc):
        ci = jnp.minimum(c, n_full)
        m_new = jnp.maximum(m, s_max)
        alpha = jnp.exp2(m - m_new)
        pv = _dot(vat_ref[0, 0, :, pl.ds(pl.multiple_of(ci * kc, kc), kc)], softmax_t(s, m_new))
        return m_new, alpha * acc + pv

    kpos = n_full * kc + lax.broadcasted_iota(jnp.int32, (kc, 1), 0)
    causal = jnp.where(kpos <= t_lane, 0.0, MASK_BIAS)
    s_diag = scores(n_full) + jnp.concatenate([causal] * r_heads, axis=1)
    init = (jnp.full((1, cols), NEG, F32), jnp.zeros((vat_ref.shape[2], cols), F32))
    carry = update(s_diag, jnp.max(s_diag, axis=0, keepdims=True), n_full, *init)

    def stash(c, s_scr, m_scr):
        s = scores(jnp.where(c < n_full, c, n_full + 1))
        s_scr[...] = s
        m_scr[0:1, :] = jnp.max(s, axis=0, keepdims=True)

    stash(0, sa_scr, ma_scr)

    def body(i, carry):
        c = 2 * i
        stash(c + 1, sb_scr, mb_scr)
        carry = update(sa_scr[...], ma_scr[0:1, :], jnp.minimum(c, last_full), *carry)
        stash(c + 2, sa_scr, ma_scr)
        return update(sb_scr[...], mb_scr[0:1, :], jnp.minimum(c + 1, last_full), *carry)

    _, acc = lax.fori_loop(0, (n_full + 1) // 2, body, carry)
    o_sel_t = acc[:dh] / acc[dh:dh + 1]

    gates = gate_ref[0, 0]
    for r in range(r_heads):
        cs = slice(r * tq, (r + 1) * tq)
        o_r = (gates[:, 3 * r:3 * r + 1] * ocmp_ref[0, 0, r]
               + gates[:, 3 * r + 1:3 * r + 2] * o_sel_t[:, cs].T
               + gates[:, 3 * r + 2:3 * r + 3] * o_win_t[:, cs].T)
        o_ref[0, :, r * dh:(r + 1) * dh] = o_r.astype(o_ref.dtype)


def _sel_win(q, ka, vat, kw, vwt, selt, ocmp, gates, tq=256, kc=SEL_CHUNK):
    b, g, r, t_len, _ = q.shape
    nsb = selt.shape[2]
    vrows = vat.shape[2]
    gw = r * HEAD_DIM
    k_spec = pl.BlockSpec((1, 1, t_len, LANES), lambda i, j, k: (i, j, 0, 0))
    vt_spec = pl.BlockSpec((1, 1, vrows, t_len), lambda i, j, k: (i, j, 0, 0))
    return pl.pallas_call(
        functools.partial(_sel_win_kernel, kc=kc), grid=(b, g, t_len // tq),
        in_specs=[pl.BlockSpec((1, 1, r, tq, LANES), lambda i, j, k: (i, j, 0, k, 0)),
                  k_spec, vt_spec, k_spec, vt_spec,
                  pl.BlockSpec((1, 1, nsb, tq), lambda i, j, k: (i, j, 0, k)),
                  pl.BlockSpec((1, 1, r, tq, HEAD_DIM), lambda i, j, k: (i, j, 0, k, 0)),
                  pl.BlockSpec((1, 1, tq, LANES), lambda i, j, k: (i, j, k, 0))],
        out_specs=pl.BlockSpec((1, tq, gw), lambda i, j, k: (i, k, j)),
        out_shape=jax.ShapeDtypeStruct((b, t_len, g * gw), BF16),
        scratch_shapes=[pltpu.VMEM((LANES, r * tq), BF16),
                        pltpu.VMEM((kc, r * tq), F32), pltpu.VMEM((kc, r * tq), F32),
                        pltpu.VMEM((SUBLANES, r * tq), F32), pltpu.VMEM((SUBLANES, r * tq), F32)],
        compiler_params=_cparams("parallel", "parallel", "arbitrary"), name="nsa_sel_win",
    )(q, ka, vat, kw, vwt, selt, ocmp, gates)


def _res_matmul_kernel(a_ref, w_ref, x_ref, o_ref):
    o_ref[...] = x_ref[...] + _dot(a_ref[...], w_ref[...])


def _res_matmul(a, w, x2, tm=512):
    m, k = a.shape
    n = w.shape[1]
    return pl.pallas_call(
        _res_matmul_kernel, grid=(m // tm,),
        in_specs=[pl.BlockSpec((tm, k), lambda i: (i, 0)), pl.BlockSpec((k, n), lambda i: (0, 0)),
                  pl.BlockSpec((tm, n), lambda i: (i, 0))],
        out_specs=pl.BlockSpec((tm, n), lambda i: (i, 0)),
        out_shape=jax.ShapeDtypeStruct((m, n), F32),
        compiler_params=_cparams("parallel"), name="res_matmul",
    )(a, w, x2)


def _ffn_kernel(x_ref, g_ref, wg_ref, wu_ref, wd_ref, o_ref, *, hc):
    x = x_ref[...]
    h = _rms_rows(x, g_ref[...]).astype(BF16)
    acc = x
    for c in range(wg_ref.shape[1] // hc):
        sl = slice(c * hc, (c + 1) * hc)
        a = jax.nn.silu(_dot(h, wg_ref[:, sl])) * _dot(h, wu_ref[:, sl])
        acc = acc + _dot(a.astype(BF16), wd_ref[sl, :])
    o_ref[...] = acc


def _ffn(x2, g, wg, wu, wd, tm=512, hc=MXU_DIM):
    m, d = x2.shape
    full = lambda a: pl.BlockSpec(a.shape, lambda i: (0,) * a.ndim)
    return pl.pallas_call(
        functools.partial(_ffn_kernel, hc=hc), grid=(m // tm,),
        in_specs=[pl.BlockSpec((tm, d), lambda i: (i, 0)), full(g), full(wg), full(wu), full(wd)],
        out_specs=pl.BlockSpec((tm, d), lambda i: (i, 0)),
        out_shape=jax.ShapeDtypeStruct((m, d), F32),
        compiler_params=_cparams("parallel"), name="ffn_swiglu",
    )(x2, g, wg, wu, wd)


def _s5_kernel(x_ref, g_ref, win_ref, bre_ref, bim_ref, ad_ref, ap_ref, cre_ref, cim_ref, d_ref, wglu_ref,
               o_ref, xr_scr, xi_scr, car_scr, *, lane_chunk):
    rows, d = x_ref.shape[1], x_ref.shape[2]
    n_state = xr_scr.shape[1]
    n_kt = bre_ref.shape[0]
    kw = bre_ref.shape[1]
    sw = bre_ref.shape[2]

    @pl.when(pl.program_id(1) == 0)
    def _():
        car_scr[...] = jnp.zeros_like(car_scr)

    x = x_ref[0]
    u = _dot(_rms_rows(x, g_ref[...]).astype(BF16), win_ref[...])
    ub = u.astype(BF16)
    for kt in range(n_kt):
        uk = ub[:, kt * kw:(kt + 1) * kw]
        xr_scr[:, kt * sw:(kt + 1) * sw] = _dot(uk, bre_ref[kt])
        xi_scr[:, kt * sw:(kt + 1) * sw] = _dot(uk, bim_ref[kt])

    def step(j, carry):
        r0 = pl.multiple_of(j * SUBLANES, SUBLANES)
        new = []
        for lc in range(n_state // lane_chunk):
            ls = slice(lc * lane_chunk, (lc + 1) * lane_chunk)
            re = xr_scr[pl.ds(r0, SUBLANES), ls]
            im = xi_scr[pl.ds(r0, SUBLANES), ls]
            for di, dist in enumerate((1, 2, 4)):
                sr = pltpu.roll(re, dist, 0)
                si = pltpu.roll(im, dist, 0)
                ar, ai = ad_ref[di, 0, :, ls], ad_ref[di, 1, :, ls]
                re, im = re + (ar * sr - ai * si), im + (ar * si + ai * sr)
            pr, pi = ap_ref[0, :, ls], ap_ref[1, :, ls]
            cr, ci = carry[2 * lc], carry[2 * lc + 1]
            re = re + (pr * cr - pi * ci)
            im = im + (pr * ci + pi * cr)
            xr_scr[pl.ds(r0, SUBLANES), ls] = re
            xi_scr[pl.ds(r0, SUBLANES), ls] = im
            new += [re[SUBLANES - 1:SUBLANES], im[SUBLANES - 1:SUBLANES]]
        return tuple(new)

    init = []
    for lc in range(n_state // lane_chunk):
        ls = slice(lc * lane_chunk, (lc + 1) * lane_chunk)
        init += [car_scr[0:1, ls], car_scr[1:2, ls]]
    fin = lax.fori_loop(0, rows // SUBLANES, step, tuple(init))
    for lc in range(n_state // lane_chunk):
        ls = slice(lc * lane_chunk, (lc + 1) * lane_chunk)
        car_scr[0:1, ls] = fin[2 * lc]
        car_scr[1:2, ls] = fin[2 * lc + 1]

    ys = []
    for kt in range(n_kt):
        ss = slice(kt * sw, (kt + 1) * sw)
        ys.append(_dot(xr_scr[:, ss].astype(BF16), cre_ref[kt]) + _dot(xi_scr[:, ss].astype(BF16), cim_ref[kt]))
    y = jnp.concatenate(ys, axis=1) + d_ref[...] * u
    vg = _dot(jax.nn.gelu(y).astype(BF16), wglu_ref[...])
    o_ref[0] = x + vg[:, :d] * jax.nn.sigmoid(vg[:, d:])


def _s5_tables(b_re, b_im, c_re, c_im, log_dt, a_re, a_im):
    ng, ps, cg = b_re.shape
    per_tile = MXU_DIM // cg
    n_kt = ng // per_tile
    dt = jnp.exp(log_dt)[:, None]
    lam_r, lam_i = dt * a_re, dt * a_im
    mag = jnp.exp(lam_r)
    abar_r, abar_i = mag * jnp.cos(lam_i), mag * jnp.sin(lam_i)
    den = a_re * a_re + a_im * a_im
    coef_r = ((abar_r - 1.0) * a_re + abar_i * a_im) / den
    coef_i = (abar_i * a_re - (abar_r - 1.0) * a_im) / den
    bbar_r = coef_r[..., None] * b_re - coef_i[..., None] * b_im
    bbar_i = coef_r[..., None] * b_im + coef_i[..., None] * b_re
    eye = jnp.eye(per_tile, dtype=F32)
    bd_in = lambda m: jnp.einsum('kgpc,gh->kgchp', m.reshape(n_kt, per_tile, ps, cg), eye).reshape(
        n_kt, per_tile * cg, per_tile * ps).astype(BF16)
    bd_out = lambda m: jnp.einsum('kgcp,gh->kgphc', m.reshape(n_kt, per_tile, cg, ps), eye).reshape(
        n_kt, per_tile * ps, per_tile * cg).astype(BF16)

    def power(n):
        pm = jnp.exp(n * lam_r)
        return (pm * jnp.cos(n * lam_i)).reshape(-1), (pm * jnp.sin(n * lam_i)).reshape(-1)

    sub = jnp.arange(SUBLANES)[:, None]
    ad = []
    for dist in (1, 2, 4):
        pr, pi = power(float(dist))
        keep = (sub >= dist).astype(F32)
        ad.append(jnp.stack([keep * pr[None, :], keep * pi[None, :]]))
    ad = jnp.stack(ad)
    rows_p = [power(float(i + 1)) for i in range(SUBLANES)]
    ap = jnp.stack([jnp.stack([p[0] for p in rows_p]), jnp.stack([p[1] for p in rows_p])])
    return bd_in(bbar_r), bd_in(bbar_i), ad, ap, bd_out(c_re), bd_out(-c_im)


def _s5_layer(x, g_norm, w_in, b_re, b_im, c_re, c_im, d_skip, log_dt, a_re, a_im, w_glu, rows=256):
    b, t, d = x.shape
    bre, bim, ad, ap, cre, cim = _s5_tables(b_re, b_im, c_re, c_im, log_dt, a_re, a_im)
    n_state = ad.shape[-1]
    rows = min(rows, t)
    full = lambda a: pl.BlockSpec(a.shape, lambda i, j: (0,) * a.ndim)
    args = (g_norm[None, :], w_in.astype(BF16), bre, bim, ad, ap, cre, cim, d_skip[None, :], w_glu.astype(BF16))
    return pl.pallas_call(
        functools.partial(_s5_kernel, lane_chunk=1024), grid=(b, t // rows),
        in_specs=[pl.BlockSpec((1, rows, d), lambda i, j: (i, j, 0))] + [full(a) for a in args],
        out_specs=pl.BlockSpec((1, rows, d), lambda i, j: (i, j, 0)),
        out_shape=jax.ShapeDtypeStruct((b, t, d), F32),
        scratch_shapes=[pltpu.VMEM((rows, n_state), F32), pltpu.VMEM((rows, n_state), F32),
                        pltpu.VMEM((SUBLANES, n_state), F32)],
        compiler_params=_cparams("parallel", "arbitrary"), name="s5_mixer",
    )(x, *args)


def _pad_heads(w, n_heads, width):
    d = w.shape[0]
    return jnp.pad(w.reshape(d, n_heads, width), ((0, 0), (0, 0), (0, LANES - width))).reshape(d, n_heads * LANES)


def _nsa_layer(x, g_norm, w_in, w_out, q_gain, k_gain, cmp_pos, cmp_w1, cmp_b1, cmp_w2, cmp_b2):
    b, t, d = x.shape
    gn, rh, dh = N_KV_GROUPS, HEADS_PER_GROUP, HEAD_DIM
    assert t % SEL_CHUNK == 0 and t >= SEL_CHUNK + WINDOW
    x2 = x.reshape(b * t, d)
    scale = LOG2E / math.sqrt(dh)

    kvw = lambda i: w_in[:, d + i * KV_DIM:d + (i + 1) * KV_DIM]
    wq = _pad_heads(w_in[:, :d], N_HEADS, dh).astype(BF16)
    wk = jnp.concatenate([_pad_heads(kvw(i), gn, dh) for i in (2, 3, 4, 5)], axis=1).astype(BF16)
    wc = jnp.concatenate([kvw(0), kvw(1)], axis=1).astype(BF16)
    wg = _pad_heads(w_in[:, d + 6 * KV_DIM:], gn, rh * N_BRANCHES).astype(BF16)
    pad_gain = lambda v, n: jnp.tile(jnp.pad(v, (0, LANES - dh)), n)[None, :]
    qg_p = pad_gain(q_gain * scale, N_HEADS)
    kg_p = jnp.concatenate([pad_gain(k_gain[1], gn), pad_gain(k_gain[2], gn)], axis=0)
    seg = jnp.arange(2 * LANES) // LANES
    ebd = (seg[:, None] == seg[None, :]).astype(BF16)

    q, ka, va, kw, vw, kc_raw, vc_raw, gates = _nsa_proj(x2, b, t, g_norm[None, :], wq, wk, wc, wg, qg_p, kg_p, ebd)

    tc = t // CMP_STRIDE
    half = CMP_STRIDE * dh
    blk = lambda a: a.reshape(b, tc, CMP_STRIDE, gn, dh).transpose(0, 3, 1, 2, 4).reshape(b, gn, tc, half)
    comp = []
    for i, (raw, norm) in enumerate(((kc_raw, True), (vc_raw, False))):
        w1 = cmp_w1[i].astype(BF16)
        comp.append(_compress(blk(raw), cmp_pos[i].reshape(2, half), w1[:half], w1[half:], cmp_b1[i][None, :],
                              jnp.pad(cmp_w2[i], ((0, 0), (0, LANES - dh))).astype(BF16),
                              jnp.pad(cmp_b2[i], (0, LANES - dh))[None, :],
                              jnp.pad(k_gain[0], (0, LANES - dh))[None, :], norm))
    kc, vc = comp
    vct = vc.transpose(0, 1, 3, 2)

    nsb = t // SEL_BLOCK
    n_i = jnp.arange(tc)
    j_i = jnp.arange(nsb)
    n_cmp = (t - CMP_BLOCK) // CMP_STRIDE + 1
    ovt = ((n_i[None, :] * CMP_STRIDE < (j_i[:, None] + 1) * SEL_BLOCK)
           & (n_i[None, :] * CMP_STRIDE + CMP_BLOCK > j_i[:, None] * SEL_BLOCK)
           & (n_i[None, :] < n_cmp)).astype(BF16)
    ocmp, selt = _cmp_select(q, kc, vct, ovt)

    vrows = dh + 2 * SUBLANES
    vat = va[..., :vrows].transpose(0, 1, 3, 2)
    vwt = vw[..., :vrows].transpose(0, 1, 3, 2)
    o = _sel_win(q, ka, vat, kw, vwt, selt, ocmp, gates)
    return _res_matmul(o.reshape(b * t, d), w_out.astype(BF16), x2).reshape(b, t, d)


def kernel(x, mix_norm, ffn_norm, nsa_w_in, nsa_w_out, nsa_q_gain, nsa_k_gain, nsa_cmp_pos, nsa_cmp_w1, nsa_cmp_b1, nsa_cmp_w2, nsa_cmp_b2, s5_w_in, s5_b_re, s5_b_im, s5_c_re, s5_c_im, s5_d, s5_log_dt, s5_a_re, s5_a_im, s5_w_glu, ffn_w_gate, ffn_w_up, ffn_w_down):
    b, t, d = x.shape
    depth = mix_norm.shape[0]
    for layer in range(depth):
        i = layer // 2
        if layer % 2 == 0:
            x = _nsa_layer(x, mix_norm[layer], nsa_w_in[i], nsa_w_out[i], nsa_q_gain[i], nsa_k_gain[i],
                           nsa_cmp_pos[i], nsa_cmp_w1[i], nsa_cmp_b1[i], nsa_cmp_w2[i], nsa_cmp_b2[i])
        else:
            x = _s5_layer(x, mix_norm[layer], s5_w_in[i], s5_b_re[i], s5_b_im[i], s5_c_re[i], s5_c_im[i],
                          s5_d[i], s5_log_dt[i], s5_a_re[i], s5_a_im[i], s5_w_glu[i])
        x = _ffn(x.reshape(b * t, d), ffn_norm[layer][None, :], ffn_w_gate[layer].astype(BF16),
                 ffn_w_up[layer].astype(BF16), ffn_w_down[layer].astype(BF16)).reshape(b, t, d)
    return x
```

```python
import functools
import math

import jax
import jax.numpy as jnp
from jax import lax
from jax.experimental import pallas as pl
from jax.experimental.pallas import tpu as pltpu

F32 = jnp.float32
BF16 = jnp.bfloat16

EPS = 1e-6
NEG = -1e30
FORCE = 1e4
MASK_BIAS = -1e9
LOG2E = 1.4426950408889634

N_HEADS = 16
HEAD_DIM = 64
N_KV_GROUPS = 4
HEADS_PER_GROUP = 4
KV_DIM = N_KV_GROUPS * HEAD_DIM
CMP_BLOCK = 32
CMP_STRIDE = 16
SEL_BLOCK = 64
SEL_TOPN = 16
WINDOW = 512
N_BRANCHES = 3
S5_GROUP = 16
S5_STATE = 64

LANES = 128
SUBLANES = 8
MXU_DIM = 256
SEL_CHUNK = 512
V7X_VMEM_LIMIT_BYTES = 56 * 1024 * 1024


def _cparams(*semantics):
    return pltpu.CompilerParams(dimension_semantics=semantics,
                                vmem_limit_bytes=V7X_VMEM_LIMIT_BYTES)


def _rms_rows(x, g):
    var = jnp.mean(x * x, axis=-1, keepdims=True)
    return x * lax.rsqrt(var + EPS) * g


def _dot(a, b):
    return jnp.dot(a, b, preferred_element_type=F32)


def _dot_nt(a, b):
    return lax.dot_general(a, b, (((1,), (1,)), ((), ())), preferred_element_type=F32)


def _seg_sumsq(v, ebd):
    sq = v * v
    hi = sq.astype(BF16)
    lo = (sq - hi.astype(F32)).astype(BF16)
    return _dot(hi, ebd) + _dot(lo, ebd)


def _nsa_proj_kernel(x_ref, g_ref, wq_ref, wk_ref, wc_ref, wg_ref, qg_ref, kg_ref, ebd_ref,
                     q_ref, ka_ref, va_ref, kw_ref, vw_ref, kcr_ref, vcr_ref, gate_ref):
    tm = x_ref.shape[0]
    h = _rms_rows(x_ref[...], g_ref[...]).astype(BF16)
    ebd = ebd_ref[...]
    inv_dh = 1.0 / HEAD_DIM
    slab = 2 * LANES

    q = _dot(h, wq_ref[...])
    qg = qg_ref[...]
    for pair in range(N_HEADS // 2):
        sl = slice(pair * slab, (pair + 1) * slab)
        qi = q[:, sl]
        qn = (qi * lax.rsqrt(_seg_sumsq(qi, ebd) * inv_dh + EPS) * qg[:, sl]).astype(BF16)
        for hh in range(2):
            g, r = divmod(2 * pair + hh, HEADS_PER_GROUP)
            q_ref[0, g, r] = qn[:, hh * LANES:(hh + 1) * LANES]

    row = lax.rem(pl.program_id(0) * tm, SEL_CHUNK) + lax.broadcasted_iota(jnp.int32, (tm, 1), 0)
    lane = lax.broadcasted_iota(jnp.int32, (1, LANES), 1)
    blk_flag = ((row // SEL_BLOCK) == (lane - HEAD_DIM)).astype(F32)
    one_flag = (lane == HEAD_DIM).astype(F32)

    kv = _dot(h, wk_ref[...])
    kg = kg_ref[...]
    plan = ((ka_ref, 0, blk_flag), (va_ref, None, one_flag), (kw_ref, 1, None), (vw_ref, None, one_flag))
    for part, (ref, gain_row, flag) in enumerate(plan):
        for half in range(2):
            base = part * N_KV_GROUPS * LANES + half * slab
            v = kv[:, base:base + slab]
            if gain_row is not None:
                v = v * lax.rsqrt(_seg_sumsq(v, ebd) * inv_dh + EPS) * kg[gain_row:gain_row + 1, half * slab:(half + 1) * slab]
            for hh in range(2):
                piece = v[:, hh * LANES:(hh + 1) * LANES]
                if flag is not None:
                    piece = piece + flag
                ref[0, 2 * half + hh] = piece.astype(BF16)

    c = _dot(h, wc_ref[...])
    kcr_ref[...] = c[:, :KV_DIM]
    vcr_ref[...] = c[:, KV_DIM:]
    gt = jax.nn.sigmoid(_dot(h, wg_ref[...]))
    for g in range(N_KV_GROUPS):
        gate_ref[0, g] = gt[:, g * LANES:(g + 1) * LANES]


def _nsa_proj(x2, b, t, g, wq, wk, wc, wg, qg_p, kg_p, ebd, tm=512):
    m, d = x2.shape
    assert SEL_CHUNK % tm == 0 and t % SEL_CHUNK == 0
    nt = t // tm
    gn, rh = N_KV_GROUPS, HEADS_PER_GROUP
    full = lambda a: pl.BlockSpec(a.shape, lambda i: (0,) * a.ndim)
    row = lambda n: pl.BlockSpec((tm, n), lambda i: (i, 0))
    head = pl.BlockSpec((1, gn, tm, LANES), lambda i: (i // nt, 0, i % nt, 0))
    outs = [jax.ShapeDtypeStruct((b, gn, rh, t, LANES), BF16)]
    outs += [jax.ShapeDtypeStruct((b, gn, t, LANES), BF16)] * 4
    outs += [jax.ShapeDtypeStruct((m, KV_DIM), F32)] * 2
    outs += [jax.ShapeDtypeStruct((b, gn, t, LANES), F32)]
    return pl.pallas_call(
        _nsa_proj_kernel, grid=(m // tm,),
        in_specs=[row(d), full(g), full(wq), full(wk), full(wc), full(wg), full(qg_p), full(kg_p), full(ebd)],
        out_specs=[pl.BlockSpec((1, gn, rh, tm, LANES), lambda i: (i // nt, 0, 0, i % nt, 0)),
                   head, head, head, head, row(KV_DIM), row(KV_DIM), head],
        out_shape=outs, compiler_params=_cparams("parallel"), name="nsa_proj",
    )(x2, g, wq, wk, wc, wg, qg_p, kg_p, ebd)


def _compress_kernel(a_ref, pos_ref, w1a_ref, w1b_ref, b1_ref, w2_ref, b2_ref, gain_ref, o_ref, *, norm):
    a = a_ref[0, 0]
    tc = a.shape[0]
    pos = pos_ref[...]
    h1 = _dot((a + pos[0:1]).astype(BF16), w1a_ref[...])
    h2 = _dot((a + pos[1:2]).astype(BF16), w1b_ref[...])
    h = h1 + pltpu.roll(h2, tc - 1, 0) + b1_ref[...]
    hid = jax.nn.gelu(h).astype(BF16)
    out = _dot(hid, w2_ref[...]) + b2_ref[...]
    if norm:
        var = jnp.sum(out * out, axis=-1, keepdims=True) * (1.0 / HEAD_DIM)
        out = out * lax.rsqrt(var + EPS) * gain_ref[...]
    o_ref[0, 0] = out.astype(o_ref.dtype)


def _compress(a, pos2, w1a, w1b, b1, w2, b2, gain, norm):
    b, g, tc, w = a.shape
    full = lambda z: pl.BlockSpec(z.shape, lambda i, j: (0,) * z.ndim)
    return pl.pallas_call(
        functools.partial(_compress_kernel, norm=norm), grid=(b, g),
        in_specs=[pl.BlockSpec((1, 1, tc, w), lambda i, j: (i, j, 0, 0)),
                  full(pos2), full(w1a), full(w1b), full(b1), full(w2), full(b2), full(gain)],
        out_specs=pl.BlockSpec((1, 1, tc, LANES), lambda i, j: (i, j, 0, 0)),
        out_shape=jax.ShapeDtypeStruct((b, g, tc, LANES), BF16),
        compiler_params=_cparams("parallel", "parallel"), name="nsa_compress",
    )(a, pos2, w1a, w1b, b1, w2, b2, gain)


def _cmp_select_kernel(q_ref, kc_ref, vct_ref, ovt_ref, ocmp_ref, sel_ref, bias_scr, *, top_n, n_tiers):
    tq = q_ref.shape[3]
    tc = kc_ref.shape[2]
    nsb = ovt_ref.shape[0]
    n_tiles = pl.num_programs(1)
    t = pl.program_id(1) * tq + lax.broadcasted_iota(jnp.int32, (1, tq), 1)
    cur = t // SEL_BLOCK
    any_vis = t >= CMP_BLOCK - 1

    def tile_body(rows, blocks):
        n_idx = lax.broadcasted_iota(jnp.int32, (rows, 1), 0)
        bias = jnp.where((n_idx * CMP_STRIDE + (CMP_BLOCK - 1)) <= t, 0.0, NEG)
        j_idx = lax.broadcasted_iota(jnp.int32, (blocks, 1), 0)
        forced = (j_idx == 0) | (j_idx == cur) | (j_idx == cur - 1)
        valid = j_idx <= cur
        j_f = j_idx.astype(F32)
        bias_scr[:rows, :] = bias

        def group_body(g, carry):
            kc = kc_ref[0, g, :rows, :]
            vct = vct_ref[0, g, :, :rows]
            psum = jnp.zeros((rows, tq), F32)
            for r in range(HEADS_PER_GROUP):
                s = _dot_nt(kc, q_ref[0, g, r]) + bias_scr[:rows, :]
                e = jnp.exp2(s - jnp.max(s, axis=0, keepdims=True))
                den = jnp.sum(e, axis=0, keepdims=True)
                p = e * jnp.where(any_vis, 1.0 / den, 0.0)
                o_t = _dot(vct, p.astype(BF16))
                ocmp_ref[0, g, r] = o_t[:HEAD_DIM].T
                psum = psum + p
            hi = psum.astype(BF16)
            lo = (psum - hi.astype(F32)).astype(BF16)
            ovt = ovt_ref[:blocks, :rows]
            psel = _dot(ovt, hi) + _dot(ovt, lo)
            score = jnp.where(forced, FORCE, jnp.where(valid, psel, NEG))
            for _ in range(top_n):
                mx = jnp.max(score, axis=0, keepdims=True)
                first = jnp.min(jnp.where(score == mx, j_f, float(nsb)), axis=0, keepdims=True)
                score = jnp.where(j_f == first, -jnp.inf, score)
            sel_ref[0, g, :blocks, :] = jnp.where((score == -jnp.inf) & valid, 1.0, 0.0)
            if blocks < nsb:
                sel_ref[0, g, blocks:, :] = jnp.zeros((nsb - blocks, tq), F32)
            return carry

        lax.fori_loop(0, N_KV_GROUPS, group_body, 0)

    tier = (pl.program_id(1) * n_tiers) // n_tiles
    for k in range(n_tiers):
        pl.when(tier == k)(functools.partial(tile_body, tc * (k + 1) // n_tiers, nsb * (k + 1) // n_tiers))


def _cmp_select(q, kc, vct, ovt, tq=256):
    b, g, r, t_len, _ = q.shape
    tc = kc.shape[2]
    nsb = ovt.shape[0]
    tq = min(tq, t_len)
    top_n = min(SEL_TOPN, nsb)
    n_tiers = 4
    assert (t_len // tq) % n_tiers == 0 and nsb % (n_tiers * SUBLANES) == 0 and tc % (n_tiers * LANES // 4) == 0
    return pl.pallas_call(
        functools.partial(_cmp_select_kernel, top_n=top_n, n_tiers=n_tiers), grid=(b, t_len // tq),
        in_specs=[pl.BlockSpec((1, g, r, tq, LANES), lambda i, j: (i, 0, 0, j, 0)),
                  pl.BlockSpec((1, g, tc, LANES), lambda i, j: (i, 0, 0, 0)),
                  pl.BlockSpec((1, g, LANES, tc), lambda i, j: (i, 0, 0, 0)),
                  pl.BlockSpec(ovt.shape, lambda i, j: (0, 0))],
        out_specs=[pl.BlockSpec((1, g, r, tq, HEAD_DIM), lambda i, j: (i, 0, 0, j, 0)),
                   pl.BlockSpec((1, g, nsb, tq), lambda i, j: (i, 0, 0, j))],
        out_shape=[jax.ShapeDtypeStruct((b, g, r, t_len, HEAD_DIM), F32),
                   jax.ShapeDtypeStruct((b, g, nsb, t_len), F32)],
        scratch_shapes=[pltpu.VMEM((tc, tq), F32)],
        compiler_params=_cparams("parallel", "parallel"), name="nsa_cmp_select",
    )(q, kc, vct, ovt)


def _sel_win_kernel(q_ref, ka_ref, vat_ref, kw_ref, vwt_ref, selt_ref, ocmp_ref, gate_ref, o_ref,
                    qa_scr, sa_scr, sb_scr, ma_scr, mb_scr, *, kc):
    r_heads, tq = q_ref.shape[2], q_ref.shape[3]
    dh = HEAD_DIM
    cols = r_heads * tq
    t_len = ka_ref.shape[2]
    per_chunk = kc // SEL_BLOCK
    aug_rows = -(-per_chunk // (2 * SUBLANES)) * (2 * SUBLANES)
    t0 = pl.program_id(2) * tq
    t_lane = t0 + lax.broadcasted_iota(jnp.int32, (1, tq), 1)

    for r in range(r_heads):
        qa_scr[:, r * tq:(r + 1) * tq] = q_ref[0, 0, r].astype(F32).T.astype(BF16)

    def softmax_t(s, m_new):
        return jnp.exp2((s - m_new).astype(BF16))

    span = min(tq + WINDOW, t_len)
    w0 = pl.multiple_of(jnp.maximum(t0 + tq - span, 0), tq)
    kpos_w = w0 + lax.broadcasted_iota(jnp.int32, (span, 1), 0)
    in_win = (kpos_w <= t_lane) & (t_lane - kpos_w < WINDOW)
    bias_w = jnp.where(in_win, 0.0, MASK_BIAS)
    s = _dot(kw_ref[0, 0, pl.ds(w0, span), :], qa_scr[...]) + jnp.concatenate([bias_w] * r_heads, axis=1)
    ow = _dot(vwt_ref[0, 0, :, pl.ds(w0, span)], softmax_t(s, jnp.max(s, axis=0, keepdims=True)))
    o_win_t = ow[:dh] / ow[dh:dh + 1]

    n_full = t0 // kc
    last_full = jnp.maximum(n_full - 1, 0)

    def scores(c):
        ci = jnp.minimum(c, n_full)
        sel8 = selt_ref[0, 0, pl.ds(pl.multiple_of(ci * per_chunk, per_chunk), per_chunk), :]
        sb = (jnp.where(c <= n_full, sel8, 0.0) - 1.0) * (-MASK_BIAS)
        if aug_rows > per_chunk:
            sb = jnp.concatenate([sb, jnp.zeros((aug_rows - per_chunk, tq), F32)], axis=0)
        qa_scr[dh:dh + aug_rows, :] = jnp.concatenate([sb.astype(BF16)] * r_heads, axis=1)
        return _dot(ka_ref[0, 0, pl.ds(pl.multiple_of(ci * kc, kc), kc), :], qa_scr[...])

    def update(s, s_max, c, m, acc):
        ci = jnp.minimum(c, n_full)
        m_new = jnp.maximum(m, s_max)
        alpha = jnp.exp2(m - m_new)
        pv = _dot(vat_ref[0, 0, :, pl.ds(pl.multiple_of(ci * kc, kc), kc)], softmax_t(s, m_new))
        return m_new, alpha * acc + pv

    kpos = n_full * kc + lax.broadcasted_iota(jnp.int32, (kc, 1), 0)
    causal = jnp.where(kpos <= t_lane, 0.0, MASK_BIAS)
    s_diag = scores(n_full) + jnp.concatenate([causal] * r_heads, axis=1)
    init = (jnp.full((1, cols), NEG, F32), jnp.zeros((vat_ref.shape[2], cols), F32))
    carry = update(s_diag, jnp.max(s_diag, axis=0, keepdims=True), n_full, *init)

    def stash(c, s_scr, m_scr):
        s = scores(jnp.where(c < n_full, c, n_full + 1))
        s_scr[...] = s
        m_scr[0:1, :] = jnp.max(s, axis=0, keepdims=True)

    stash(0, sa_scr, ma_scr)

    def body(i, carry):
        c = 2 * i
        stash(c + 1, sb_scr, mb_scr)
        carry = update(sa_scr[...], ma_scr[0:1, :], jnp.minimum(c, last_full), *carry)
        stash(c + 2, sa_scr, ma_scr)
        return update(sb_scr[...], mb_scr[0:1, :], jnp.minimum(c + 1, last_full), *carry)

    _, acc = lax.fori_loop(0, (n_full + 1) // 2, body, carry)
    o_sel_t = acc[:dh] / acc[dh:dh + 1]

    gates = gate_ref[0, 0]
    for r in range(r_heads):
        cs = slice(r * tq, (r + 1) * tq)
        o_r = (gates[:, 3 * r:3 * r + 1] * ocmp_ref[0, 0, r]
               + gates[:, 3 * r + 1:3 * r + 2] * o_sel_t[:, cs].T
               + gates[:, 3 * r + 2:3 * r + 3] * o_win_t[:, cs].T)
        o_ref[0, :, r * dh:(r + 1) * dh] = o_r.astype(o_ref.dtype)


def _sel_win(q, ka, vat, kw, vwt, selt, ocmp, gates, tq=256, kc=SEL_CHUNK):
    b, g, r, t_len, _ = q.shape
    nsb = selt.shape[2]
    vrows = vat.shape[2]
    gw = r * HEAD_DIM
    k_spec = pl.BlockSpec((1, 1, t_len, LANES), lambda i, j, k: (i, j, 0, 0))
    vt_spec = pl.BlockSpec((1, 1, vrows, t_len), lambda i, j, k: (i, j, 0, 0))
    return pl.pallas_call(
        functools.partial(_sel_win_kernel, kc=kc), grid=(b, g, t_len // tq),
        in_specs=[pl.BlockSpec((1, 1, r, tq, LANES), lambda i, j, k: (i, j, 0, k, 0)),
                  k_spec, vt_spec, k_spec, vt_spec,
                  pl.BlockSpec((1, 1, nsb, tq), lambda i, j, k: (i, j, 0, k)),
                  pl.BlockSpec((1, 1, r, tq, HEAD_DIM), lambda i, j, k: (i, j, 0, k, 0)),
                  pl.BlockSpec((1, 1, tq, LANES), lambda i, j, k: (i, j, k, 0))],
        out_specs=pl.BlockSpec((1, tq, gw), lambda i, j, k: (i, k, j)),
        out_shape=jax.ShapeDtypeStruct((b, t_len, g * gw), BF16),
        scratch_shapes=[pltpu.VMEM((LANES, r * tq), BF16),
                        pltpu.VMEM((kc, r * tq), F32), pltpu.VMEM((kc, r * tq), F32),
                        pltpu.VMEM((SUBLANES, r * tq), F32), pltpu.VMEM((SUBLANES, r * tq), F32)],
        compiler_params=_cparams("parallel", "parallel", "arbitrary"), name="nsa_sel_win",
    )(q, ka, vat, kw, vwt, selt, ocmp, gates)


def _res_matmul_kernel(a_ref, w_ref, x_ref, o_ref):
    o_ref[...] = x_ref[...] + _dot(a_ref[...], w_ref[...])


def _res_matmul(a, w, x2, tm=512):
    m, k = a.shape
    n = w.shape[1]
    return pl.pallas_call(
        _res_matmul_kernel, grid=(m // tm,),
        in_specs=[pl.BlockSpec((tm, k), lambda i: (i, 0)), pl.BlockSpec((k, n), lambda i: (0, 0)),
                  pl.BlockSpec((tm, n), lambda i: (i, 0))],
        out_specs=pl.BlockSpec((tm, n), lambda i: (i, 0)),
        out_shape=jax.ShapeDtypeStruct((m, n), F32),
        compiler_params=_cparams("parallel"), name="res_matmul",
    )(a, w, x2)


def _ffn_kernel(x_ref, g_ref, wg_ref, wu_ref, wd_ref, o_ref, *, hc):
    x = x_ref[...]
    h = _rms_rows(x, g_ref[...]).astype(BF16)
    acc = x
    for c in range(wg_ref.shape[1] // hc):
        sl = slice(c * hc, (c + 1) * hc)
        a = jax.nn.silu(_dot(h, wg_ref[:, sl])) * _dot(h, wu_ref[:, sl])
        acc = acc + _dot(a.astype(BF16), wd_ref[sl, :])
    o_ref[...] = acc


def _ffn(x2, g, wg, wu, wd, tm=512, hc=MXU_DIM):
    m, d = x2.shape
    full = lambda a: pl.BlockSpec(a.shape, lambda i: (0,) * a.ndim)
    return pl.pallas_call(
        functools.partial(_ffn_kernel, hc=hc), grid=(m // tm,),
        in_specs=[pl.BlockSpec((tm, d), lambda i: (i, 0)), full(g), full(wg), full(wu), full(wd)],
        out_specs=pl.BlockSpec((tm, d), lambda i: (i, 0)),
        out_shape=jax.ShapeDtypeStruct((m, d), F32),
        compiler_params=_cparams("parallel"), name="ffn_swiglu",
    )(x2, g, wg, wu, wd)


def _s5_kernel(x_ref, g_ref, win_ref, pm_ref, pmt_ref, bre_ref, bim_ref, a1_ref, aseg_ref, pw_ref, cre_ref, cim_ref,
               d_ref, wglu_ref, o_ref, xr_scr, xi_scr, cin_scr, car_scr, *, lane_chunk):
    rows, d = x_ref.shape[1], x_ref.shape[2]
    n_state = xr_scr.shape[1]
    n_kt = bre_ref.shape[0]
    kw = bre_ref.shape[1]
    sw = bre_ref.shape[2]
    n_seg = SUBLANES
    seg = rows // n_seg

    @pl.when(pl.program_id(1) == 0)
    def _():
        car_scr[...] = jnp.zeros_like(car_scr)

    x = x_ref[0]
    h = _dot(pm_ref[...], _rms_rows(x, g_ref[...]).astype(BF16)).astype(BF16)
    u = _dot(h, win_ref[...])
    ub = u.astype(BF16)
    for kt in range(n_kt):
        uk = ub[:, kt * kw:(kt + 1) * kw]
        xr_scr[:, kt * sw:(kt + 1) * sw] = _dot(uk, bre_ref[kt])
        xi_scr[:, kt * sw:(kt + 1) * sw] = _dot(uk, bim_ref[kt])

    for lc in range(n_state // lane_chunk):
        ls = slice(lc * lane_chunk, (lc + 1) * lane_chunk)
        ar, ai = a1_ref[0, :, ls], a1_ref[1, :, ls]

        def step(r, carry, ls=ls, ar=ar, ai=ai):
            pr, pi = carry
            r0 = pl.multiple_of(r * SUBLANES, SUBLANES)
            nr = ar * pr - ai * pi + xr_scr[pl.ds(r0, SUBLANES), ls]
            ni = ar * pi + ai * pr + xi_scr[pl.ds(r0, SUBLANES), ls]
            xr_scr[pl.ds(r0, SUBLANES), ls] = nr
            xi_scr[pl.ds(r0, SUBLANES), ls] = ni
            return nr, ni

        zero = jnp.zeros((SUBLANES, lane_chunk), F32)
        lr, li = lax.fori_loop(0, seg, step, (zero, zero), unroll=4)

        sr, si = aseg_ref[0:1, ls], aseg_ref[1:2, ls]
        cr, ci = car_scr[0:1, ls], car_scr[1:2, ls]
        for s in range(n_seg):
            cin_scr[0, s:s + 1, ls] = cr
            cin_scr[1, s:s + 1, ls] = ci
            cr, ci = (sr * cr - si * ci + lr[s:s + 1], sr * ci + si * cr + li[s:s + 1])
        car_scr[0:1, ls] = cr
        car_scr[1:2, ls] = ci

        def fix(r, carry, ls=ls):
            r0 = pl.multiple_of(r * SUBLANES, SUBLANES)
            pr, pi = pw_ref[0, pl.ds(r, 1), ls], pw_ref[1, pl.ds(r, 1), ls]
            cr, ci = cin_scr[0, :, ls], cin_scr[1, :, ls]
            xr_scr[pl.ds(r0, SUBLANES), ls] += pr * cr - pi * ci
            xi_scr[pl.ds(r0, SUBLANES), ls] += pr * ci + pi * cr
            return carry

        lax.fori_loop(0, seg, fix, 0, unroll=4)

    ys = []
    for kt in range(n_kt):
        ss = slice(kt * sw, (kt + 1) * sw)
        ys.append(_dot(xr_scr[:, ss].astype(BF16), cre_ref[kt]) + _dot(xi_scr[:, ss].astype(BF16), cim_ref[kt]))
    y = jnp.concatenate(ys, axis=1) + d_ref[...] * u
    z = _dot(pmt_ref[...], jax.nn.gelu(y).astype(BF16)).astype(BF16)
    vg = _dot(z, wglu_ref[...])
    o_ref[0] = x + vg[:, :d] * jax.nn.sigmoid(vg[:, d:])


def _s5_tables(b_re, b_im, c_re, c_im, log_dt, a_re, a_im, seg):
    ng, ps, cg = b_re.shape
    per_tile = MXU_DIM // cg
    n_kt = ng // per_tile
    dt = jnp.exp(log_dt)[:, None]
    lam_r, lam_i = dt * a_re, dt * a_im
    mag = jnp.exp(lam_r)
    abar_r, abar_i = mag * jnp.cos(lam_i), mag * jnp.sin(lam_i)
    den = a_re * a_re + a_im * a_im
    coef_r = ((abar_r - 1.0) * a_re + abar_i * a_im) / den
    coef_i = (abar_i * a_re - (abar_r - 1.0) * a_im) / den
    bbar_r = coef_r[..., None] * b_re - coef_i[..., None] * b_im
    bbar_i = coef_r[..., None] * b_im + coef_i[..., None] * b_re
    eye = jnp.eye(per_tile, dtype=F32)
    bd_in = lambda m: jnp.einsum('kgpc,gh->kgchp', m.reshape(n_kt, per_tile, ps, cg), eye).reshape(
        n_kt, per_tile * cg, per_tile * ps).astype(BF16)
    bd_out = lambda m: jnp.einsum('kgcp,gh->kgphc', m.reshape(n_kt, per_tile, cg, ps), eye).reshape(
        n_kt, per_tile * ps, per_tile * cg).astype(BF16)

    def power(n):
        pm = jnp.exp(n * lam_r)
        return (pm * jnp.cos(n * lam_i)).reshape(-1), (pm * jnp.sin(n * lam_i)).reshape(-1)

    a1 = jnp.tile(jnp.stack(power(1.0))[:, None, :], (1, SUBLANES, 1))
    aseg = jnp.stack(power(float(seg)))
    steps = [power(float(i + 1)) for i in range(seg)]
    pw = jnp.stack([jnp.stack([p[0] for p in steps]), jnp.stack([p[1] for p in steps])])
    return bd_in(bbar_r), bd_in(bbar_i), a1, aseg, pw, bd_out(c_re), bd_out(-c_im)


def _s5_layer(x, g_norm, w_in, b_re, b_im, c_re, c_im, d_skip, log_dt, a_re, a_im, w_glu, rows=256):
    b, t, d = x.shape
    rows = min(rows, t)
    bre, bim, a1, aseg, pw, cre, cim = _s5_tables(b_re, b_im, c_re, c_im, log_dt, a_re, a_im, rows // SUBLANES)
    n_state = a1.shape[-1]
    full = lambda a: pl.BlockSpec(a.shape, lambda i, j: (0,) * a.ndim)
    seg = rows // SUBLANES
    new_row = jnp.arange(rows)
    pm = (jnp.arange(rows)[None, :] == (new_row % SUBLANES * seg + new_row // SUBLANES)[:, None]).astype(BF16)
    args = (g_norm[None, :], w_in.astype(BF16), pm, pm.T, bre, bim, a1, aseg, pw, cre, cim, d_skip[None, :],
            w_glu.astype(BF16))
    return pl.pallas_call(
        functools.partial(_s5_kernel, lane_chunk=1024), grid=(b, t // rows),
        in_specs=[pl.BlockSpec((1, rows, d), lambda i, j: (i, j, 0))] + [full(a) for a in args],
        out_specs=pl.BlockSpec((1, rows, d), lambda i, j: (i, j, 0)),
        out_shape=jax.ShapeDtypeStruct((b, t, d), F32),
        scratch_shapes=[pltpu.VMEM((rows, n_state), F32), pltpu.VMEM((rows, n_state), F32),
                        pltpu.VMEM((2, SUBLANES, n_state), F32), pltpu.VMEM((SUBLANES, n_state), F32)],
        compiler_params=_cparams("parallel", "arbitrary"), name="s5_mixer",
    )(x, *args)


def _pad_heads(w, n_heads, width):
    d = w.shape[0]
    return jnp.pad(w.reshape(d, n_heads, width), ((0, 0), (0, 0), (0, LANES - width))).reshape(d, n_heads * LANES)


def _nsa_layer(x, g_norm, w_in, w_out, q_gain, k_gain, cmp_pos, cmp_w1, cmp_b1, cmp_w2, cmp_b2):
    b, t, d = x.shape
    gn, rh, dh = N_KV_GROUPS, HEADS_PER_GROUP, HEAD_DIM
    assert t % SEL_CHUNK == 0 and t >= SEL_CHUNK + WINDOW
    x2 = x.reshape(b * t, d)
    scale = LOG2E / math.sqrt(dh)

    kvw = lambda i: w_in[:, d + i * KV_DIM:d + (i + 1) * KV_DIM]
    wq = _pad_heads(w_in[:, :d], N_HEADS, dh).astype(BF16)
    wk = jnp.concatenate([_pad_heads(kvw(i), gn, dh) for i in (2, 3, 4, 5)], axis=1).astype(BF16)
    wc = jnp.concatenate([kvw(0), kvw(1)], axis=1).astype(BF16)
    wg = _pad_heads(w_in[:, d + 6 * KV_DIM:], gn, rh * N_BRANCHES).astype(BF16)
    pad_gain = lambda v, n: jnp.tile(jnp.pad(v, (0, LANES - dh)), n)[None, :]
    qg_p = pad_gain(q_gain * scale, N_HEADS)
    kg_p = jnp.concatenate([pad_gain(k_gain[1], gn), pad_gain(k_gain[2], gn)], axis=0)
    seg = jnp.arange(2 * LANES) // LANES
    ebd = (seg[:, None] == seg[None, :]).astype(BF16)

    q, ka, va, kw, vw, kc_raw, vc_raw, gates = _nsa_proj(x2, b, t, g_norm[None, :], wq, wk, wc, wg, qg_p, kg_p, ebd)

    tc = t // CMP_STRIDE
    half = CMP_STRIDE * dh
    blk = lambda a: a.reshape(b, tc, CMP_STRIDE, gn, dh).transpose(0, 3, 1, 2, 4).reshape(b, gn, tc, half)
    comp = []
    for i, (raw, norm) in enumerate(((kc_raw, True), (vc_raw, False))):
        w1 = cmp_w1[i].astype(BF16)
        comp.append(_compress(blk(raw), cmp_pos[i].reshape(2, half), w1[:half], w1[half:], cmp_b1[i][None, :],
                              jnp.pad(cmp_w2[i], ((0, 0), (0, LANES - dh))).astype(BF16),
                              jnp.pad(cmp_b2[i], (0, LANES - dh))[None, :],
                              jnp.pad(k_gain[0], (0, LANES - dh))[None, :], norm))
    kc, vc = comp
    vct = vc.transpose(0, 1, 3, 2)

    nsb = t // SEL_BLOCK
    n_i = jnp.arange(tc)
    j_i = jnp.arange(nsb)
    n_cmp = (t - CMP_BLOCK) // CMP_STRIDE + 1
    ovt = ((n_i[None, :] * CMP_STRIDE < (j_i[:, None] + 1) * SEL_BLOCK)
           & (n_i[None, :] * CMP_STRIDE + CMP_BLOCK > j_i[:, None] * SEL_BLOCK)
           & (n_i[None, :] < n_cmp)).astype(BF16)
    ocmp, selt = _cmp_select(q, kc, vct, ovt)

    vrows = dh + 2 * SUBLANES
    vat = va[..., :vrows].transpose(0, 1, 3, 2)
    vwt = vw[..., :vrows].transpose(0, 1, 3, 2)
    o = _sel_win(q, ka, vat, kw, vwt, selt, ocmp, gates)
    return _res_matmul(o.reshape(b * t, d), w_out.astype(BF16), x2).reshape(b, t, d)


def kernel(x, mix_norm, ffn_norm, nsa_w_in, nsa_w_out, nsa_q_gain, nsa_k_gain, nsa_cmp_pos, nsa_cmp_w1, nsa_cmp_b1, nsa_cmp_w2, nsa_cmp_b2, s5_w_in, s5_b_re, s5_b_im, s5_c_re, s5_c_im, s5_d, s5_log_dt, s5_a_re, s5_a_im, s5_w_glu, ffn_w_gate, ffn_w_up, ffn_w_down):
    b, t, d = x.shape
    depth = mix_norm.shape[0]
    for layer in range(depth):
        i = layer // 2
        if layer % 2 == 0:
            x = _nsa_layer(x, mix_norm[layer], nsa_w_in[i], nsa_w_out[i], nsa_q_gain[i], nsa_k_gain[i],
                           nsa_cmp_pos[i], nsa_cmp_w1[i], nsa_cmp_b1[i], nsa_cmp_w2[i], nsa_cmp_b2[i])
        else:
            x = _s5_layer(x, mix_norm[layer], s5_w_in[i], s5_b_re[i], s5_b_im[i], s5_c_re[i], s5_c_im[i],
                          s5_d[i], s5_log_dt[i], s5_a_re[i], s5_a_im[i], s5_w_glu[i])
        x = _ffn(x.reshape(b * t, d), ffn_norm[layer][None, :], ffn_w_gate[layer].astype(BF16),
                 ffn_w_up[layer].astype(BF16), ffn_w_down[layer].astype(BF16)).reshape(b, t, d)
    return x
```

```python
import functools
import math

import jax
import jax.numpy as jnp
from jax import lax
from jax.experimental import pallas as pl
from jax.experimental.pallas import tpu as pltpu

F32 = jnp.float32
BF16 = jnp.bfloat16

EPS = 1e-6
NEG = -1e30
FORCE = 1e4
MASK_BIAS = -1e9
LOG2E = 1.4426950408889634

N_HEADS = 16
HEAD_DIM = 64
N_KV_GROUPS = 4
HEADS_PER_GROUP = 4
KV_DIM = N_KV_GROUPS * HEAD_DIM
CMP_BLOCK = 32
CMP_STRIDE = 16
SEL_BLOCK = 64
SEL_TOPN = 16
N_FORCED = 3
WINDOW = 512
N_BRANCHES = 3
S5_GROUP = 16
S5_STATE = 64

LANES = 128
SUBLANES = 8
MXU_DIM = 256
SEL_CHUNK = 512
V7X_VMEM_LIMIT_BYTES = 56 * 1024 * 1024


def _cparams(*semantics):
    return pltpu.CompilerParams(dimension_semantics=semantics,
                                vmem_limit_bytes=V7X_VMEM_LIMIT_BYTES)


def _rms_rows(x, g):
    var = jnp.mean(x * x, axis=-1, keepdims=True)
    return x * lax.rsqrt(var + EPS) * g


def _dot(a, b):
    return jnp.dot(a, b, preferred_element_type=F32)


def _dot_nt(a, b):
    return lax.dot_general(a, b, (((1,), (1,)), ((), ())), preferred_element_type=F32)


def _seg_sumsq(v, ebd):
    sq = v * v
    hi = sq.astype(BF16)
    lo = (sq - hi.astype(F32)).astype(BF16)
    return _dot(hi, ebd) + _dot(lo, ebd)


def _nsa_proj_kernel(x_ref, g_ref, wq_ref, wk_ref, wc_ref, wg_ref, qg_ref, kg_ref, ebd_ref,
                     q_ref, ka_ref, va_ref, kw_ref, vw_ref, kcr_ref, vcr_ref, gate_ref):
    tm = x_ref.shape[0]
    h = _rms_rows(x_ref[...], g_ref[...]).astype(BF16)
    ebd = ebd_ref[...]
    inv_dh = 1.0 / HEAD_DIM
    slab = 2 * LANES

    q = _dot(h, wq_ref[...])
    qg = qg_ref[...]
    for pair in range(N_HEADS // 2):
        sl = slice(pair * slab, (pair + 1) * slab)
        qi = q[:, sl]
        qn = (qi * lax.rsqrt(_seg_sumsq(qi, ebd) * inv_dh + EPS) * qg[:, sl]).astype(BF16)
        for hh in range(2):
            g, r = divmod(2 * pair + hh, HEADS_PER_GROUP)
            q_ref[0, g, r] = qn[:, hh * LANES:(hh + 1) * LANES]

    row = lax.rem(pl.program_id(0) * tm, SEL_CHUNK) + lax.broadcasted_iota(jnp.int32, (tm, 1), 0)
    lane = lax.broadcasted_iota(jnp.int32, (1, LANES), 1)
    blk_flag = ((row // SEL_BLOCK) == (lane - HEAD_DIM)).astype(F32)
    one_flag = (lane == HEAD_DIM).astype(F32)

    kv = _dot(h, wk_ref[...])
    kg = kg_ref[...]
    plan = ((ka_ref, 0, blk_flag), (va_ref, None, one_flag), (kw_ref, 1, None), (vw_ref, None, one_flag))
    for part, (ref, gain_row, flag) in enumerate(plan):
        for half in range(2):
            base = part * N_KV_GROUPS * LANES + half * slab
            v = kv[:, base:base + slab]
            if gain_row is not None:
                v = v * lax.rsqrt(_seg_sumsq(v, ebd) * inv_dh + EPS) * kg[gain_row:gain_row + 1, half * slab:(half + 1) * slab]
            for hh in range(2):
                piece = v[:, hh * LANES:(hh + 1) * LANES]
                if flag is not None:
                    piece = piece + flag
                ref[0, 2 * half + hh] = piece.astype(BF16)

    c = _dot(h, wc_ref[...])
    kcr_ref[...] = c[:, :KV_DIM]
    vcr_ref[...] = c[:, KV_DIM:]
    gt = jax.nn.sigmoid(_dot(h, wg_ref[...]))
    for g in range(N_KV_GROUPS):
        gate_ref[0, g] = gt[:, g * LANES:(g + 1) * LANES]


def _nsa_proj(x2, b, t, g, wq, wk, wc, wg, qg_p, kg_p, ebd, tm=512):
    m, d = x2.shape
    assert SEL_CHUNK % tm == 0 and t % SEL_CHUNK == 0
    nt = t // tm
    gn, rh = N_KV_GROUPS, HEADS_PER_GROUP
    full = lambda a: pl.BlockSpec(a.shape, lambda i: (0,) * a.ndim)
    row = lambda n: pl.BlockSpec((tm, n), lambda i: (i, 0))
    head = pl.BlockSpec((1, gn, tm, LANES), lambda i: (i // nt, 0, i % nt, 0))
    outs = [jax.ShapeDtypeStruct((b, gn, rh, t, LANES), BF16)]
    outs += [jax.ShapeDtypeStruct((b, gn, t, LANES), BF16)] * 4
    outs += [jax.ShapeDtypeStruct((m, KV_DIM), F32)] * 2
    outs += [jax.ShapeDtypeStruct((b, gn, t, LANES), F32)]
    return pl.pallas_call(
        _nsa_proj_kernel, grid=(m // tm,),
        in_specs=[row(d), full(g), full(wq), full(wk), full(wc), full(wg), full(qg_p), full(kg_p), full(ebd)],
        out_specs=[pl.BlockSpec((1, gn, rh, tm, LANES), lambda i: (i // nt, 0, 0, i % nt, 0)),
                   head, head, head, head, row(KV_DIM), row(KV_DIM), head],
        out_shape=outs, compiler_params=_cparams("parallel"), name="nsa_proj",
    )(x2, g, wq, wk, wc, wg, qg_p, kg_p, ebd)


def _compress_kernel(a_ref, pos_ref, w1a_ref, w1b_ref, b1_ref, w2_ref, b2_ref, gain_ref, o_ref, *, norm):
    a = a_ref[0, 0]
    tc = a.shape[0]
    pos = pos_ref[...]
    h1 = _dot((a + pos[0:1]).astype(BF16), w1a_ref[...])
    h2 = _dot((a + pos[1:2]).astype(BF16), w1b_ref[...])
    h = h1 + pltpu.roll(h2, tc - 1, 0) + b1_ref[...]
    hid = jax.nn.gelu(h).astype(BF16)
    out = _dot(hid, w2_ref[...]) + b2_ref[...]
    if norm:
        var = jnp.sum(out * out, axis=-1, keepdims=True) * (1.0 / HEAD_DIM)
        out = out * lax.rsqrt(var + EPS) * gain_ref[...]
    o_ref[0, 0] = out.astype(o_ref.dtype)


def _compress(a, pos2, w1a, w1b, b1, w2, b2, gain, norm):
    b, g, tc, w = a.shape
    full = lambda z: pl.BlockSpec(z.shape, lambda i, j: (0,) * z.ndim)
    return pl.pallas_call(
        functools.partial(_compress_kernel, norm=norm), grid=(b, g),
        in_specs=[pl.BlockSpec((1, 1, tc, w), lambda i, j: (i, j, 0, 0)),
                  full(pos2), full(w1a), full(w1b), full(b1), full(w2), full(b2), full(gain)],
        out_specs=pl.BlockSpec((1, 1, tc, LANES), lambda i, j: (i, j, 0, 0)),
        out_shape=jax.ShapeDtypeStruct((b, g, tc, LANES), BF16),
        compiler_params=_cparams("parallel", "parallel"), name="nsa_compress",
    )(a, pos2, w1a, w1b, b1, w2, b2, gain)


def _cmp_select_kernel(q_ref, kc_ref, vct_ref, ovt_ref, ocmp_ref, sel_ref, bias_scr, *, top_n, n_tiers):
    tq = q_ref.shape[3]
    tc = kc_ref.shape[2]
    nsb = ovt_ref.shape[0]
    n_tiles = pl.num_programs(1)
    t = pl.program_id(1) * tq + lax.broadcasted_iota(jnp.int32, (1, tq), 1)
    cur = t // SEL_BLOCK
    any_vis = t >= CMP_BLOCK - 1
    n_rest = max(top_n - N_FORCED, 0)

    def tile_body(rows, blocks):
        n_idx = lax.broadcasted_iota(jnp.int32, (rows, 1), 0)
        bias = jnp.where((n_idx * CMP_STRIDE + (CMP_BLOCK - 1)) <= t, 0.0, NEG)
        j_idx = lax.broadcasted_iota(jnp.int32, (blocks, 1), 0)
        forced = (j_idx == 0) | (j_idx == cur) | (j_idx == cur - 1)
        valid = j_idx <= cur
        j_f = j_idx.astype(F32)
        bias_scr[:rows, :] = bias

        def group_body(g, carry):
            kc = kc_ref[0, g, :rows, :]
            vct = vct_ref[0, g, :, :rows]
            psum = jnp.zeros((rows, tq), F32)
            for r in range(HEADS_PER_GROUP):
                s = _dot_nt(kc, q_ref[0, g, r]) + bias_scr[:rows, :]
                e = jnp.exp2(s - jnp.max(s, axis=0, keepdims=True))
                den = jnp.sum(e, axis=0, keepdims=True)
                p = e * jnp.where(any_vis, 1.0 / den, 0.0)
                o_t = _dot(vct, p.astype(BF16))
                ocmp_ref[0, g, r] = o_t[:HEAD_DIM]
                psum = psum + p
            hi = psum.astype(BF16)
            lo = (psum - hi.astype(F32)).astype(BF16)
            ovt = ovt_ref[:blocks, :rows]
            psel = _dot(ovt, hi) + _dot(ovt, lo)
            score = jnp.where(valid & ~forced, psel, NEG)
            for _ in range(n_rest):
                mx = jnp.max(score, axis=0, keepdims=True)
                first = jnp.min(jnp.where(score == mx, j_f, float(nsb)), axis=0, keepdims=True)
                score = jnp.where(j_f == first, -jnp.inf, score)
            sel_ref[0, g, :blocks, :] = jnp.where(forced | ((score == -jnp.inf) & valid), 1.0, 0.0)
            if blocks < nsb:
                sel_ref[0, g, blocks:, :] = jnp.zeros((nsb - blocks, tq), F32)
            return carry

        lax.fori_loop(0, N_KV_GROUPS, group_body, 0)

    tier = (pl.program_id(1) * n_tiers) // n_tiles
    for k in range(n_tiers):
        pl.when(tier == k)(functools.partial(tile_body, tc * (k + 1) // n_tiers, nsb * (k + 1) // n_tiers))


def _cmp_select(q, kc, vct, ovt, tq=256):
    b, g, r, t_len, _ = q.shape
    tc = kc.shape[2]
    nsb = ovt.shape[0]
    tq = min(tq, t_len)
    top_n = min(SEL_TOPN, nsb)
    assert top_n >= N_FORCED
    n_tiers = 4
    assert (t_len // tq) % n_tiers == 0 and nsb % (n_tiers * SUBLANES) == 0 and tc % (n_tiers * LANES // 4) == 0
    return pl.pallas_call(
        functools.partial(_cmp_select_kernel, top_n=top_n, n_tiers=n_tiers), grid=(b, t_len // tq),
        in_specs=[pl.BlockSpec((1, g, r, tq, LANES), lambda i, j: (i, 0, 0, j, 0)),
                  pl.BlockSpec((1, g, tc, LANES), lambda i, j: (i, 0, 0, 0)),
                  pl.BlockSpec((1, g, LANES, tc), lambda i, j: (i, 0, 0, 0)),
                  pl.BlockSpec(ovt.shape, lambda i, j: (0, 0))],
        out_specs=[pl.BlockSpec((1, g, r, HEAD_DIM, tq), lambda i, j: (i, 0, 0, 0, j)),
                   pl.BlockSpec((1, g, nsb, tq), lambda i, j: (i, 0, 0, j))],
        out_shape=[jax.ShapeDtypeStruct((b, g, r, HEAD_DIM, t_len), F32),
                   jax.ShapeDtypeStruct((b, g, nsb, t_len), F32)],
        scratch_shapes=[pltpu.VMEM((tc, tq), F32)],
        compiler_params=_cparams("parallel", "parallel"), name="nsa_cmp_select",
    )(q, kc, vct, ovt)


def _sel_win_kernel(q_ref, ka_ref, vat_ref, kw_ref, vwt_ref, selt_ref, ocmp_ref, gate_ref, o_ref,
                    qa_scr, sa_scr, sb_scr, ma_scr, mb_scr, *, kc):
    r_heads, tq = q_ref.shape[2], q_ref.shape[3]
    dh = HEAD_DIM
    cols = r_heads * tq
    t_len = ka_ref.shape[2]
    per_chunk = kc // SEL_BLOCK
    aug_rows = -(-per_chunk // (2 * SUBLANES)) * (2 * SUBLANES)
    t0 = pl.program_id(2) * tq
    t_lane = t0 + lax.broadcasted_iota(jnp.int32, (1, tq), 1)

    for r in range(r_heads):
        qa_scr[:, r * tq:(r + 1) * tq] = q_ref[0, 0, r].astype(F32).T.astype(BF16)

    def softmax_t(s, m_new):
        return jnp.exp2((s - m_new).astype(BF16))

    span = min(tq + WINDOW, t_len)
    w0 = pl.multiple_of(jnp.maximum(t0 + tq - span, 0), tq)
    kpos_w = w0 + lax.broadcasted_iota(jnp.int32, (span, 1), 0)
    in_win = (kpos_w <= t_lane) & (t_lane - kpos_w < WINDOW)
    bias_w = jnp.where(in_win, 0.0, MASK_BIAS)
    s = _dot(kw_ref[0, 0, pl.ds(w0, span), :], qa_scr[...]) + jnp.concatenate([bias_w] * r_heads, axis=1)
    ow = _dot(vwt_ref[0, 0, :, pl.ds(w0, span)], softmax_t(s, jnp.max(s, axis=0, keepdims=True)))
    o_win_t = ow[:dh] / ow[dh:dh + 1]

    n_full = t0 // kc
    last_full = jnp.maximum(n_full - 1, 0)

    def scores(c):
        ci = jnp.minimum(c, n_full)
        sel8 = selt_ref[0, 0, pl.ds(pl.multiple_of(ci * per_chunk, per_chunk), per_chunk), :]
        sb = (jnp.where(c <= n_full, sel8, 0.0) - 1.0) * (-MASK_BIAS)
        if aug_rows > per_chunk:
            sb = jnp.concatenate([sb, jnp.zeros((aug_rows - per_chunk, tq), F32)], axis=0)
        qa_scr[dh:dh + aug_rows, :] = jnp.concatenate([sb.astype(BF16)] * r_heads, axis=1)
        return _dot(ka_ref[0, 0, pl.ds(pl.multiple_of(ci * kc, kc), kc), :], qa_scr[...])

    def update(s, s_max, c, m, acc):
        ci = jnp.minimum(c, n_full)
        m_new = jnp.maximum(m, s_max)
        alpha = jnp.exp2(m - m_new)
        pv = _dot(vat_ref[0, 0, :, pl.ds(pl.multiple_of(ci * kc, kc), kc)], softmax_t(s, m_new))
        return m_new, alpha * acc + pv

    kpos = n_full * kc + lax.broadcasted_iota(jnp.int32, (kc, 1), 0)
    causal = jnp.where(kpos <= t_lane, 0.0, MASK_BIAS)
    s_diag = scores(n_full) + jnp.concatenate([causal] * r_heads, axis=1)
    init = (jnp.full((1, cols), NEG, F32), jnp.zeros((vat_ref.shape[2], cols), F32))
    carry = update(s_diag, jnp.max(s_diag, axis=0, keepdims=True), n_full, *init)

    def stash(c, s_scr, m_scr):
        c = jnp.where(c < n_full, c, n_full + 1)
        ci = jnp.minimum(c, n_full)
        sel8 = selt_ref[0, 0, pl.ds(pl.multiple_of(ci * per_chunk, per_chunk), per_chunk), :]
        sb = (jnp.where(c <= n_full, sel8, 0.0) - 1.0) * (-MASK_BIAS)
        if aug_rows > per_chunk:
            sb = jnp.concatenate([sb, jnp.zeros((aug_rows - per_chunk, tq), F32)], axis=0)
        qa_scr[dh:dh + aug_rows, :] = jnp.concatenate([sb.astype(BF16)] * r_heads, axis=1)
        k_rows = ka_ref[0, 0, pl.ds(pl.multiple_of(ci * kc, kc), kc), :]
        for n in range(r_heads):
            cs = slice(n * tq, (n + 1) * tq)
            s = _dot(k_rows, qa_scr[:, cs])
            s_scr[:, cs] = s
            m_scr[0:1, cs] = jnp.max(s, axis=0, keepdims=True)

    def update_scr(s_scr, m_scr, c, m, acc):
        ci = jnp.minimum(c, n_full)
        v_cols = vat_ref[0, 0, :, pl.ds(pl.multiple_of(ci * kc, kc), kc)]
        m_new = jnp.maximum(m, m_scr[0:1, :])
        alpha = jnp.exp2(m - m_new)
        pvs = []
        for n in range(r_heads):
            cs = slice(n * tq, (n + 1) * tq)
            pvs.append(_dot(v_cols, softmax_t(s_scr[:, cs], m_new[:, cs])))
        return m_new, alpha * acc + jnp.concatenate(pvs, axis=1)

    stash(0, sa_scr, ma_scr)

    def body(i, carry):
        c = 2 * i
        stash(c + 1, sb_scr, mb_scr)
        carry = update_scr(sa_scr, ma_scr, jnp.minimum(c, last_full), *carry)
        stash(c + 2, sa_scr, ma_scr)
        return update_scr(sb_scr, mb_scr, jnp.minimum(c + 1, last_full), *carry)

    _, acc = lax.fori_loop(0, (n_full + 1) // 2, body, carry)
    o_sel_t = acc[:dh] / acc[dh:dh + 1]

    gates_t = gate_ref[0, 0].T
    parts = []
    for r in range(r_heads):
        cs = slice(r * tq, (r + 1) * tq)
        parts.append(gates_t[3 * r:3 * r + 1] * ocmp_ref[0, 0, r]
                     + gates_t[3 * r + 1:3 * r + 2] * o_sel_t[:, cs]
                     + gates_t[3 * r + 2:3 * r + 3] * o_win_t[:, cs])
    o_ref[0] = jnp.concatenate(parts, axis=0).T.astype(o_ref.dtype)


def _sel_win(q, ka, vat, kw, vwt, selt, ocmp, gates, tq=256, kc=SEL_CHUNK):
    b, g, r, t_len, _ = q.shape
    nsb = selt.shape[2]
    vrows = vat.shape[2]
    gw = r * HEAD_DIM
    k_spec = pl.BlockSpec((1, 1, t_len, LANES), lambda i, j, k: (i, j, 0, 0))
    vt_spec = pl.BlockSpec((1, 1, vrows, t_len), lambda i, j, k: (i, j, 0, 0))
    return pl.pallas_call(
        functools.partial(_sel_win_kernel, kc=kc), grid=(b, g, t_len // tq),
        in_specs=[pl.BlockSpec((1, 1, r, tq, LANES), lambda i, j, k: (i, j, 0, k, 0)),
                  k_spec, vt_spec, k_spec, vt_spec,
                  pl.BlockSpec((1, 1, nsb, tq), lambda i, j, k: (i, j, 0, k)),
                  pl.BlockSpec((1, 1, r, HEAD_DIM, tq), lambda i, j, k: (i, j, 0, 0, k)),
                  pl.BlockSpec((1, 1, tq, LANES), lambda i, j, k: (i, j, k, 0))],
        out_specs=pl.BlockSpec((1, tq, gw), lambda i, j, k: (i, k, j)),
        out_shape=jax.ShapeDtypeStruct((b, t_len, g * gw), BF16),
        scratch_shapes=[pltpu.VMEM((LANES, r * tq), BF16),
                        pltpu.VMEM((kc, r * tq), F32), pltpu.VMEM((kc, r * tq), F32),
                        pltpu.VMEM((SUBLANES, r * tq), F32), pltpu.VMEM((SUBLANES, r * tq), F32)],
        compiler_params=_cparams("parallel", "parallel", "arbitrary"), name="nsa_sel_win",
    )(q, ka, vat, kw, vwt, selt, ocmp, gates)


def _res_matmul_kernel(a_ref, w_ref, x_ref, o_ref):
    o_ref[...] = x_ref[...] + _dot(a_ref[...], w_ref[...])


def _res_matmul(a, w, x2, tm=512):
    m, k = a.shape
    n = w.shape[1]
    return pl.pallas_call(
        _res_matmul_kernel, grid=(m // tm,),
        in_specs=[pl.BlockSpec((tm, k), lambda i: (i, 0)), pl.BlockSpec((k, n), lambda i: (0, 0)),
                  pl.BlockSpec((tm, n), lambda i: (i, 0))],
        out_specs=pl.BlockSpec((tm, n), lambda i: (i, 0)),
        out_shape=jax.ShapeDtypeStruct((m, n), F32),
        compiler_params=_cparams("parallel"), name="res_matmul",
    )(a, w, x2)


def _ffn_kernel(x_ref, g_ref, wg_ref, wu_ref, wd_ref, o_ref, *, hc):
    x = x_ref[...]
    h = _rms_rows(x, g_ref[...]).astype(BF16)
    acc = x
    for c in range(wg_ref.shape[1] // hc):
        sl = slice(c * hc, (c + 1) * hc)
        a = jax.nn.silu(_dot(h, wg_ref[:, sl])) * _dot(h, wu_ref[:, sl])
        acc = acc + _dot(a.astype(BF16), wd_ref[sl, :])
    o_ref[...] = acc


def _ffn(x2, g, wg, wu, wd, tm=512, hc=MXU_DIM):
    m, d = x2.shape
    full = lambda a: pl.BlockSpec(a.shape, lambda i: (0,) * a.ndim)
    return pl.pallas_call(
        functools.partial(_ffn_kernel, hc=hc), grid=(m // tm,),
        in_specs=[pl.BlockSpec((tm, d), lambda i: (i, 0)), full(g), full(wg), full(wu), full(wd)],
        out_specs=pl.BlockSpec((tm, d), lambda i: (i, 0)),
        out_shape=jax.ShapeDtypeStruct((m, d), F32),
        compiler_params=_cparams("parallel"), name="ffn_swiglu",
    )(x2, g, wg, wu, wd)


def _s5_kernel(x_ref, g_ref, win_ref, pm_ref, pmt_ref, bre_ref, bim_ref, a1_ref, aseg_ref, pw_ref, cre_ref, cim_ref,
               d_ref, wglu_ref, o_ref, xr_scr, xi_scr, cin_scr, car_scr, *, lane_chunk):
    rows, d = x_ref.shape[1], x_ref.shape[2]
    n_state = xr_scr.shape[1]
    n_kt = bre_ref.shape[0]
    kw = bre_ref.shape[1]
    sw = bre_ref.shape[2]
    n_seg = SUBLANES
    seg = rows // n_seg

    @pl.when(pl.program_id(1) == 0)
    def _():
        car_scr[...] = jnp.zeros_like(car_scr)

    x = x_ref[0]
    h = _dot(pm_ref[...], _rms_rows(x, g_ref[...]).astype(BF16)).astype(BF16)
    u = _dot(h, win_ref[...])
    ub = u.astype(BF16)
    for kt in range(n_kt):
        uk = ub[:, kt * kw:(kt + 1) * kw]
        xr_scr[:, kt * sw:(kt + 1) * sw] = _dot(uk, bre_ref[kt])
        xi_scr[:, kt * sw:(kt + 1) * sw] = _dot(uk, bim_ref[kt])

    for lc in range(n_state // lane_chunk):
        ls = slice(lc * lane_chunk, (lc + 1) * lane_chunk)
        ar, ai = a1_ref[0, :, ls], a1_ref[1, :, ls]

        def step(r, carry, ls=ls, ar=ar, ai=ai):
            pr, pi = carry
            r0 = pl.multiple_of(r * SUBLANES, SUBLANES)
            nr = ar * pr - ai * pi + xr_scr[pl.ds(r0, SUBLANES), ls]
            ni = ar * pi + ai * pr + xi_scr[pl.ds(r0, SUBLANES), ls]
            xr_scr[pl.ds(r0, SUBLANES), ls] = nr
            xi_scr[pl.ds(r0, SUBLANES), ls] = ni
            return nr, ni

        zero = jnp.zeros((SUBLANES, lane_chunk), F32)
        lr, li = lax.fori_loop(0, seg, step, (zero, zero), unroll=4)

        sr, si = aseg_ref[0:1, ls], aseg_ref[1:2, ls]
        cr, ci = car_scr[0:1, ls], car_scr[1:2, ls]
        for s in range(n_seg):
            cin_scr[0, s:s + 1, ls] = cr
            cin_scr[1, s:s + 1, ls] = ci
            cr, ci = (sr * cr - si * ci + lr[s:s + 1], sr * ci + si * cr + li[s:s + 1])
        car_scr[0:1, ls] = cr
        car_scr[1:2, ls] = ci

        def fix(r, carry, ls=ls):
            r0 = pl.multiple_of(r * SUBLANES, SUBLANES)
            pr, pi = pw_ref[0, pl.ds(r, 1), ls], pw_ref[1, pl.ds(r, 1), ls]
            cr, ci = cin_scr[0, :, ls], cin_scr[1, :, ls]
            xr_scr[pl.ds(r0, SUBLANES), ls] += pr * cr - pi * ci
            xi_scr[pl.ds(r0, SUBLANES), ls] += pr * ci + pi * cr
            return carry

        lax.fori_loop(0, seg, fix, 0, unroll=4)

    ys = []
    for kt in range(n_kt):
        ss = slice(kt * sw, (kt + 1) * sw)
        ys.append(_dot(xr_scr[:, ss].astype(BF16), cre_ref[kt]) + _dot(xi_scr[:, ss].astype(BF16), cim_ref[kt]))
    y = jnp.concatenate(ys, axis=1) + d_ref[...] * u
    z = _dot(pmt_ref[...], jax.nn.gelu(y).astype(BF16)).astype(BF16)
    vg = _dot(z, wglu_ref[...])
    o_ref[0] = x + vg[:, :d] * jax.nn.sigmoid(vg[:, d:])


def _s5_tables(b_re, b_im, c_re, c_im, log_dt, a_re, a_im, seg):
    ng, ps, cg = b_re.shape
    per_tile = MXU_DIM // cg
    n_kt = ng // per_tile
    dt = jnp.exp(log_dt)[:, None]
    lam_r, lam_i = dt * a_re, dt * a_im
    mag = jnp.exp(lam_r)
    abar_r, abar_i = mag * jnp.cos(lam_i), mag * jnp.sin(lam_i)
    den = a_re * a_re + a_im * a_im
    coef_r = ((abar_r - 1.0) * a_re + abar_i * a_im) / den
    coef_i = (abar_i * a_re - (abar_r - 1.0) * a_im) / den
    bbar_r = coef_r[..., None] * b_re - coef_i[..., None] * b_im
    bbar_i = coef_r[..., None] * b_im + coef_i[..., None] * b_re
    eye = jnp.eye(per_tile, dtype=F32)
    bd_in = lambda m: jnp.einsum('kgpc,gh->kgchp', m.reshape(n_kt, per_tile, ps, cg), eye).reshape(
        n_kt, per_tile * cg, per_tile * ps).astype(BF16)
    bd_out = lambda m: jnp.einsum('kgcp,gh->kgphc', m.reshape(n_kt, per_tile, cg, ps), eye).reshape(
        n_kt, per_tile * ps, per_tile * cg).astype(BF16)

    def power(n):
        pm = jnp.exp(n * lam_r)
        return (pm * jnp.cos(n * lam_i)).reshape(-1), (pm * jnp.sin(n * lam_i)).reshape(-1)

    a1 = jnp.tile(jnp.stack(power(1.0))[:, None, :], (1, SUBLANES, 1))
    aseg = jnp.stack(power(float(seg)))
    steps = [power(float(i + 1)) for i in range(seg)]
    pw = jnp.stack([jnp.stack([p[0] for p in steps]), jnp.stack([p[1] for p in steps])])
    return bd_in(bbar_r), bd_in(bbar_i), a1, aseg, pw, bd_out(c_re), bd_out(-c_im)


def _s5_layer(x, g_norm, w_in, b_re, b_im, c_re, c_im, d_skip, log_dt, a_re, a_im, w_glu, rows=256):
    b, t, d = x.shape
    rows = min(rows, t)
    bre, bim, a1, aseg, pw, cre, cim = _s5_tables(b_re, b_im, c_re, c_im, log_dt, a_re, a_im, rows // SUBLANES)
    n_state = a1.shape[-1]
    full = lambda a: pl.BlockSpec(a.shape, lambda i, j: (0,) * a.ndim)
    seg = rows // SUBLANES
    new_row = jnp.arange(rows)
    pm = (jnp.arange(rows)[None, :] == (new_row % SUBLANES * seg + new_row // SUBLANES)[:, None]).astype(BF16)
    args = (g_norm[None, :], w_in.astype(BF16), pm, pm.T, bre, bim, a1, aseg, pw, cre, cim, d_skip[None, :],
            w_glu.astype(BF16))
    return pl.pallas_call(
        functools.partial(_s5_kernel, lane_chunk=1024), grid=(b, t // rows),
        in_specs=[pl.BlockSpec((1, rows, d), lambda i, j: (i, j, 0))] + [full(a) for a in args],
        out_specs=pl.BlockSpec((1, rows, d), lambda i, j: (i, j, 0)),
        out_shape=jax.ShapeDtypeStruct((b, t, d), F32),
        scratch_shapes=[pltpu.VMEM((rows, n_state), F32), pltpu.VMEM((rows, n_state), F32),
                        pltpu.VMEM((2, SUBLANES, n_state), F32), pltpu.VMEM((SUBLANES, n_state), F32)],
        compiler_params=_cparams("parallel", "arbitrary"), name="s5_mixer",
    )(x, *args)


def _pad_heads(w, n_heads, width):
    d = w.shape[0]
    return jnp.pad(w.reshape(d, n_heads, width), ((0, 0), (0, 0), (0, LANES - width))).reshape(d, n_heads * LANES)


def _nsa_layer(x, g_norm, w_in, w_out, q_gain, k_gain, cmp_pos, cmp_w1, cmp_b1, cmp_w2, cmp_b2):
    b, t, d = x.shape
    gn, rh, dh = N_KV_GROUPS, HEADS_PER_GROUP, HEAD_DIM
    assert t % SEL_CHUNK == 0 and t >= SEL_CHUNK + WINDOW
    x2 = x.reshape(b * t, d)
    scale = LOG2E / math.sqrt(dh)

    kvw = lambda i: w_in[:, d + i * KV_DIM:d + (i + 1) * KV_DIM]
    wq = _pad_heads(w_in[:, :d], N_HEADS, dh).astype(BF16)
    wk = jnp.concatenate([_pad_heads(kvw(i), gn, dh) for i in (2, 3, 4, 5)], axis=1).astype(BF16)
    wc = jnp.concatenate([kvw(0), kvw(1)], axis=1).astype(BF16)
    wg = _pad_heads(w_in[:, d + 6 * KV_DIM:], gn, rh * N_BRANCHES).astype(BF16)
    pad_gain = lambda v, n: jnp.tile(jnp.pad(v, (0, LANES - dh)), n)[None, :]
    qg_p = pad_gain(q_gain * scale, N_HEADS)
    kg_p = jnp.concatenate([pad_gain(k_gain[1], gn), pad_gain(k_gain[2], gn)], axis=0)
    seg = jnp.arange(2 * LANES) // LANES
    ebd = (seg[:, None] == seg[None, :]).astype(BF16)

    q, ka, va, kw, vw, kc_raw, vc_raw, gates = _nsa_proj(x2, b, t, g_norm[None, :], wq, wk, wc, wg, qg_p, kg_p, ebd)

    tc = t // CMP_STRIDE
    half = CMP_STRIDE * dh
    blk = lambda a: a.reshape(b, tc, CMP_STRIDE, gn, dh).transpose(0, 3, 1, 2, 4).reshape(b, gn, tc, half)
    comp = []
    for i, (raw, norm) in enumerate(((kc_raw, True), (vc_raw, False))):
        w1 = cmp_w1[i].astype(BF16)
        comp.append(_compress(blk(raw), cmp_pos[i].reshape(2, half), w1[:half], w1[half:], cmp_b1[i][None, :],
                              jnp.pad(cmp_w2[i], ((0, 0), (0, LANES - dh))).astype(BF16),
                              jnp.pad(cmp_b2[i], (0, LANES - dh))[None, :],
                              jnp.pad(k_gain[0], (0, LANES - dh))[None, :], norm))
    kc, vc = comp
    vct = vc.transpose(0, 1, 3, 2)

    nsb = t // SEL_BLOCK
    n_i = jnp.arange(tc)
    j_i = jnp.arange(nsb)
    n_cmp = (t - CMP_BLOCK) // CMP_STRIDE + 1
    ovt = ((n_i[None, :] * CMP_STRIDE < (j_i[:, None] + 1) * SEL_BLOCK)
           & (n_i[None, :] * CMP_STRIDE + CMP_BLOCK > j_i[:, None] * SEL_BLOCK)
           & (n_i[None, :] < n_cmp)).astype(BF16)
    ocmp, selt = _cmp_select(q, kc, vct, ovt)

    vrows = dh + 2 * SUBLANES
    vat = va[..., :vrows].transpose(0, 1, 3, 2)
    vwt = vw[..., :vrows].transpose(0, 1, 3, 2)
    o = _sel_win(q, ka, vat, kw, vwt, selt, ocmp, gates)
    return _res_matmul(o.reshape(b * t, d), w_out.astype(BF16), x2).reshape(b, t, d)


def kernel(x, mix_norm, ffn_norm, nsa_w_in, nsa_w_out, nsa_q_gain, nsa_k_gain, nsa_cmp_pos, nsa_cmp_w1, nsa_cmp_b1, nsa_cmp_w2, nsa_cmp_b2, s5_w_in, s5_b_re, s5_b_im, s5_c_re, s5_c_im, s5_d, s5_log_dt, s5_a_re, s5_a_im, s5_w_glu, ffn_w_gate, ffn_w_up, ffn_w_down):
    b, t, d = x.shape
    depth = mix_norm.shape[0]
    for layer in range(depth):
        i = layer // 2
        if layer % 2 == 0:
            x = _nsa_layer(x, mix_norm[layer], nsa_w_in[i], nsa_w_out[i], nsa_q_gain[i], nsa_k_gain[i],
                           nsa_cmp_pos[i], nsa_cmp_w1[i], nsa_cmp_b1[i], nsa_cmp_w2[i], nsa_cmp_b2[i])
        else:
            x = _s5_layer(x, mix_norm[layer], s5_w_in[i], s5_b_re[i], s5_b_im[i], s5_c_re[i], s5_c_im[i],
                          s5_d[i], s5_log_dt[i], s5_a_re[i], s5_a_im[i], s5_w_glu[i])
        x = _ffn(x.reshape(b * t, d), ffn_norm[layer][None, :], ffn_w_gate[layer].astype(BF16),
                 ffn_w_up[layer].astype(BF16), ffn_w_down[layer].astype(BF16)).reshape(b, t, d)
    return x
```

```python
import functools
import math

import jax
import jax.numpy as jnp
from jax import lax
from jax.experimental import pallas as pl
from jax.experimental.pallas import tpu as pltpu

F32 = jnp.float32
BF16 = jnp.bfloat16

EPS = 1e-6
NEG = -1e30
MASK_BIAS = -1e9
LOG2E = 1.4426950408889634

N_HEADS = 16
HEAD_DIM = 64
N_KV_GROUPS = 4
HEADS_PER_GROUP = 4
KV_DIM = N_KV_GROUPS * HEAD_DIM
CMP_BLOCK = 32
CMP_STRIDE = 16
SEL_BLOCK = 64
SEL_TOPN = 16
N_FORCED = 3
WINDOW = 512
N_BRANCHES = 3
S5_GROUP = 16
S5_STATE = 64

LANES = 128
SUBLANES = 8
MXU_DIM = 256
SEL_CHUNK = 512
V7X_VMEM_LIMIT_BYTES = 56 * 1024 * 1024


def _cparams(*semantics):
    return pltpu.CompilerParams(dimension_semantics=semantics,
                                vmem_limit_bytes=V7X_VMEM_LIMIT_BYTES)


def _rms_rows(x, g):
    var = jnp.mean(x * x, axis=-1, keepdims=True)
    return x * lax.rsqrt(var + EPS) * g


def _dot(a, b):
    return jnp.dot(a, b, preferred_element_type=F32)


def _dot_nt(a, b):
    return lax.dot_general(a, b, (((1,), (1,)), ((), ())), preferred_element_type=F32)


def _seg_sumsq(v, ebd):
    sq = v * v
    hi = sq.astype(BF16)
    lo = (sq - hi.astype(F32)).astype(BF16)
    return _dot(hi, ebd) + _dot(lo, ebd)


def _nsa_proj_kernel(x_ref, g_ref, wq_ref, wk_ref, wc_ref, wg_ref, qg_ref, kg_ref, ebd_ref,
                     q_ref, ka_ref, va_ref, kw_ref, vw_ref, kcr_ref, vcr_ref, gate_ref):
    tm = x_ref.shape[0]
    h = _rms_rows(x_ref[...], g_ref[...]).astype(BF16)
    ebd = ebd_ref[...]
    inv_dh = 1.0 / HEAD_DIM
    slab = 2 * LANES

    q = _dot(h, wq_ref[...])
    qg = qg_ref[...]
    for pair in range(N_HEADS // 2):
        sl = slice(pair * slab, (pair + 1) * slab)
        qi = q[:, sl]
        qn = (qi * lax.rsqrt(_seg_sumsq(qi, ebd) * inv_dh + EPS) * qg[:, sl]).astype(BF16)
        for hh in range(2):
            g, r = divmod(2 * pair + hh, HEADS_PER_GROUP)
            q_ref[0, g, r] = qn[:, hh * LANES:(hh + 1) * LANES]

    row = lax.rem(pl.program_id(0) * tm, SEL_CHUNK) + lax.broadcasted_iota(jnp.int32, (tm, 1), 0)
    lane = lax.broadcasted_iota(jnp.int32, (1, LANES), 1)
    blk_flag = ((row // SEL_BLOCK) == (lane - HEAD_DIM)).astype(F32)
    one_flag = (lane == HEAD_DIM).astype(F32)

    kv = _dot(h, wk_ref[...])
    kg = kg_ref[...]
    plan = ((ka_ref, 0, blk_flag), (va_ref, None, one_flag), (kw_ref, 1, None), (vw_ref, None, one_flag))
    for part, (ref, gain_row, flag) in enumerate(plan):
        for half in range(2):
            base = part * N_KV_GROUPS * LANES + half * slab
            v = kv[:, base:base + slab]
            if gain_row is not None:
                v = v * lax.rsqrt(_seg_sumsq(v, ebd) * inv_dh + EPS) * kg[gain_row:gain_row + 1, half * slab:(half + 1) * slab]
            for hh in range(2):
                piece = v[:, hh * LANES:(hh + 1) * LANES]
                if flag is not None:
                    piece = piece + flag
                ref[0, 2 * half + hh] = piece.astype(BF16)

    c = _dot(h, wc_ref[...])
    kcr_ref[...] = c[:, :KV_DIM]
    vcr_ref[...] = c[:, KV_DIM:]
    gt = jax.nn.sigmoid(_dot(h, wg_ref[...]))
    for g in range(N_KV_GROUPS):
        gate_ref[0, g] = gt[:, g * LANES:(g + 1) * LANES]


def _nsa_proj(x2, b, t, g, wq, wk, wc, wg, qg_p, kg_p, ebd, tm=512):
    m, d = x2.shape
    assert SEL_CHUNK % tm == 0 and t % SEL_CHUNK == 0
    nt = t // tm
    gn, rh = N_KV_GROUPS, HEADS_PER_GROUP
    full = lambda a: pl.BlockSpec(a.shape, lambda i: (0,) * a.ndim)
    row = lambda n: pl.BlockSpec((tm, n), lambda i: (i, 0))
    head = pl.BlockSpec((1, gn, tm, LANES), lambda i: (i // nt, 0, i % nt, 0))
    outs = [jax.ShapeDtypeStruct((b, gn, rh, t, LANES), BF16)]
    outs += [jax.ShapeDtypeStruct((b, gn, t, LANES), BF16)] * 4
    outs += [jax.ShapeDtypeStruct((m, KV_DIM), F32)] * 2
    outs += [jax.ShapeDtypeStruct((b, gn, t, LANES), F32)]
    return pl.pallas_call(
        _nsa_proj_kernel, grid=(m // tm,),
        in_specs=[row(d), full(g), full(wq), full(wk), full(wc), full(wg), full(qg_p), full(kg_p), full(ebd)],
        out_specs=[pl.BlockSpec((1, gn, rh, tm, LANES), lambda i: (i // nt, 0, 0, i % nt, 0)),
                   head, head, head, head, row(KV_DIM), row(KV_DIM), head],
        out_shape=outs, compiler_params=_cparams("parallel"), name="nsa_proj",
    )(x2, g, wq, wk, wc, wg, qg_p, kg_p, ebd)


def _compress_kernel(a_ref, pos_ref, w1a_ref, w1b_ref, b1_ref, w2_ref, b2_ref, gain_ref, o_ref, *, norm):
    a = a_ref[0, 0]
    tc = a.shape[0]
    pos = pos_ref[...]
    h1 = _dot((a + pos[0:1]).astype(BF16), w1a_ref[...])
    h2 = _dot((a + pos[1:2]).astype(BF16), w1b_ref[...])
    h = h1 + pltpu.roll(h2, tc - 1, 0) + b1_ref[...]
    hid = jax.nn.gelu(h).astype(BF16)
    out = _dot(hid, w2_ref[...]) + b2_ref[...]
    if norm:
        var = jnp.sum(out * out, axis=-1, keepdims=True) * (1.0 / HEAD_DIM)
        out = out * lax.rsqrt(var + EPS) * gain_ref[...]
    o_ref[0, 0] = out.astype(o_ref.dtype)


def _compress(a, pos2, w1a, w1b, b1, w2, b2, gain, norm):
    b, g, tc, w = a.shape
    full = lambda z: pl.BlockSpec(z.shape, lambda i, j: (0,) * z.ndim)
    return pl.pallas_call(
        functools.partial(_compress_kernel, norm=norm), grid=(b, g),
        in_specs=[pl.BlockSpec((1, 1, tc, w), lambda i, j: (i, j, 0, 0)),
                  full(pos2), full(w1a), full(w1b), full(b1), full(w2), full(b2), full(gain)],
        out_specs=pl.BlockSpec((1, 1, tc, LANES), lambda i, j: (i, j, 0, 0)),
        out_shape=jax.ShapeDtypeStruct((b, g, tc, LANES), BF16),
        compiler_params=_cparams("parallel", "parallel"), name="nsa_compress",
    )(a, pos2, w1a, w1b, b1, w2, b2, gain)


def _cmp_select_kernel(q_ref, kc_ref, vct_ref, ovt_ref, ocmp_ref, sel_ref, bias_scr, *, top_n, n_tiers):
    tq = q_ref.shape[3]
    tc = kc_ref.shape[2]
    nsb = ovt_ref.shape[0]
    n_tiles = pl.num_programs(1)
    t = pl.program_id(1) * tq + lax.broadcasted_iota(jnp.int32, (1, tq), 1)
    cur = t // SEL_BLOCK
    any_vis = t >= CMP_BLOCK - 1
    n_rest = max(top_n - N_FORCED, 0)

    def tile_body(rows, blocks):
        n_idx = lax.broadcasted_iota(jnp.int32, (rows, 1), 0)
        bias = jnp.where((n_idx * CMP_STRIDE + (CMP_BLOCK - 1)) <= t, 0.0, NEG)
        j_idx = lax.broadcasted_iota(jnp.int32, (blocks, 1), 0)
        forced = (j_idx == 0) | (j_idx == cur) | (j_idx == cur - 1)
        valid = j_idx <= cur
        j_f = j_idx.astype(F32)
        bias_scr[:rows, :] = bias

        def group_body(g, carry):
            kc = kc_ref[0, g, :rows, :]
            vct = vct_ref[0, g, :, :rows]
            psum = jnp.zeros((rows, tq), F32)
            for r in range(HEADS_PER_GROUP):
                s = _dot_nt(kc, q_ref[0, g, r]) + bias_scr[:rows, :]
                e = jnp.exp2(s - jnp.max(s, axis=0, keepdims=True))
                den = jnp.sum(e, axis=0, keepdims=True)
                p = e * jnp.where(any_vis, 1.0 / den, 0.0)
                o_t = _dot(vct, p.astype(BF16))
                ocmp_ref[0, g, r] = o_t[:HEAD_DIM]
                psum = psum + p
            hi = psum.astype(BF16)
            lo = (psum - hi.astype(F32)).astype(BF16)
            ovt = ovt_ref[:blocks, :rows]
            psel = _dot(ovt, hi) + _dot(ovt, lo)
            score = jnp.where(valid & ~forced, psel, NEG)
            for _ in range(n_rest):
                mx = jnp.max(score, axis=0, keepdims=True)
                first = jnp.min(jnp.where(score == mx, j_f, float(nsb)), axis=0, keepdims=True)
                score = jnp.where(j_f == first, -jnp.inf, score)
            sel_ref[0, g, :blocks, :] = jnp.where(forced | ((score == -jnp.inf) & valid), 1.0, 0.0)
            if blocks < nsb:
                sel_ref[0, g, blocks:, :] = jnp.zeros((nsb - blocks, tq), F32)
            return carry

        lax.fori_loop(0, N_KV_GROUPS, group_body, 0)

    tier = (pl.program_id(1) * n_tiers) // n_tiles
    for k in range(n_tiers):
        pl.when(tier == k)(functools.partial(tile_body, tc * (k + 1) // n_tiers, nsb * (k + 1) // n_tiers))


def _cmp_select(q, kc, vct, ovt, tq=256):
    b, g, r, t_len, _ = q.shape
    tc = kc.shape[2]
    nsb = ovt.shape[0]
    tq = min(tq, t_len)
    top_n = min(SEL_TOPN, nsb)
    assert top_n >= N_FORCED
    n_tiers = 4
    assert (t_len // tq) % n_tiers == 0 and nsb % (n_tiers * SUBLANES) == 0 and tc % (n_tiers * LANES // 4) == 0
    return pl.pallas_call(
        functools.partial(_cmp_select_kernel, top_n=top_n, n_tiers=n_tiers), grid=(b, t_len // tq),
        in_specs=[pl.BlockSpec((1, g, r, tq, LANES), lambda i, j: (i, 0, 0, j, 0)),
                  pl.BlockSpec((1, g, tc, LANES), lambda i, j: (i, 0, 0, 0)),
                  pl.BlockSpec((1, g, LANES, tc), lambda i, j: (i, 0, 0, 0)),
                  pl.BlockSpec(ovt.shape, lambda i, j: (0, 0))],
        out_specs=[pl.BlockSpec((1, g, r, HEAD_DIM, tq), lambda i, j: (i, 0, 0, 0, j)),
                   pl.BlockSpec((1, g, nsb, tq), lambda i, j: (i, 0, 0, j))],
        out_shape=[jax.ShapeDtypeStruct((b, g, r, HEAD_DIM, t_len), F32),
                   jax.ShapeDtypeStruct((b, g, nsb, t_len), F32)],
        scratch_shapes=[pltpu.VMEM((tc, tq), F32)],
        compiler_params=_cparams("parallel", "parallel"), name="nsa_cmp_select",
    )(q, kc, vct, ovt)


def _sel_win_kernel(q_ref, ka_ref, vat_ref, kw_ref, vwt_ref, selt_ref, ocmp_ref, gate_ref, o_ref,
                    qa_scr, sa_scr, sb_scr, ma_scr, mb_scr, *, kc):
    r_heads, tq = q_ref.shape[2], q_ref.shape[3]
    dh = HEAD_DIM
    cols = r_heads * tq
    t_len = ka_ref.shape[2]
    per_chunk = kc // SEL_BLOCK
    aug_rows = -(-per_chunk // (2 * SUBLANES)) * (2 * SUBLANES)
    t0 = pl.program_id(2) * tq
    t_lane = t0 + lax.broadcasted_iota(jnp.int32, (1, tq), 1)

    for r in range(r_heads):
        qa_scr[:, r * tq:(r + 1) * tq] = q_ref[0, 0, r].astype(F32).T.astype(BF16)

    def softmax_t(s, m_new):
        return jnp.exp2((s - m_new).astype(BF16))

    span = min(tq + WINDOW, t_len)
    w0 = pl.multiple_of(jnp.maximum(t0 + tq - span, 0), tq)
    kpos_w = w0 + lax.broadcasted_iota(jnp.int32, (span, 1), 0)
    in_win = (kpos_w <= t_lane) & (t_lane - kpos_w < WINDOW)
    bias_w = jnp.where(in_win, 0.0, MASK_BIAS)
    s = _dot(kw_ref[0, 0, pl.ds(w0, span), :], qa_scr[...]) + jnp.concatenate([bias_w] * r_heads, axis=1)
    ow = _dot(vwt_ref[0, 0, :, pl.ds(w0, span)], softmax_t(s, jnp.max(s, axis=0, keepdims=True)))
    o_win_t = ow[:dh] / ow[dh:dh + 1]

    n_full = t0 // kc
    last_full = jnp.maximum(n_full - 1, 0)

    def scores(c):
        ci = jnp.minimum(c, n_full)
        sel8 = selt_ref[0, 0, pl.ds(pl.multiple_of(ci * per_chunk, per_chunk), per_chunk), :]
        sb = (jnp.where(c <= n_full, sel8, 0.0) - 1.0) * (-MASK_BIAS)
        if aug_rows > per_chunk:
            sb = jnp.concatenate([sb, jnp.zeros((aug_rows - per_chunk, tq), F32)], axis=0)
        qa_scr[dh:dh + aug_rows, :] = jnp.concatenate([sb.astype(BF16)] * r_heads, axis=1)
        return _dot(ka_ref[0, 0, pl.ds(pl.multiple_of(ci * kc, kc), kc), :], qa_scr[...])

    def update(s, s_max, c, m, acc):
        ci = jnp.minimum(c, n_full)
        m_new = jnp.maximum(m, s_max)
        alpha = jnp.exp2(m - m_new)
        pv = _dot(vat_ref[0, 0, :, pl.ds(pl.multiple_of(ci * kc, kc), kc)], softmax_t(s, m_new))
        return m_new, alpha * acc + pv

    kpos = n_full * kc + lax.broadcasted_iota(jnp.int32, (kc, 1), 0)
    causal = jnp.where(kpos <= t_lane, 0.0, MASK_BIAS)
    s_diag = scores(n_full) + jnp.concatenate([causal] * r_heads, axis=1)
    init = (jnp.full((1, cols), NEG, F32), jnp.zeros((vat_ref.shape[2], cols), F32))
    carry = update(s_diag, jnp.max(s_diag, axis=0, keepdims=True), n_full, *init)

    def stash(c, s_scr, m_scr):
        c = jnp.where(c < n_full, c, n_full + 1)
        ci = jnp.minimum(c, n_full)
        sel8 = selt_ref[0, 0, pl.ds(pl.multiple_of(ci * per_chunk, per_chunk), per_chunk), :]
        sb = (jnp.where(c <= n_full, sel8, 0.0) - 1.0) * (-MASK_BIAS)
        if aug_rows > per_chunk:
            sb = jnp.concatenate([sb, jnp.zeros((aug_rows - per_chunk, tq), F32)], axis=0)
        qa_scr[dh:dh + aug_rows, :] = jnp.concatenate([sb.astype(BF16)] * r_heads, axis=1)
        k_rows = ka_ref[0, 0, pl.ds(pl.multiple_of(ci * kc, kc), kc), :]
        for n in range(r_heads):
            cs = slice(n * tq, (n + 1) * tq)
            s = _dot(k_rows, qa_scr[:, cs])
            s_scr[:, cs] = s
            m_scr[0:1, cs] = jnp.max(s, axis=0, keepdims=True)

    def update_scr(s_scr, m_scr, c, m, acc):
        ci = jnp.minimum(c, n_full)
        v_cols = vat_ref[0, 0, :, pl.ds(pl.multiple_of(ci * kc, kc), kc)]
        m_new = jnp.maximum(m, m_scr[0:1, :])
        alpha = jnp.exp2(m - m_new)
        pvs = []
        for n in range(r_heads):
            cs = slice(n * tq, (n + 1) * tq)
            pvs.append(_dot(v_cols, softmax_t(s_scr[:, cs], m_new[:, cs])))
        return m_new, alpha * acc + jnp.concatenate(pvs, axis=1)

    stash(0, sa_scr, ma_scr)

    def body(i, carry):
        c = 2 * i
        stash(c + 1, sb_scr, mb_scr)
        carry = update_scr(sa_scr, ma_scr, jnp.minimum(c, last_full), *carry)
        stash(c + 2, sa_scr, ma_scr)
        return update_scr(sb_scr, mb_scr, jnp.minimum(c + 1, last_full), *carry)

    _, acc = lax.fori_loop(0, (n_full + 1) // 2, body, carry)
    o_sel_t = acc[:dh] / acc[dh:dh + 1]

    gates_t = gate_ref[0, 0].T
    parts = []
    for r in range(r_heads):
        cs = slice(r * tq, (r + 1) * tq)
        parts.append(gates_t[3 * r:3 * r + 1] * ocmp_ref[0, 0, r]
                     + gates_t[3 * r + 1:3 * r + 2] * o_sel_t[:, cs]
                     + gates_t[3 * r + 2:3 * r + 3] * o_win_t[:, cs])
    o_ref[0] = jnp.concatenate(parts, axis=0).T.astype(o_ref.dtype)


def _sel_win(q, ka, vat, kw, vwt, selt, ocmp, gates, tq=256, kc=SEL_CHUNK):
    b, g, r, t_len, _ = q.shape
    nsb = selt.shape[2]
    vrows = vat.shape[2]
    gw = r * HEAD_DIM
    k_spec = pl.BlockSpec((1, 1, t_len, LANES), lambda i, j, k: (i, j, 0, 0))
    vt_spec = pl.BlockSpec((1, 1, vrows, t_len), lambda i, j, k: (i, j, 0, 0))
    return pl.pallas_call(
        functools.partial(_sel_win_kernel, kc=kc), grid=(b, g, t_len // tq),
        in_specs=[pl.BlockSpec((1, 1, r, tq, LANES), lambda i, j, k: (i, j, 0, k, 0)),
                  k_spec, vt_spec, k_spec, vt_spec,
                  pl.BlockSpec((1, 1, nsb, tq), lambda i, j, k: (i, j, 0, k)),
                  pl.BlockSpec((1, 1, r, HEAD_DIM, tq), lambda i, j, k: (i, j, 0, 0, k)),
                  pl.BlockSpec((1, 1, tq, LANES), lambda i, j, k: (i, j, k, 0))],
        out_specs=pl.BlockSpec((1, tq, gw), lambda i, j, k: (i, k, j)),
        out_shape=jax.ShapeDtypeStruct((b, t_len, g * gw), BF16),
        scratch_shapes=[pltpu.VMEM((LANES, r * tq), BF16),
                        pltpu.VMEM((kc, r * tq), F32), pltpu.VMEM((kc, r * tq), F32),
                        pltpu.VMEM((SUBLANES, r * tq), F32), pltpu.VMEM((SUBLANES, r * tq), F32)],
        compiler_params=_cparams("parallel", "parallel", "arbitrary"), name="nsa_sel_win",
    )(q, ka, vat, kw, vwt, selt, ocmp, gates)


def _ffn_tile(x, g_ref, wg_ref, wu_ref, wd_ref, hc):
    h = _rms_rows(x, g_ref[...]).astype(BF16)
    acc = x
    for c in range(wg_ref.shape[1] // hc):
        sl = slice(c * hc, (c + 1) * hc)
        a = jax.nn.silu(_dot(h, wg_ref[:, sl])) * _dot(h, wu_ref[:, sl])
        acc = acc + _dot(a.astype(BF16), wd_ref[sl, :])
    return acc


def _ffn_kernel(x_ref, g_ref, wg_ref, wu_ref, wd_ref, o_ref, *, hc):
    o_ref[...] = _ffn_tile(x_ref[...], g_ref, wg_ref, wu_ref, wd_ref, hc)


def _attn_out_ffn_kernel(a_ref, wo_ref, x_ref, g_ref, wg_ref, wu_ref, wd_ref, o_ref, *, hc):
    x = x_ref[...] + _dot(a_ref[...], wo_ref[...])
    o_ref[...] = _ffn_tile(x, g_ref, wg_ref, wu_ref, wd_ref, hc)


def _ffn(x2, g, wg, wu, wd, attn=None, tm=512, hc=MXU_DIM):
    m, d = x2.shape
    full = lambda a: pl.BlockSpec(a.shape, lambda i: (0,) * a.ndim)
    row = pl.BlockSpec((tm, d), lambda i: (i, 0))
    if attn is None:
        body, pre, pre_specs = _ffn_kernel, (), []
    else:
        body, pre = _attn_out_ffn_kernel, attn
        pre_specs = [pl.BlockSpec((tm, attn[0].shape[1]), lambda i: (i, 0)), full(attn[1])]
    return pl.pallas_call(
        functools.partial(body, hc=hc), grid=(m // tm,),
        in_specs=pre_specs + [row, full(g), full(wg), full(wu), full(wd)],
        out_specs=row, out_shape=jax.ShapeDtypeStruct((m, d), F32),
        compiler_params=_cparams("parallel"), name="ffn_swiglu",
    )(*pre, x2, g, wg, wu, wd)


def _s5_kernel(x_ref, g_ref, win_ref, pm_ref, pmt_ref, bre_ref, bim_ref, a1_ref, aseg_ref, pw_ref, cre_ref, cim_ref,
               d_ref, wglu_ref, o_ref, xr_scr, xi_scr, cin_scr, car_scr, *, lane_chunk):
    rows, d = x_ref.shape[1], x_ref.shape[2]
    n_state = xr_scr.shape[1]
    n_kt = bre_ref.shape[0]
    kw = bre_ref.shape[1]
    sw = bre_ref.shape[2]
    n_seg = SUBLANES
    seg = rows // n_seg

    @pl.when(pl.program_id(1) == 0)
    def _():
        car_scr[...] = jnp.zeros_like(car_scr)

    x = x_ref[0]
    h = _dot(pm_ref[...], _rms_rows(x, g_ref[...]).astype(BF16)).astype(BF16)
    u = _dot(h, win_ref[...])
    ub = u.astype(BF16)
    for kt in range(n_kt):
        uk = ub[:, kt * kw:(kt + 1) * kw]
        xr_scr[:, kt * sw:(kt + 1) * sw] = _dot(uk, bre_ref[kt])
        xi_scr[:, kt * sw:(kt + 1) * sw] = _dot(uk, bim_ref[kt])

    for lc in range(n_state // lane_chunk):
        ls = slice(lc * lane_chunk, (lc + 1) * lane_chunk)
        ar, ai = a1_ref[0, :, ls], a1_ref[1, :, ls]

        def step(r, carry, ls=ls, ar=ar, ai=ai):
            pr, pi = carry
            r0 = pl.multiple_of(r * SUBLANES, SUBLANES)
            nr = ar * pr - ai * pi + xr_scr[pl.ds(r0, SUBLANES), ls]
            ni = ar * pi + ai * pr + xi_scr[pl.ds(r0, SUBLANES), ls]
            xr_scr[pl.ds(r0, SUBLANES), ls] = nr
            xi_scr[pl.ds(r0, SUBLANES), ls] = ni
            return nr, ni

        zero = jnp.zeros((SUBLANES, lane_chunk), F32)
        lr, li = lax.fori_loop(0, seg, step, (zero, zero), unroll=4)

        sr, si = aseg_ref[0:1, ls], aseg_ref[1:2, ls]
        cr, ci = car_scr[0:1, ls], car_scr[1:2, ls]
        for s in range(n_seg):
            cin_scr[0, s:s + 1, ls] = cr
            cin_scr[1, s:s + 1, ls] = ci
            cr, ci = (sr * cr - si * ci + lr[s:s + 1], sr * ci + si * cr + li[s:s + 1])
        car_scr[0:1, ls] = cr
        car_scr[1:2, ls] = ci

        def fix(r, carry, ls=ls):
            r0 = pl.multiple_of(r * SUBLANES, SUBLANES)
            pr, pi = pw_ref[0, pl.ds(r, 1), ls], pw_ref[1, pl.ds(r, 1), ls]
            cr, ci = cin_scr[0, :, ls], cin_scr[1, :, ls]
            xr_scr[pl.ds(r0, SUBLANES), ls] += pr * cr - pi * ci
            xi_scr[pl.ds(r0, SUBLANES), ls] += pr * ci + pi * cr
            return carry

        lax.fori_loop(0, seg, fix, 0, unroll=4)

    ys = []
    for kt in range(n_kt):
        ss = slice(kt * sw, (kt + 1) * sw)
        ys.append(_dot(xr_scr[:, ss].astype(BF16), cre_ref[kt]) + _dot(xi_scr[:, ss].astype(BF16), cim_ref[kt]))
    y = jnp.concatenate(ys, axis=1) + d_ref[...] * u
    z = _dot(pmt_ref[...], jax.nn.gelu(y).astype(BF16)).astype(BF16)
    vg = _dot(z, wglu_ref[...])
    o_ref[0] = x + vg[:, :d] * jax.nn.sigmoid(vg[:, d:])


def _s5_tables(b_re, b_im, c_re, c_im, log_dt, a_re, a_im, seg):
    ng, ps, cg = b_re.shape
    per_tile = MXU_DIM // cg
    n_kt = ng // per_tile
    dt = jnp.exp(log_dt)[:, None]
    lam_r, lam_i = dt * a_re, dt * a_im
    mag = jnp.exp(lam_r)
    abar_r, abar_i = mag * jnp.cos(lam_i), mag * jnp.sin(lam_i)
    den = a_re * a_re + a_im * a_im
    coef_r = ((abar_r - 1.0) * a_re + abar_i * a_im) / den
    coef_i = (abar_i * a_re - (abar_r - 1.0) * a_im) / den
    bbar_r = coef_r[..., None] * b_re - coef_i[..., None] * b_im
    bbar_i = coef_r[..., None] * b_im + coef_i[..., None] * b_re
    eye = jnp.eye(per_tile, dtype=F32)
    bd_in = lambda m: jnp.einsum('kgpc,gh->kgchp', m.reshape(n_kt, per_tile, ps, cg), eye).reshape(
        n_kt, per_tile * cg, per_tile * ps).astype(BF16)
    bd_out = lambda m: jnp.einsum('kgcp,gh->kgphc', m.reshape(n_kt, per_tile, cg, ps), eye).reshape(
        n_kt, per_tile * ps, per_tile * cg).astype(BF16)

    def power(n):
        pm = jnp.exp(n * lam_r)
        return (pm * jnp.cos(n * lam_i)).reshape(-1), (pm * jnp.sin(n * lam_i)).reshape(-1)

    a1 = jnp.tile(jnp.stack(power(1.0))[:, None, :], (1, SUBLANES, 1))
    aseg = jnp.stack(power(float(seg)))
    steps = [power(float(i + 1)) for i in range(seg)]
    pw = jnp.stack([jnp.stack([p[0] for p in steps]), jnp.stack([p[1] for p in steps])])
    return bd_in(bbar_r), bd_in(bbar_i), a1, aseg, pw, bd_out(c_re), bd_out(-c_im)


def _s5_layer(x, g_norm, w_in, b_re, b_im, c_re, c_im, d_skip, log_dt, a_re, a_im, w_glu, rows=256):
    b, t, d = x.shape
    rows = min(rows, t)
    bre, bim, a1, aseg, pw, cre, cim = _s5_tables(b_re, b_im, c_re, c_im, log_dt, a_re, a_im, rows // SUBLANES)
    n_state = a1.shape[-1]
    full = lambda a: pl.BlockSpec(a.shape, lambda i, j: (0,) * a.ndim)
    seg = rows // SUBLANES
    new_row = jnp.arange(rows)
    pm = (jnp.arange(rows)[None, :] == (new_row % SUBLANES * seg + new_row // SUBLANES)[:, None]).astype(BF16)
    args = (g_norm[None, :], w_in.astype(BF16), pm, pm.T, bre, bim, a1, aseg, pw, cre, cim, d_skip[None, :],
            w_glu.astype(BF16))
    return pl.pallas_call(
        functools.partial(_s5_kernel, lane_chunk=1024), grid=(b, t // rows),
        in_specs=[pl.BlockSpec((1, rows, d), lambda i, j: (i, j, 0))] + [full(a) for a in args],
        out_specs=pl.BlockSpec((1, rows, d), lambda i, j: (i, j, 0)),
        out_shape=jax.ShapeDtypeStruct((b, t, d), F32),
        scratch_shapes=[pltpu.VMEM((rows, n_state), F32), pltpu.VMEM((rows, n_state), F32),
                        pltpu.VMEM((2, SUBLANES, n_state), F32), pltpu.VMEM((SUBLANES, n_state), F32)],
        compiler_params=_cparams("parallel", "arbitrary"), name="s5_mixer",
    )(x, *args)


def _pad_heads(w, n_heads, width):
    d = w.shape[0]
    return jnp.pad(w.reshape(d, n_heads, width), ((0, 0), (0, 0), (0, LANES - width))).reshape(d, n_heads * LANES)


def _nsa_layer(x, g_norm, w_in, w_out, q_gain, k_gain, cmp_pos, cmp_w1, cmp_b1, cmp_w2, cmp_b2):
    b, t, d = x.shape
    gn, rh, dh = N_KV_GROUPS, HEADS_PER_GROUP, HEAD_DIM
    assert t % SEL_CHUNK == 0 and t >= SEL_CHUNK + WINDOW
    x2 = x.reshape(b * t, d)
    scale = LOG2E / math.sqrt(dh)

    kvw = lambda i: w_in[:, d + i * KV_DIM:d + (i + 1) * KV_DIM]
    wq = _pad_heads(w_in[:, :d], N_HEADS, dh).astype(BF16)
    wk = jnp.concatenate([_pad_heads(kvw(i), gn, dh) for i in (2, 3, 4, 5)], axis=1).astype(BF16)
    wc = jnp.concatenate([kvw(0), kvw(1)], axis=1).astype(BF16)
    wg = _pad_heads(w_in[:, d + 6 * KV_DIM:], gn, rh * N_BRANCHES).astype(BF16)
    pad_gain = lambda v, n: jnp.tile(jnp.pad(v, (0, LANES - dh)), n)[None, :]
    qg_p = pad_gain(q_gain * scale, N_HEADS)
    kg_p = jnp.concatenate([pad_gain(k_gain[1], gn), pad_gain(k_gain[2], gn)], axis=0)
    seg = jnp.arange(2 * LANES) // LANES
    ebd = (seg[:, None] == seg[None, :]).astype(BF16)

    q, ka, va, kw, vw, kc_raw, vc_raw, gates = _nsa_proj(x2, b, t, g_norm[None, :], wq, wk, wc, wg, qg_p, kg_p, ebd)

    tc = t // CMP_STRIDE
    half = CMP_STRIDE * dh
    blk = lambda a: a.reshape(b, tc, CMP_STRIDE, gn, dh).transpose(0, 3, 1, 2, 4).reshape(b, gn, tc, half)
    comp = []
    for i, (raw, norm) in enumerate(((kc_raw, True), (vc_raw, False))):
        w1 = cmp_w1[i].astype(BF16)
        comp.append(_compress(blk(raw), cmp_pos[i].reshape(2, half), w1[:half], w1[half:], cmp_b1[i][None, :],
                              jnp.pad(cmp_w2[i], ((0, 0), (0, LANES - dh))).astype(BF16),
                              jnp.pad(cmp_b2[i], (0, LANES - dh))[None, :],
                              jnp.pad(k_gain[0], (0, LANES - dh))[None, :], norm))
    kc, vc = comp
    vct = vc.transpose(0, 1, 3, 2)

    nsb = t // SEL_BLOCK
    n_i = jnp.arange(tc)
    j_i = jnp.arange(nsb)
    n_cmp = (t - CMP_BLOCK) // CMP_STRIDE + 1
    ovt = ((n_i[None, :] * CMP_STRIDE < (j_i[:, None] + 1) * SEL_BLOCK)
           & (n_i[None, :] * CMP_STRIDE + CMP_BLOCK > j_i[:, None] * SEL_BLOCK)
           & (n_i[None, :] < n_cmp)).astype(BF16)
    ocmp, selt = _cmp_select(q, kc, vct, ovt)

    vrows = dh + 2 * SUBLANES
    vat = va[..., :vrows].transpose(0, 1, 3, 2)
    vwt = vw[..., :vrows].transpose(0, 1, 3, 2)
    o = _sel_win(q, ka, vat, kw, vwt, selt, ocmp, gates)
    return o.reshape(b * t, d), w_out.astype(BF16)


def kernel(x, mix_norm, ffn_norm, nsa_w_in, nsa_w_out, nsa_q_gain, nsa_k_gain, nsa_cmp_pos, nsa_cmp_w1, nsa_cmp_b1, nsa_cmp_w2, nsa_cmp_b2, s5_w_in, s5_b_re, s5_b_im, s5_c_re, s5_c_im, s5_d, s5_log_dt, s5_a_re, s5_a_im, s5_w_glu, ffn_w_gate, ffn_w_up, ffn_w_down):
    b, t, d = x.shape
    depth = mix_norm.shape[0]
    for layer in range(depth):
        i = layer // 2
        attn = None
        if layer % 2 == 0:
            attn = _nsa_layer(x, mix_norm[layer], nsa_w_in[i], nsa_w_out[i], nsa_q_gain[i], nsa_k_gain[i],
                              nsa_cmp_pos[i], nsa_cmp_w1[i], nsa_cmp_b1[i], nsa_cmp_w2[i], nsa_cmp_b2[i])
        else:
            x = _s5_layer(x, mix_norm[layer], s5_w_in[i], s5_b_re[i], s5_b_im[i], s5_c_re[i], s5_c_im[i],
                          s5_d[i], s5_log_dt[i], s5_a_re[i], s5_a_im[i], s5_w_glu[i])
        x = _ffn(x.reshape(b * t, d), ffn_norm[layer][None, :], ffn_w_gate[layer].astype(BF16),
                 ffn_w_up[layer].astype(BF16), ffn_w_down[layer].astype(BF16), attn=attn).reshape(b, t, d)
    return x
```

```python
import functools
import math

import jax
import jax.numpy as jnp
from jax import lax
from jax.experimental import pallas as pl
from jax.experimental.pallas import tpu as pltpu

F32 = jnp.float32
BF16 = jnp.bfloat16

EPS = 1e-6
NEG = -1e30
MASK_BIAS = -1e9
LOG2E = 1.4426950408889634

N_HEADS = 16
HEAD_DIM = 64
N_KV_GROUPS = 4
HEADS_PER_GROUP = 4
KV_DIM = N_KV_GROUPS * HEAD_DIM
CMP_BLOCK = 32
CMP_STRIDE = 16
SEL_BLOCK = 64
SEL_TOPN = 16
N_FORCED = 3
WINDOW = 512
N_BRANCHES = 3
S5_GROUP = 16
S5_STATE = 64

LANES = 128
SUBLANES = 8
MXU_DIM = 256
SEL_CHUNK = 512
V7X_VMEM_LIMIT_BYTES = 56 * 1024 * 1024


def _cparams(*semantics):
    return pltpu.CompilerParams(dimension_semantics=semantics,
                                vmem_limit_bytes=V7X_VMEM_LIMIT_BYTES)


def _rms_rows(x, g):
    var = jnp.mean(x * x, axis=-1, keepdims=True)
    return x * lax.rsqrt(var + EPS) * g


def _dot(a, b):
    return jnp.dot(a, b, preferred_element_type=F32)


def _dot_nt(a, b):
    return lax.dot_general(a, b, (((1,), (1,)), ((), ())), preferred_element_type=F32)


def _seg_sumsq(v, ebd):
    sq = v * v
    hi = sq.astype(BF16)
    lo = (sq - hi.astype(F32)).astype(BF16)
    return _dot(hi, ebd) + _dot(lo, ebd)


def _nsa_proj_kernel(x_ref, g_ref, wq_ref, wk_ref, wc_ref, wg_ref, qg_ref, kg_ref, ebd_ref,
                     q_ref, ka_ref, va_ref, kw_ref, vw_ref, kcr_ref, vcr_ref, gate_ref):
    tm = x_ref.shape[0]
    h = _rms_rows(x_ref[...], g_ref[...]).astype(BF16)
    ebd = ebd_ref[...]
    inv_dh = 1.0 / HEAD_DIM
    slab = 2 * LANES

    q = _dot(h, wq_ref[...])
    qg = qg_ref[...]
    for pair in range(N_HEADS // 2):
        sl = slice(pair * slab, (pair + 1) * slab)
        qi = q[:, sl]
        qn = (qi * lax.rsqrt(_seg_sumsq(qi, ebd) * inv_dh + EPS) * qg[:, sl]).astype(BF16)
        for hh in range(2):
            g, r = divmod(2 * pair + hh, HEADS_PER_GROUP)
            q_ref[0, g, r] = qn[:, hh * LANES:(hh + 1) * LANES]

    row = lax.rem(pl.program_id(0) * tm, SEL_CHUNK) + lax.broadcasted_iota(jnp.int32, (tm, 1), 0)
    lane = lax.broadcasted_iota(jnp.int32, (1, LANES), 1)
    blk_flag = ((row // SEL_BLOCK) == (lane - HEAD_DIM)).astype(F32)
    one_flag = (lane == HEAD_DIM).astype(F32)

    kv = _dot(h, wk_ref[...])
    kg = kg_ref[...]
    plan = ((ka_ref, 0, blk_flag), (va_ref, None, one_flag), (kw_ref, 1, None), (vw_ref, None, one_flag))
    for part, (ref, gain_row, flag) in enumerate(plan):
        for half in range(2):
            base = part * N_KV_GROUPS * LANES + half * slab
            v = kv[:, base:base + slab]
            if gain_row is not None:
                v = v * lax.rsqrt(_seg_sumsq(v, ebd) * inv_dh + EPS) * kg[gain_row:gain_row + 1, half * slab:(half + 1) * slab]
            for hh in range(2):
                piece = v[:, hh * LANES:(hh + 1) * LANES]
                if flag is not None:
                    piece = piece + flag
                ref[0, 2 * half + hh] = piece.astype(BF16)

    c = _dot(h, wc_ref[...])
    kcr_ref[...] = c[:, :KV_DIM]
    vcr_ref[...] = c[:, KV_DIM:]
    gt = jax.nn.sigmoid(_dot(h, wg_ref[...]))
    for g in range(N_KV_GROUPS):
        gate_ref[0, g] = gt[:, g * LANES:(g + 1) * LANES]


def _nsa_proj(x2, b, t, g, wq, wk, wc, wg, qg_p, kg_p, ebd, tm=512):
    m, d = x2.shape
    assert SEL_CHUNK % tm == 0 and t % SEL_CHUNK == 0
    nt = t // tm
    gn, rh = N_KV_GROUPS, HEADS_PER_GROUP
    full = lambda a: pl.BlockSpec(a.shape, lambda i: (0,) * a.ndim)
    row = lambda n: pl.BlockSpec((tm, n), lambda i: (i, 0))
    head = pl.BlockSpec((1, gn, tm, LANES), lambda i: (i // nt, 0, i % nt, 0))
    outs = [jax.ShapeDtypeStruct((b, gn, rh, t, LANES), BF16)]
    outs += [jax.ShapeDtypeStruct((b, gn, t, LANES), BF16)] * 4
    outs += [jax.ShapeDtypeStruct((m, KV_DIM), F32)] * 2
    outs += [jax.ShapeDtypeStruct((b, gn, t, LANES), F32)]
    return pl.pallas_call(
        _nsa_proj_kernel, grid=(m // tm,),
        in_specs=[row(d), full(g), full(wq), full(wk), full(wc), full(wg), full(qg_p), full(kg_p), full(ebd)],
        out_specs=[pl.BlockSpec((1, gn, rh, tm, LANES), lambda i: (i // nt, 0, 0, i % nt, 0)),
                   head, head, head, head, row(KV_DIM), row(KV_DIM), head],
        out_shape=outs, compiler_params=_cparams("parallel"), name="nsa_proj",
    )(x2, g, wq, wk, wc, wg, qg_p, kg_p, ebd)


def _compress_kernel(a_ref, pos_ref, w1a_ref, w1b_ref, b1_ref, w2_ref, b2_ref, gain_ref, o_ref, *, norm):
    a = a_ref[0, 0]
    tc = a.shape[0]
    pos = pos_ref[...]
    h1 = _dot((a + pos[0:1]).astype(BF16), w1a_ref[...])
    h2 = _dot((a + pos[1:2]).astype(BF16), w1b_ref[...])
    h = h1 + pltpu.roll(h2, tc - 1, 0) + b1_ref[...]
    hid = jax.nn.gelu(h).astype(BF16)
    out = _dot(hid, w2_ref[...]) + b2_ref[...]
    if norm:
        var = jnp.sum(out * out, axis=-1, keepdims=True) * (1.0 / HEAD_DIM)
        out = out * lax.rsqrt(var + EPS) * gain_ref[...]
    o_ref[0, 0] = out.astype(o_ref.dtype)


def _compress(a, pos2, w1a, w1b, b1, w2, b2, gain, norm):
    b, g, tc, w = a.shape
    full = lambda z: pl.BlockSpec(z.shape, lambda i, j: (0,) * z.ndim)
    return pl.pallas_call(
        functools.partial(_compress_kernel, norm=norm), grid=(b, g),
        in_specs=[pl.BlockSpec((1, 1, tc, w), lambda i, j: (i, j, 0, 0)),
                  full(pos2), full(w1a), full(w1b), full(b1), full(w2), full(b2), full(gain)],
        out_specs=pl.BlockSpec((1, 1, tc, LANES), lambda i, j: (i, j, 0, 0)),
        out_shape=jax.ShapeDtypeStruct((b, g, tc, LANES), BF16),
        compiler_params=_cparams("parallel", "parallel"), name="nsa_compress",
    )(a, pos2, w1a, w1b, b1, w2, b2, gain)


def _cmp_select_kernel(q_ref, kc_ref, vct_ref, ovt_ref, ocmp_ref, sel_ref, bias_scr, *, top_n, n_tiers):
    tq = q_ref.shape[3]
    tc = kc_ref.shape[2]
    nsb = ovt_ref.shape[0]
    n_tiles = pl.num_programs(1)
    t = pl.program_id(1) * tq + lax.broadcasted_iota(jnp.int32, (1, tq), 1)
    cur = t // SEL_BLOCK
    any_vis = t >= CMP_BLOCK - 1
    n_rest = max(top_n - N_FORCED, 0)

    def tile_body(rows, blocks):
        n_idx = lax.broadcasted_iota(jnp.int32, (rows, 1), 0)
        bias = jnp.where((n_idx * CMP_STRIDE + (CMP_BLOCK - 1)) <= t, 0.0, NEG)
        j_idx = lax.broadcasted_iota(jnp.int32, (blocks, 1), 0)
        forced = (j_idx == 0) | (j_idx == cur) | (j_idx == cur - 1)
        valid = j_idx <= cur
        j_f = j_idx.astype(F32)
        bias_scr[:rows, :] = bias

        def group_body(g, carry):
            kc = kc_ref[0, g, :rows, :]
            vct = vct_ref[0, g, :, :rows]
            psum = jnp.zeros((rows, tq), F32)
            for r in range(HEADS_PER_GROUP):
                s = _dot_nt(kc, q_ref[0, g, r]) + bias_scr[:rows, :]
                e = jnp.exp2(s - jnp.max(s, axis=0, keepdims=True))
                den = jnp.sum(e, axis=0, keepdims=True)
                p = e * jnp.where(any_vis, 1.0 / den, 0.0)
                o_t = _dot(vct, p.astype(BF16))
                ocmp_ref[0, g, r] = o_t[:HEAD_DIM]
                psum = psum + p
            hi = psum.astype(BF16)
            lo = (psum - hi.astype(F32)).astype(BF16)
            ovt = ovt_ref[:blocks, :rows]
            psel = _dot(ovt, hi) + _dot(ovt, lo)
            score = jnp.where(valid & ~forced, psel, NEG)
            for _ in range(n_rest):
                mx = jnp.max(score, axis=0, keepdims=True)
                first = jnp.min(jnp.where(score == mx, j_f, float(nsb)), axis=0, keepdims=True)
                score = jnp.where(j_f == first, -jnp.inf, score)
            sel_ref[0, g, :blocks, :] = jnp.where(forced | ((score == -jnp.inf) & valid), 1.0, 0.0)
            if blocks < nsb:
                sel_ref[0, g, blocks:, :] = jnp.zeros((nsb - blocks, tq), F32)
            return carry

        lax.fori_loop(0, N_KV_GROUPS, group_body, 0)

    tier = (pl.program_id(1) * n_tiers) // n_tiles
    for k in range(n_tiers):
        pl.when(tier == k)(functools.partial(tile_body, tc * (k + 1) // n_tiers, nsb * (k + 1) // n_tiers))


def _cmp_select(q, kc, vct, ovt, tq=256):
    b, g, r, t_len, _ = q.shape
    tc = kc.shape[2]
    nsb = ovt.shape[0]
    tq = min(tq, t_len)
    top_n = min(SEL_TOPN, nsb)
    assert top_n >= N_FORCED
    n_tiers = 4
    assert (t_len // tq) % n_tiers == 0 and nsb % (n_tiers * SUBLANES) == 0 and tc % (n_tiers * LANES // 4) == 0
    return pl.pallas_call(
        functools.partial(_cmp_select_kernel, top_n=top_n, n_tiers=n_tiers), grid=(b, t_len // tq),
        in_specs=[pl.BlockSpec((1, g, r, tq, LANES), lambda i, j: (i, 0, 0, j, 0)),
                  pl.BlockSpec((1, g, tc, LANES), lambda i, j: (i, 0, 0, 0)),
                  pl.BlockSpec((1, g, LANES, tc), lambda i, j: (i, 0, 0, 0)),
                  pl.BlockSpec(ovt.shape, lambda i, j: (0, 0))],
        out_specs=[pl.BlockSpec((1, g, r, HEAD_DIM, tq), lambda i, j: (i, 0, 0, 0, j)),
                   pl.BlockSpec((1, g, nsb, tq), lambda i, j: (i, 0, 0, j))],
        out_shape=[jax.ShapeDtypeStruct((b, g, r, HEAD_DIM, t_len), F32),
                   jax.ShapeDtypeStruct((b, g, nsb, t_len), F32)],
        scratch_shapes=[pltpu.VMEM((tc, tq), F32)],
        compiler_params=_cparams("parallel", "parallel"), name="nsa_cmp_select",
    )(q, kc, vct, ovt)


def _sel_win_kernel(q_ref, ka_ref, vat_ref, kw_ref, vwt_ref, selt_ref, ocmp_ref, gate_ref, o_ref,
                    qa_scr, sa_scr, sb_scr, ma_scr, mb_scr, *, kc):
    r_heads, tq = q_ref.shape[2], q_ref.shape[3]
    dh = HEAD_DIM
    cols = r_heads * tq
    t_len = ka_ref.shape[2]
    per_chunk = kc // SEL_BLOCK
    aug_rows = -(-per_chunk // (2 * SUBLANES)) * (2 * SUBLANES)
    t0 = pl.program_id(2) * tq
    t_lane = t0 + lax.broadcasted_iota(jnp.int32, (1, tq), 1)

    for r in range(r_heads):
        qa_scr[:, r * tq:(r + 1) * tq] = q_ref[0, 0, r].astype(F32).T.astype(BF16)

    def softmax_t(s, m_new):
        return jnp.exp2((s - m_new).astype(BF16))

    span = min(tq + WINDOW, t_len)
    w0 = pl.multiple_of(jnp.maximum(t0 + tq - span, 0), tq)
    kpos_w = w0 + lax.broadcasted_iota(jnp.int32, (span, 1), 0)
    in_win = (kpos_w <= t_lane) & (t_lane - kpos_w < WINDOW)
    bias_w = jnp.where(in_win, 0.0, MASK_BIAS)
    s = _dot(kw_ref[0, 0, pl.ds(w0, span), :], qa_scr[...]) + jnp.concatenate([bias_w] * r_heads, axis=1)
    ow = _dot(vwt_ref[0, 0, :, pl.ds(w0, span)], softmax_t(s, jnp.max(s, axis=0, keepdims=True)))
    o_win_t = ow[:dh] / ow[dh:dh + 1]

    n_full = t0 // kc
    last_full = jnp.maximum(n_full - 1, 0)

    def scores(c):
        ci = jnp.minimum(c, n_full)
        sel8 = selt_ref[0, 0, pl.ds(pl.multiple_of(ci * per_chunk, per_chunk), per_chunk), :]
        sb = (jnp.where(c <= n_full, sel8, 0.0) - 1.0) * (-MASK_BIAS)
        if aug_rows > per_chunk:
            sb = jnp.concatenate([sb, jnp.zeros((aug_rows - per_chunk, tq), F32)], axis=0)
        qa_scr[dh:dh + aug_rows, :] = jnp.concatenate([sb.astype(BF16)] * r_heads, axis=1)
        return _dot(ka_ref[0, 0, pl.ds(pl.multiple_of(ci * kc, kc), kc), :], qa_scr[...])

    def update(s, s_max, c, m, acc):
        ci = jnp.minimum(c, n_full)
        m_new = jnp.maximum(m, s_max)
        alpha = jnp.exp2(m - m_new)
        pv = _dot(vat_ref[0, 0, :, pl.ds(pl.multiple_of(ci * kc, kc), kc)], softmax_t(s, m_new))
        return m_new, alpha * acc + pv

    kpos = n_full * kc + lax.broadcasted_iota(jnp.int32, (kc, 1), 0)
    causal = jnp.where(kpos <= t_lane, 0.0, MASK_BIAS)
    s_diag = scores(n_full) + jnp.concatenate([causal] * r_heads, axis=1)
    init = (jnp.full((1, cols), NEG, F32), jnp.zeros((vat_ref.shape[2], cols), F32))
    carry = update(s_diag, jnp.max(s_diag, axis=0, keepdims=True), n_full, *init)

    def stash(c, s_scr, m_scr):
        c = jnp.where(c < n_full, c, n_full + 1)
        ci = jnp.minimum(c, n_full)
        sel8 = selt_ref[0, 0, pl.ds(pl.multiple_of(ci * per_chunk, per_chunk), per_chunk), :]
        sb = (jnp.where(c <= n_full, sel8, 0.0) - 1.0) * (-MASK_BIAS)
        if aug_rows > per_chunk:
            sb = jnp.concatenate([sb, jnp.zeros((aug_rows - per_chunk, tq), F32)], axis=0)
        qa_scr[dh:dh + aug_rows, :] = jnp.concatenate([sb.astype(BF16)] * r_heads, axis=1)
        k_rows = ka_ref[0, 0, pl.ds(pl.multiple_of(ci * kc, kc), kc), :]
        for n in range(r_heads):
            cs = slice(n * tq, (n + 1) * tq)
            s = _dot(k_rows, qa_scr[:, cs])
            s_scr[:, cs] = s
            m_scr[0:1, cs] = jnp.max(s, axis=0, keepdims=True)

    def update_scr(s_scr, m_scr, c, m, acc):
        ci = jnp.minimum(c, n_full)
        v_cols = vat_ref[0, 0, :, pl.ds(pl.multiple_of(ci * kc, kc), kc)]
        m_new = jnp.maximum(m, m_scr[0:1, :])
        alpha = jnp.exp2(m - m_new)
        pvs = []
        for n in range(r_heads):
            cs = slice(n * tq, (n + 1) * tq)
            pvs.append(_dot(v_cols, softmax_t(s_scr[:, cs], m_new[:, cs])))
        return m_new, alpha * acc + jnp.concatenate(pvs, axis=1)

    stash(0, sa_scr, ma_scr)

    def body(i, carry):
        c = 2 * i
        stash(c + 1, sb_scr, mb_scr)
        carry = update_scr(sa_scr, ma_scr, jnp.minimum(c, last_full), *carry)
        stash(c + 2, sa_scr, ma_scr)
        return update_scr(sb_scr, mb_scr, jnp.minimum(c + 1, last_full), *carry)

    carry = lax.fori_loop(0, n_full // 2, body, carry)
    _, acc = lax.cond(n_full % 2 == 1,
                      lambda cr: update_scr(sa_scr, ma_scr, last_full, *cr), lambda cr: cr, carry)
    o_sel_t = acc[:dh] / acc[dh:dh + 1]

    gates_t = gate_ref[0, 0].T
    parts = []
    for r in range(r_heads):
        cs = slice(r * tq, (r + 1) * tq)
        parts.append(gates_t[3 * r:3 * r + 1] * ocmp_ref[0, 0, r]
                     + gates_t[3 * r + 1:3 * r + 2] * o_sel_t[:, cs]
                     + gates_t[3 * r + 2:3 * r + 3] * o_win_t[:, cs])
    o_ref[0] = jnp.concatenate(parts, axis=0).T.astype(o_ref.dtype)


def _sel_win(q, ka, vat, kw, vwt, selt, ocmp, gates, tq=256, kc=SEL_CHUNK):
    b, g, r, t_len, _ = q.shape
    nsb = selt.shape[2]
    vrows = vat.shape[2]
    gw = r * HEAD_DIM
    k_spec = pl.BlockSpec((1, 1, t_len, LANES), lambda i, j, k: (i, j, 0, 0))
    vt_spec = pl.BlockSpec((1, 1, vrows, t_len), lambda i, j, k: (i, j, 0, 0))
    return pl.pallas_call(
        functools.partial(_sel_win_kernel, kc=kc), grid=(b, g, t_len // tq),
        in_specs=[pl.BlockSpec((1, 1, r, tq, LANES), lambda i, j, k: (i, j, 0, k, 0)),
                  k_spec, vt_spec, k_spec, vt_spec,
                  pl.BlockSpec((1, 1, nsb, tq), lambda i, j, k: (i, j, 0, k)),
                  pl.BlockSpec((1, 1, r, HEAD_DIM, tq), lambda i, j, k: (i, j, 0, 0, k)),
                  pl.BlockSpec((1, 1, tq, LANES), lambda i, j, k: (i, j, k, 0))],
        out_specs=pl.BlockSpec((1, tq, gw), lambda i, j, k: (i, k, j)),
        out_shape=jax.ShapeDtypeStruct((b, t_len, g * gw), BF16),
        scratch_shapes=[pltpu.VMEM((LANES, r * tq), BF16),
                        pltpu.VMEM((kc, r * tq), F32), pltpu.VMEM((kc, r * tq), F32),
                        pltpu.VMEM((SUBLANES, r * tq), F32), pltpu.VMEM((SUBLANES, r * tq), F32)],
        compiler_params=_cparams("parallel", "parallel", "arbitrary"), name="nsa_sel_win",
    )(q, ka, vat, kw, vwt, selt, ocmp, gates)


def _ffn_tile(x, g_ref, wg_ref, wu_ref, wd_ref, hc):
    h = _rms_rows(x, g_ref[...]).astype(BF16)
    acc = x
    for c in range(wg_ref.shape[1] // hc):
        sl = slice(c * hc, (c + 1) * hc)
        a = jax.nn.silu(_dot(h, wg_ref[:, sl])) * _dot(h, wu_ref[:, sl])
        acc = acc + _dot(a.astype(BF16), wd_ref[sl, :])
    return acc


def _ffn_kernel(x_ref, g_ref, wg_ref, wu_ref, wd_ref, o_ref, *, hc):
    o_ref[...] = _ffn_tile(x_ref[...], g_ref, wg_ref, wu_ref, wd_ref, hc)


def _attn_out_ffn_kernel(a_ref, wo_ref, x_ref, g_ref, wg_ref, wu_ref, wd_ref, o_ref, *, hc):
    x = x_ref[...] + _dot(a_ref[...], wo_ref[...])
    o_ref[...] = _ffn_tile(x, g_ref, wg_ref, wu_ref, wd_ref, hc)


def _ffn(x2, g, wg, wu, wd, attn=None, tm=512, hc=MXU_DIM):
    m, d = x2.shape
    full = lambda a: pl.BlockSpec(a.shape, lambda i: (0,) * a.ndim)
    row = pl.BlockSpec((tm, d), lambda i: (i, 0))
    if attn is None:
        body, pre, pre_specs = _ffn_kernel, (), []
    else:
        body, pre = _attn_out_ffn_kernel, attn
        pre_specs = [pl.BlockSpec((tm, attn[0].shape[1]), lambda i: (i, 0)), full(attn[1])]
    return pl.pallas_call(
        functools.partial(body, hc=hc), grid=(m // tm,),
        in_specs=pre_specs + [row, full(g), full(wg), full(wu), full(wd)],
        out_specs=row, out_shape=jax.ShapeDtypeStruct((m, d), F32),
        compiler_params=_cparams("parallel"), name="ffn_swiglu",
    )(*pre, x2, g, wg, wu, wd)


def _s5_kernel(x_ref, g_ref, win_ref, pm_ref, pmt_ref, bre_ref, bim_ref, a1_ref, aseg_ref, pw_ref, cre_ref, cim_ref,
               d_ref, wglu_ref, o_ref, xr_scr, xi_scr, cin_scr, car_scr, *, lane_chunk):
    rows, d = x_ref.shape[1], x_ref.shape[2]
    n_state = xr_scr.shape[1]
    n_kt = bre_ref.shape[0]
    kw = bre_ref.shape[1]
    sw = bre_ref.shape[2]
    n_seg = SUBLANES
    seg = rows // n_seg

    @pl.when(pl.program_id(1) == 0)
    def _():
        car_scr[...] = jnp.zeros_like(car_scr)

    x = x_ref[0]
    h = _dot(pm_ref[...], _rms_rows(x, g_ref[...]).astype(BF16)).astype(BF16)
    u = _dot(h, win_ref[...])
    ub = u.astype(BF16)
    for kt in range(n_kt):
        uk = ub[:, kt * kw:(kt + 1) * kw]
        xr_scr[:, kt * sw:(kt + 1) * sw] = _dot(uk, bre_ref[kt])
        xi_scr[:, kt * sw:(kt + 1) * sw] = _dot(uk, bim_ref[kt])

    for lc in range(n_state // lane_chunk):
        ls = slice(lc * lane_chunk, (lc + 1) * lane_chunk)
        ar, ai = a1_ref[0, :, ls], a1_ref[1, :, ls]

        def step(r, carry, ls=ls, ar=ar, ai=ai):
            pr, pi = carry
            r0 = pl.multiple_of(r * SUBLANES, SUBLANES)
            nr = ar * pr - ai * pi + xr_scr[pl.ds(r0, SUBLANES), ls]
            ni = ar * pi + ai * pr + xi_scr[pl.ds(r0, SUBLANES), ls]
            xr_scr[pl.ds(r0, SUBLANES), ls] = nr
            xi_scr[pl.ds(r0, SUBLANES), ls] = ni
            return nr, ni

        zero = jnp.zeros((SUBLANES, lane_chunk), F32)
        lr, li = lax.fori_loop(0, seg, step, (zero, zero), unroll=4)

        sr, si = aseg_ref[0:1, ls], aseg_ref[1:2, ls]
        cr, ci = car_scr[0:1, ls], car_scr[1:2, ls]
        for s in range(n_seg):
            cin_scr[0, s:s + 1, ls] = cr
            cin_scr[1, s:s + 1, ls] = ci
            cr, ci = (sr * cr - si * ci + lr[s:s + 1], sr * ci + si * cr + li[s:s + 1])
        car_scr[0:1, ls] = cr
        car_scr[1:2, ls] = ci

        def fix(r, carry, ls=ls):
            r0 = pl.multiple_of(r * SUBLANES, SUBLANES)
            pr, pi = pw_ref[0, pl.ds(r, 1), ls], pw_ref[1, pl.ds(r, 1), ls]
            cr, ci = cin_scr[0, :, ls], cin_scr[1, :, ls]
            xr_scr[pl.ds(r0, SUBLANES), ls] += pr * cr - pi * ci
            xi_scr[pl.ds(r0, SUBLANES), ls] += pr * ci + pi * cr
            return carry

        lax.fori_loop(0, seg, fix, 0, unroll=4)

    ys = []
    for kt in range(n_kt):
        ss = slice(kt * sw, (kt + 1) * sw)
        ys.append(_dot(xr_scr[:, ss].astype(BF16), cre_ref[kt]) + _dot(xi_scr[:, ss].astype(BF16), cim_ref[kt]))
    y = jnp.concatenate(ys, axis=1) + d_ref[...] * u
    z = _dot(pmt_ref[...], jax.nn.gelu(y).astype(BF16)).astype(BF16)
    vg = _dot(z, wglu_ref[...])
    o_ref[0] = x + vg[:, :d] * jax.nn.sigmoid(vg[:, d:])


def _s5_tables(b_re, b_im, c_re, c_im, log_dt, a_re, a_im, seg):
    ng, ps, cg = b_re.shape
    per_tile = MXU_DIM // cg
    n_kt = ng // per_tile
    dt = jnp.exp(log_dt)[:, None]
    lam_r, lam_i = dt * a_re, dt * a_im
    mag = jnp.exp(lam_r)
    abar_r, abar_i = mag * jnp.cos(lam_i), mag * jnp.sin(lam_i)
    den = a_re * a_re + a_im * a_im
    coef_r = ((abar_r - 1.0) * a_re + abar_i * a_im) / den
    coef_i = (abar_i * a_re - (abar_r - 1.0) * a_im) / den
    bbar_r = coef_r[..., None] * b_re - coef_i[..., None] * b_im
    bbar_i = coef_r[..., None] * b_im + coef_i[..., None] * b_re
    eye = jnp.eye(per_tile, dtype=F32)
    bd_in = lambda m: jnp.einsum('kgpc,gh->kgchp', m.reshape(n_kt, per_tile, ps, cg), eye).reshape(
        n_kt, per_tile * cg, per_tile * ps).astype(BF16)
    bd_out = lambda m: jnp.einsum('kgcp,gh->kgphc', m.reshape(n_kt, per_tile, cg, ps), eye).reshape(
        n_kt, per_tile * ps, per_tile * cg).astype(BF16)

    def power(n):
        pm = jnp.exp(n * lam_r)
        return (pm * jnp.cos(n * lam_i)).reshape(-1), (pm * jnp.sin(n * lam_i)).reshape(-1)

    a1 = jnp.tile(jnp.stack(power(1.0))[:, None, :], (1, SUBLANES, 1))
    aseg = jnp.stack(power(float(seg)))
    steps = [power(float(i + 1)) for i in range(seg)]
    pw = jnp.stack([jnp.stack([p[0] for p in steps]), jnp.stack([p[1] for p in steps])])
    return bd_in(bbar_r), bd_in(bbar_i), a1, aseg, pw, bd_out(c_re), bd_out(-c_im)


def _s5_layer(x, g_norm, w_in, b_re, b_im, c_re, c_im, d_skip, log_dt, a_re, a_im, w_glu, rows=256):
    b, t, d = x.shape
    rows = min(rows, t)
    bre, bim, a1, aseg, pw, cre, cim = _s5_tables(b_re, b_im, c_re, c_im, log_dt, a_re, a_im, rows // SUBLANES)
    n_state = a1.shape[-1]
    full = lambda a: pl.BlockSpec(a.shape, lambda i, j: (0,) * a.ndim)
    seg = rows // SUBLANES
    new_row = jnp.arange(rows)
    pm = (jnp.arange(rows)[None, :] == (new_row % SUBLANES * seg + new_row // SUBLANES)[:, None]).astype(BF16)
    args = (g_norm[None, :], w_in.astype(BF16), pm, pm.T, bre, bim, a1, aseg, pw, cre, cim, d_skip[None, :],
            w_glu.astype(BF16))
    return pl.pallas_call(
        functools.partial(_s5_kernel, lane_chunk=1024), grid=(b, t // rows),
        in_specs=[pl.BlockSpec((1, rows, d), lambda i, j: (i, j, 0))] + [full(a) for a in args],
        out_specs=pl.BlockSpec((1, rows, d), lambda i, j: (i, j, 0)),
        out_shape=jax.ShapeDtypeStruct((b, t, d), F32),
        scratch_shapes=[pltpu.VMEM((rows, n_state), F32), pltpu.VMEM((rows, n_state), F32),
                        pltpu.VMEM((2, SUBLANES, n_state), F32), pltpu.VMEM((SUBLANES, n_state), F32)],
        compiler_params=_cparams("parallel", "arbitrary"), name="s5_mixer",
    )(x, *args)


def _pad_heads(w, n_heads, width):
    d = w.shape[0]
    return jnp.pad(w.reshape(d, n_heads, width), ((0, 0), (0, 0), (0, LANES - width))).reshape(d, n_heads * LANES)


def _nsa_layer(x, g_norm, w_in, w_out, q_gain, k_gain, cmp_pos, cmp_w1, cmp_b1, cmp_w2, cmp_b2):
    b, t, d = x.shape
    gn, rh, dh = N_KV_GROUPS, HEADS_PER_GROUP, HEAD_DIM
    assert t % SEL_CHUNK == 0 and t >= SEL_CHUNK + WINDOW
    x2 = x.reshape(b * t, d)
    scale = LOG2E / math.sqrt(dh)

    kvw = lambda i: w_in[:, d + i * KV_DIM:d + (i + 1) * KV_DIM]
    wq = _pad_heads(w_in[:, :d], N_HEADS, dh).astype(BF16)
    wk = jnp.concatenate([_pad_heads(kvw(i), gn, dh) for i in (2, 3, 4, 5)], axis=1).astype(BF16)
    wc = jnp.concatenate([kvw(0), kvw(1)], axis=1).astype(BF16)
    wg = _pad_heads(w_in[:, d + 6 * KV_DIM:], gn, rh * N_BRANCHES).astype(BF16)
    pad_gain = lambda v, n: jnp.tile(jnp.pad(v, (0, LANES - dh)), n)[None, :]
    qg_p = pad_gain(q_gain * scale, N_HEADS)
    kg_p = jnp.concatenate([pad_gain(k_gain[1], gn), pad_gain(k_gain[2], gn)], axis=0)
    seg = jnp.arange(2 * LANES) // LANES
    ebd = (seg[:, None] == seg[None, :]).astype(BF16)

    q, ka, va, kw, vw, kc_raw, vc_raw, gates = _nsa_proj(x2, b, t, g_norm[None, :], wq, wk, wc, wg, qg_p, kg_p, ebd)

    tc = t // CMP_STRIDE
    half = CMP_STRIDE * dh
    blk = lambda a: a.reshape(b, tc, CMP_STRIDE, gn, dh).transpose(0, 3, 1, 2, 4).reshape(b, gn, tc, half)
    comp = []
    for i, (raw, norm) in enumerate(((kc_raw, True), (vc_raw, False))):
        w1 = cmp_w1[i].astype(BF16)
        comp.append(_compress(blk(raw), cmp_pos[i].reshape(2, half), w1[:half], w1[half:], cmp_b1[i][None, :],
                              jnp.pad(cmp_w2[i], ((0, 0), (0, LANES - dh))).astype(BF16),
                              jnp.pad(cmp_b2[i], (0, LANES - dh))[None, :],
                              jnp.pad(k_gain[0], (0, LANES - dh))[None, :], norm))
    kc, vc = comp
    vct = vc.transpose(0, 1, 3, 2)

    nsb = t // SEL_BLOCK
    n_i = jnp.arange(tc)
    j_i = jnp.arange(nsb)
    n_cmp = (t - CMP_BLOCK) // CMP_STRIDE + 1
    ovt = ((n_i[None, :] * CMP_STRIDE < (j_i[:, None] + 1) * SEL_BLOCK)
           & (n_i[None, :] * CMP_STRIDE + CMP_BLOCK > j_i[:, None] * SEL_BLOCK)
           & (n_i[None, :] < n_cmp)).astype(BF16)
    ocmp, selt = _cmp_select(q, kc, vct, ovt)

    vrows = dh + 2 * SUBLANES
    vat = va[..., :vrows].transpose(0, 1, 3, 2)
    vwt = vw[..., :vrows].transpose(0, 1, 3, 2)
    o = _sel_win(q, ka, vat, kw, vwt, selt, ocmp, gates)
    return o.reshape(b * t, d), w_out.astype(BF16)


def kernel(x, mix_norm, ffn_norm, nsa_w_in, nsa_w_out, nsa_q_gain, nsa_k_gain, nsa_cmp_pos, nsa_cmp_w1, nsa_cmp_b1, nsa_cmp_w2, nsa_cmp_b2, s5_w_in, s5_b_re, s5_b_im, s5_c_re, s5_c_im, s5_d, s5_log_dt, s5_a_re, s5_a_im, s5_w_glu, ffn_w_gate, ffn_w_up, ffn_w_down):
    b, t, d = x.shape
    depth = mix_norm.shape[0]
    for layer in range(depth):
        i = layer // 2
        attn = None
        if layer % 2 == 0:
            attn = _nsa_layer(x, mix_norm[layer], nsa_w_in[i], nsa_w_out[i], nsa_q_gain[i], nsa_k_gain[i],
                              nsa_cmp_pos[i], nsa_cmp_w1[i], nsa_cmp_b1[i], nsa_cmp_w2[i], nsa_cmp_b2[i])
        else:
            x = _s5_layer(x, mix_norm[layer], s5_w_in[i], s5_b_re[i], s5_b_im[i], s5_c_re[i], s5_c_im[i],
                          s5_d[i], s5_log_dt[i], s5_a_re[i], s5_a_im[i], s5_w_glu[i])
        x = _ffn(x.reshape(b * t, d), ffn_norm[layer][None, :], ffn_w_gate[layer].astype(BF16),
                 ffn_w_up[layer].astype(BF16), ffn_w_down[layer].astype(BF16), attn=attn).reshape(b, t, d)
    return x
```

```python
import functools
import math

import jax
import jax.numpy as jnp
from jax import lax
from jax.experimental import pallas as pl
from jax.experimental.pallas import tpu as pltpu

F32 = jnp.float32
BF16 = jnp.bfloat16

EPS = 1e-6
NEG = -1e30
MASK_BIAS = -1e9
LOG2E = 1.4426950408889634

N_HEADS = 16
HEAD_DIM = 64
N_KV_GROUPS = 4
HEADS_PER_GROUP = 4
KV_DIM = N_KV_GROUPS * HEAD_DIM
CMP_BLOCK = 32
CMP_STRIDE = 16
SEL_BLOCK = 64
SEL_TOPN = 16
N_FORCED = 3
WINDOW = 512
N_BRANCHES = 3
S5_GROUP = 16
S5_STATE = 64

LANES = 128
SUBLANES = 8
MXU_DIM = 256
SEL_CHUNK = 512
V7X_VMEM_LIMIT_BYTES = 56 * 1024 * 1024


def _cparams(*semantics):
    return pltpu.CompilerParams(dimension_semantics=semantics,
                                vmem_limit_bytes=V7X_VMEM_LIMIT_BYTES)


def _rms_rows(x, g):
    var = jnp.mean(x * x, axis=-1, keepdims=True)
    return x * lax.rsqrt(var + EPS) * g


def _dot(a, b):
    return jnp.dot(a, b, preferred_element_type=F32)


def _dot_nt(a, b):
    return lax.dot_general(a, b, (((1,), (1,)), ((), ())), preferred_element_type=F32)


def _seg_sumsq(v, ebd):
    sq = v * v
    hi = sq.astype(BF16)
    lo = (sq - hi.astype(F32)).astype(BF16)
    return _dot(hi, ebd) + _dot(lo, ebd)


def _nsa_proj_kernel(x_ref, g_ref, wq_ref, wk_ref, wc_ref, wg_ref, qg_ref, kg_ref, ebd_ref,
                     q_ref, ka_ref, va_ref, kw_ref, vw_ref, kcr_ref, vcr_ref, gate_ref):
    tm = x_ref.shape[0]
    h = _rms_rows(x_ref[...], g_ref[...]).astype(BF16)
    ebd = ebd_ref[...]
    inv_dh = 1.0 / HEAD_DIM
    slab = 2 * LANES

    q = _dot(h, wq_ref[...])
    qg = qg_ref[...]
    for pair in range(N_HEADS // 2):
        sl = slice(pair * slab, (pair + 1) * slab)
        qi = q[:, sl]
        qn = (qi * lax.rsqrt(_seg_sumsq(qi, ebd) * inv_dh + EPS) * qg[:, sl]).astype(BF16)
        for hh in range(2):
            g, r = divmod(2 * pair + hh, HEADS_PER_GROUP)
            q_ref[0, g, r] = qn[:, hh * LANES:(hh + 1) * LANES]

    row = lax.rem(pl.program_id(0) * tm, SEL_CHUNK) + lax.broadcasted_iota(jnp.int32, (tm, 1), 0)
    lane = lax.broadcasted_iota(jnp.int32, (1, LANES), 1)
    blk_flag = ((row // SEL_BLOCK) == (lane - HEAD_DIM)).astype(F32)
    one_flag = (lane == HEAD_DIM).astype(F32)

    kv = _dot(h, wk_ref[...])
    kg = kg_ref[...]
    plan = ((ka_ref, 0, blk_flag), (va_ref, None, one_flag), (kw_ref, 1, None), (vw_ref, None, one_flag))
    for part, (ref, gain_row, flag) in enumerate(plan):
        for half in range(2):
            base = part * N_KV_GROUPS * LANES + half * slab
            v = kv[:, base:base + slab]
            if gain_row is not None:
                v = v * lax.rsqrt(_seg_sumsq(v, ebd) * inv_dh + EPS) * kg[gain_row:gain_row + 1, half * slab:(half + 1) * slab]
            for hh in range(2):
                piece = v[:, hh * LANES:(hh + 1) * LANES]
                if flag is not None:
                    piece = piece + flag
                ref[0, 2 * half + hh] = piece.astype(BF16)

    c = _dot(h, wc_ref[...])
    kcr_ref[...] = c[:, :KV_DIM]
    vcr_ref[...] = c[:, KV_DIM:]
    gt = jax.nn.sigmoid(_dot(h, wg_ref[...]))
    for g in range(N_KV_GROUPS):
        gate_ref[0, g] = gt[:, g * LANES:(g + 1) * LANES]


def _nsa_proj(x2, b, t, g, wq, wk, wc, wg, qg_p, kg_p, ebd, tm=512):
    m, d = x2.shape
    assert SEL_CHUNK % tm == 0 and t % SEL_CHUNK == 0
    nt = t // tm
    gn, rh = N_KV_GROUPS, HEADS_PER_GROUP
    full = lambda a: pl.BlockSpec(a.shape, lambda i: (0,) * a.ndim)
    row = lambda n: pl.BlockSpec((tm, n), lambda i: (i, 0))
    head = pl.BlockSpec((1, gn, tm, LANES), lambda i: (i // nt, 0, i % nt, 0))
    outs = [jax.ShapeDtypeStruct((b, gn, rh, t, LANES), BF16)]
    outs += [jax.ShapeDtypeStruct((b, gn, t, LANES), BF16)] * 4
    outs += [jax.ShapeDtypeStruct((m, KV_DIM), F32)] * 2
    outs += [jax.ShapeDtypeStruct((b, gn, t, LANES), F32)]
    return pl.pallas_call(
        _nsa_proj_kernel, grid=(m // tm,),
        in_specs=[row(d), full(g), full(wq), full(wk), full(wc), full(wg), full(qg_p), full(kg_p), full(ebd)],
        out_specs=[pl.BlockSpec((1, gn, rh, tm, LANES), lambda i: (i // nt, 0, 0, i % nt, 0)),
                   head, head, head, head, row(KV_DIM), row(KV_DIM), head],
        out_shape=outs, compiler_params=_cparams("parallel"), name="nsa_proj",
    )(x2, g, wq, wk, wc, wg, qg_p, kg_p, ebd)


def _compress_kernel(a_ref, pos_ref, w1a_ref, w1b_ref, b1_ref, w2_ref, b2_ref, gain_ref, o_ref, *, norm):
    a = a_ref[0, 0]
    tc = a.shape[0]
    pos = pos_ref[...]
    h1 = _dot((a + pos[0:1]).astype(BF16), w1a_ref[...])
    h2 = _dot((a + pos[1:2]).astype(BF16), w1b_ref[...])
    h = h1 + pltpu.roll(h2, tc - 1, 0) + b1_ref[...]
    hid = jax.nn.gelu(h).astype(BF16)
    out = _dot(hid, w2_ref[...]) + b2_ref[...]
    if norm:
        var = jnp.sum(out * out, axis=-1, keepdims=True) * (1.0 / HEAD_DIM)
        out = out * lax.rsqrt(var + EPS) * gain_ref[...]
    o_ref[0, 0] = out.astype(o_ref.dtype)


def _compress(a, pos2, w1a, w1b, b1, w2, b2, gain, norm):
    b, g, tc, w = a.shape
    full = lambda z: pl.BlockSpec(z.shape, lambda i, j: (0,) * z.ndim)
    return pl.pallas_call(
        functools.partial(_compress_kernel, norm=norm), grid=(b, g),
        in_specs=[pl.BlockSpec((1, 1, tc, w), lambda i, j: (i, j, 0, 0)),
                  full(pos2), full(w1a), full(w1b), full(b1), full(w2), full(b2), full(gain)],
        out_specs=pl.BlockSpec((1, 1, tc, LANES), lambda i, j: (i, j, 0, 0)),
        out_shape=jax.ShapeDtypeStruct((b, g, tc, LANES), BF16),
        compiler_params=_cparams("parallel", "parallel"), name="nsa_compress",
    )(a, pos2, w1a, w1b, b1, w2, b2, gain)


def _cmp_select_kernel(q_ref, kc_ref, vct_ref, ovt_ref, ocmp_ref, sel_ref, bias_scr, *, top_n, n_tiers):
    tq = q_ref.shape[3]
    tc = kc_ref.shape[2]
    nsb = ovt_ref.shape[0]
    n_tiles = pl.num_programs(1)
    t = pl.program_id(1) * tq + lax.broadcasted_iota(jnp.int32, (1, tq), 1)
    cur = t // SEL_BLOCK
    any_vis = t >= CMP_BLOCK - 1
    n_rest = max(top_n - N_FORCED, 0)

    def tile_body(rows, blocks):
        n_idx = lax.broadcasted_iota(jnp.int32, (rows, 1), 0)
        bias = jnp.where((n_idx * CMP_STRIDE + (CMP_BLOCK - 1)) <= t, 0.0, NEG)
        j_idx = lax.broadcasted_iota(jnp.int32, (blocks, 1), 0)
        forced = (j_idx == 0) | (j_idx == cur) | (j_idx == cur - 1)
        valid = j_idx <= cur
        j_f = j_idx.astype(F32)
        bias_scr[:rows, :] = bias

        def group_body(g, carry):
            kc = kc_ref[0, g, :rows, :]
            vct = vct_ref[0, g, :, :rows]
            psum = jnp.zeros((rows, tq), F32)
            for r in range(HEADS_PER_GROUP):
                qr = q_ref[0, g, r]
                s = jnp.concatenate([_dot_nt(kc[:rows // 2], qr), _dot_nt(kc[rows // 2:], qr)], axis=0)
                s = s + bias_scr[:rows, :]
                e = jnp.exp2(s - jnp.max(s, axis=0, keepdims=True))
                den = jnp.sum(e, axis=0, keepdims=True)
                p = e * jnp.where(any_vis, 1.0 / den, 0.0)
                pb = p.astype(BF16)
                o_t = (_dot(vct[:, :rows // 2], pb[:rows // 2])
                       + _dot(vct[:, rows // 2:], pb[rows // 2:]))
                ocmp_ref[0, g, r] = o_t[:HEAD_DIM]
                psum = psum + p
            hi = psum.astype(BF16)
            lo = (psum - hi.astype(F32)).astype(BF16)
            ovt = ovt_ref[:blocks, :rows]
            psel = _dot(ovt, hi) + _dot(ovt, lo)
            score = jnp.where(valid & ~forced, psel, NEG)
            for _ in range(n_rest):
                mx = jnp.max(score, axis=0, keepdims=True)
                first = jnp.min(jnp.where(score == mx, j_f, float(nsb)), axis=0, keepdims=True)
                score = jnp.where(j_f == first, -jnp.inf, score)
            sel_ref[0, g, :blocks, :] = jnp.where(forced | ((score == -jnp.inf) & valid), 1.0, 0.0)
            if blocks < nsb:
                sel_ref[0, g, blocks:, :] = jnp.zeros((nsb - blocks, tq), F32)
            return carry

        lax.fori_loop(0, N_KV_GROUPS, group_body, 0)

    tier = (pl.program_id(1) * n_tiers) // n_tiles
    for k in range(n_tiers):
        pl.when(tier == k)(functools.partial(tile_body, tc * (k + 1) // n_tiers, nsb * (k + 1) // n_tiers))


def _cmp_select(q, kc, vct, ovt, tq=256):
    b, g, r, t_len, _ = q.shape
    tc = kc.shape[2]
    nsb = ovt.shape[0]
    tq = min(tq, t_len)
    top_n = min(SEL_TOPN, nsb)
    assert top_n >= N_FORCED
    n_tiers = 4
    assert (t_len // tq) % n_tiers == 0 and nsb % (n_tiers * SUBLANES) == 0 and tc % (n_tiers * LANES // 4) == 0
    return pl.pallas_call(
        functools.partial(_cmp_select_kernel, top_n=top_n, n_tiers=n_tiers), grid=(b, t_len // tq),
        in_specs=[pl.BlockSpec((1, g, r, tq, LANES), lambda i, j: (i, 0, 0, j, 0)),
                  pl.BlockSpec((1, g, tc, LANES), lambda i, j: (i, 0, 0, 0)),
                  pl.BlockSpec((1, g, LANES, tc), lambda i, j: (i, 0, 0, 0)),
                  pl.BlockSpec(ovt.shape, lambda i, j: (0, 0))],
        out_specs=[pl.BlockSpec((1, g, r, HEAD_DIM, tq), lambda i, j: (i, 0, 0, 0, j)),
                   pl.BlockSpec((1, g, nsb, tq), lambda i, j: (i, 0, 0, j))],
        out_shape=[jax.ShapeDtypeStruct((b, g, r, HEAD_DIM, t_len), F32),
                   jax.ShapeDtypeStruct((b, g, nsb, t_len), F32)],
        scratch_shapes=[pltpu.VMEM((tc, tq), F32)],
        compiler_params=_cparams("parallel", "parallel"), name="nsa_cmp_select",
    )(q, kc, vct, ovt)


def _sel_win_kernel(q_ref, ka_ref, vat_ref, kw_ref, vwt_ref, selt_ref, ocmp_ref, gate_ref, o_ref,
                    qa_scr, sa_scr, sb_scr, ma_scr, mb_scr, *, kc):
    r_heads, tq = q_ref.shape[2], q_ref.shape[3]
    dh = HEAD_DIM
    cols = r_heads * tq
    t_len = ka_ref.shape[2]
    per_chunk = kc // SEL_BLOCK
    aug_rows = -(-per_chunk // (2 * SUBLANES)) * (2 * SUBLANES)
    t0 = pl.program_id(2) * tq
    t_lane = t0 + lax.broadcasted_iota(jnp.int32, (1, tq), 1)

    for r in range(r_heads):
        qa_scr[:, r * tq:(r + 1) * tq] = q_ref[0, 0, r].astype(F32).T.astype(BF16)

    def softmax_t(s, m_new):
        return jnp.exp2((s - m_new).astype(BF16))

    span = min(tq + WINDOW, t_len)
    w0 = pl.multiple_of(jnp.maximum(t0 + tq - span, 0), tq)
    kpos_w = w0 + lax.broadcasted_iota(jnp.int32, (span, 1), 0)
    in_win = (kpos_w <= t_lane) & (t_lane - kpos_w < WINDOW)
    bias_w = jnp.where(in_win, 0.0, MASK_BIAS)
    s = _dot(kw_ref[0, 0, pl.ds(w0, span), :], qa_scr[...]) + jnp.concatenate([bias_w] * r_heads, axis=1)
    ow = _dot(vwt_ref[0, 0, :, pl.ds(w0, span)], softmax_t(s, jnp.max(s, axis=0, keepdims=True)))
    o_win_t = ow[:dh] / ow[dh:dh + 1]

    n_full = t0 // kc
    last_full = jnp.maximum(n_full - 1, 0)

    def scores(c):
        ci = jnp.minimum(c, n_full)
        sel8 = selt_ref[0, 0, pl.ds(pl.multiple_of(ci * per_chunk, per_chunk), per_chunk), :]
        sb = (jnp.where(c <= n_full, sel8, 0.0) - 1.0) * (-MASK_BIAS)
        if aug_rows > per_chunk:
            sb = jnp.concatenate([sb, jnp.zeros((aug_rows - per_chunk, tq), F32)], axis=0)
        qa_scr[dh:dh + aug_rows, :] = jnp.concatenate([sb.astype(BF16)] * r_heads, axis=1)
        return _dot(ka_ref[0, 0, pl.ds(pl.multiple_of(ci * kc, kc), kc), :], qa_scr[...])

    def update(s, s_max, c, m, acc):
        ci = jnp.minimum(c, n_full)
        m_new = jnp.maximum(m, s_max)
        alpha = jnp.exp2(m - m_new)
        pv = _dot(vat_ref[0, 0, :, pl.ds(pl.multiple_of(ci * kc, kc), kc)], softmax_t(s, m_new))
        return m_new, alpha * acc + pv

    kpos = n_full * kc + lax.broadcasted_iota(jnp.int32, (kc, 1), 0)
    causal = jnp.where(kpos <= t_lane, 0.0, MASK_BIAS)
    s_diag = scores(n_full) + jnp.concatenate([causal] * r_heads, axis=1)
    init = (jnp.full((1, cols), NEG, F32), jnp.zeros((vat_ref.shape[2], cols), F32))
    carry = update(s_diag, jnp.max(s_diag, axis=0, keepdims=True), n_full, *init)

    def stash(c, s_scr, m_scr):
        c = jnp.where(c < n_full, c, n_full + 1)
        ci = jnp.minimum(c, n_full)
        sel8 = selt_ref[0, 0, pl.ds(pl.multiple_of(ci * per_chunk, per_chunk), per_chunk), :]
        sb = (jnp.where(c <= n_full, sel8, 0.0) - 1.0) * (-MASK_BIAS)
        if aug_rows > per_chunk:
            sb = jnp.concatenate([sb, jnp.zeros((aug_rows - per_chunk, tq), F32)], axis=0)
        qa_scr[dh:dh + aug_rows, :] = jnp.concatenate([sb.astype(BF16)] * r_heads, axis=1)
        k_rows = ka_ref[0, 0, pl.ds(pl.multiple_of(ci * kc, kc), kc), :]
        for n in range(r_heads):
            cs = slice(n * tq, (n + 1) * tq)
            s = _dot(k_rows, qa_scr[:, cs])
            s_scr[:, cs] = s
            m_scr[0:1, cs] = jnp.max(s, axis=0, keepdims=True)

    def update_scr(s_scr, m_scr, c, m, acc):
        ci = jnp.minimum(c, n_full)
        v_cols = vat_ref[0, 0, :, pl.ds(pl.multiple_of(ci * kc, kc), kc)]
        m_new = jnp.maximum(m, m_scr[0:1, :])
        alpha = jnp.exp2(m - m_new)
        pvs = []
        for n in range(r_heads):
            cs = slice(n * tq, (n + 1) * tq)
            pvs.append(_dot(v_cols, softmax_t(s_scr[:, cs], m_new[:, cs])))
        return m_new, alpha * acc + jnp.concatenate(pvs, axis=1)

    stash(0, sa_scr, ma_scr)

    def body(i, carry):
        c = 2 * i
        stash(c + 1, sb_scr, mb_scr)
        carry = update_scr(sa_scr, ma_scr, jnp.minimum(c, last_full), *carry)
        stash(c + 2, sa_scr, ma_scr)
        return update_scr(sb_scr, mb_scr, jnp.minimum(c + 1, last_full), *carry)

    carry = lax.fori_loop(0, n_full // 2, body, carry)
    _, acc = lax.cond(n_full % 2 == 1,
                      lambda cr: update_scr(sa_scr, ma_scr, last_full, *cr), lambda cr: cr, carry)
    o_sel_t = acc[:dh] / acc[dh:dh + 1]

    gates_t = gate_ref[0, 0].T
    parts = []
    for r in range(r_heads):
        cs = slice(r * tq, (r + 1) * tq)
        parts.append(gates_t[3 * r:3 * r + 1] * ocmp_ref[0, 0, r]
                     + gates_t[3 * r + 1:3 * r + 2] * o_sel_t[:, cs]
                     + gates_t[3 * r + 2:3 * r + 3] * o_win_t[:, cs])
    o_ref[0] = jnp.concatenate(parts, axis=0).T.astype(o_ref.dtype)


def _sel_win(q, ka, vat, kw, vwt, selt, ocmp, gates, tq=256, kc=SEL_CHUNK):
    b, g, r, t_len, _ = q.shape
    nsb = selt.shape[2]
    vrows = vat.shape[2]
    gw = r * HEAD_DIM
    k_spec = pl.BlockSpec((1, 1, t_len, LANES), lambda i, j, k: (i, j, 0, 0))
    vt_spec = pl.BlockSpec((1, 1, vrows, t_len), lambda i, j, k: (i, j, 0, 0))
    return pl.pallas_call(
        functools.partial(_sel_win_kernel, kc=kc), grid=(b, g, t_len // tq),
        in_specs=[pl.BlockSpec((1, 1, r, tq, LANES), lambda i, j, k: (i, j, 0, k, 0)),
                  k_spec, vt_spec, k_spec, vt_spec,
                  pl.BlockSpec((1, 1, nsb, tq), lambda i, j, k: (i, j, 0, k)),
                  pl.BlockSpec((1, 1, r, HEAD_DIM, tq), lambda i, j, k: (i, j, 0, 0, k)),
                  pl.BlockSpec((1, 1, tq, LANES), lambda i, j, k: (i, j, k, 0))],
        out_specs=pl.BlockSpec((1, tq, gw), lambda i, j, k: (i, k, j)),
        out_shape=jax.ShapeDtypeStruct((b, t_len, g * gw), BF16),
        scratch_shapes=[pltpu.VMEM((LANES, r * tq), BF16),
                        pltpu.VMEM((kc, r * tq), F32), pltpu.VMEM((kc, r * tq), F32),
                        pltpu.VMEM((SUBLANES, r * tq), F32), pltpu.VMEM((SUBLANES, r * tq), F32)],
        compiler_params=_cparams("parallel", "parallel", "arbitrary"), name="nsa_sel_win",
    )(q, ka, vat, kw, vwt, selt, ocmp, gates)


def _ffn_tile(x, g_ref, wg_ref, wu_ref, wd_ref, hc):
    h = _rms_rows(x, g_ref[...]).astype(BF16)
    acc = x
    for c in range(wg_ref.shape[1] // hc):
        sl = slice(c * hc, (c + 1) * hc)
        a = jax.nn.silu(_dot(h, wg_ref[:, sl])) * _dot(h, wu_ref[:, sl])
        acc = acc + _dot(a.astype(BF16), wd_ref[sl, :])
    return acc


def _ffn_kernel(x_ref, g_ref, wg_ref, wu_ref, wd_ref, o_ref, *, hc):
    o_ref[...] = _ffn_tile(x_ref[...], g_ref, wg_ref, wu_ref, wd_ref, hc)


def _attn_out_ffn_kernel(a_ref, wo_ref, x_ref, g_ref, wg_ref, wu_ref, wd_ref, o_ref, *, hc):
    x = x_ref[...] + _dot(a_ref[...], wo_ref[...])
    o_ref[...] = _ffn_tile(x, g_ref, wg_ref, wu_ref, wd_ref, hc)


def _ffn(x2, g, wg, wu, wd, attn=None, tm=512, hc=MXU_DIM):
    m, d = x2.shape
    full = lambda a: pl.BlockSpec(a.shape, lambda i: (0,) * a.ndim)
    row = pl.BlockSpec((tm, d), lambda i: (i, 0))
    if attn is None:
        body, pre, pre_specs = _ffn_kernel, (), []
    else:
        body, pre = _attn_out_ffn_kernel, attn
        pre_specs = [pl.BlockSpec((tm, attn[0].shape[1]), lambda i: (i, 0)), full(attn[1])]
    return pl.pallas_call(
        functools.partial(body, hc=hc), grid=(m // tm,),
        in_specs=pre_specs + [row, full(g), full(wg), full(wu), full(wd)],
        out_specs=row, out_shape=jax.ShapeDtypeStruct((m, d), F32),
        compiler_params=_cparams("parallel"), name="ffn_swiglu",
    )(*pre, x2, g, wg, wu, wd)


def _s5_kernel(x_ref, g_ref, win_ref, pm_ref, pmt_ref, bre_ref, bim_ref, a1_ref, aseg_ref, pw_ref, cre_ref, cim_ref,
               d_ref, wglu_ref, o_ref, xr_scr, xi_scr, cin_scr, car_scr, *, lane_chunk):
    rows, d = x_ref.shape[1], x_ref.shape[2]
    n_state = xr_scr.shape[1]
    n_kt = bre_ref.shape[0]
    kw = bre_ref.shape[1]
    sw = bre_ref.shape[2]
    n_seg = SUBLANES
    seg = rows // n_seg

    @pl.when(pl.program_id(1) == 0)
    def _():
        car_scr[...] = jnp.zeros_like(car_scr)

    x = x_ref[0]
    h = _dot(pm_ref[...], _rms_rows(x, g_ref[...]).astype(BF16)).astype(BF16)
    u = _dot(h, win_ref[...])
    ub = u.astype(BF16)
    for kt in range(n_kt):
        uk = ub[:, kt * kw:(kt + 1) * kw]
        xr_scr[:, kt * sw:(kt + 1) * sw] = _dot(uk, bre_ref[kt])
        xi_scr[:, kt * sw:(kt + 1) * sw] = _dot(uk, bim_ref[kt])

    for lc in range(n_state // lane_chunk):
        ls = slice(lc * lane_chunk, (lc + 1) * lane_chunk)
        ar, ai = a1_ref[0, :, ls], a1_ref[1, :, ls]

        def step(r, carry, ls=ls, ar=ar, ai=ai):
            pr, pi = carry
            r0 = pl.multiple_of(r * SUBLANES, SUBLANES)
            nr = ar * pr - ai * pi + xr_scr[pl.ds(r0, SUBLANES), ls]
            ni = ar * pi + ai * pr + xi_scr[pl.ds(r0, SUBLANES), ls]
            xr_scr[pl.ds(r0, SUBLANES), ls] = nr
            xi_scr[pl.ds(r0, SUBLANES), ls] = ni
            return nr, ni

        zero = jnp.zeros((SUBLANES, lane_chunk), F32)
        lr, li = lax.fori_loop(0, seg, step, (zero, zero), unroll=4)

        sr, si = aseg_ref[0:1, ls], aseg_ref[1:2, ls]
        cr, ci = car_scr[0:1, ls], car_scr[1:2, ls]
        for s in range(n_seg):
            cin_scr[0, s:s + 1, ls] = cr
            cin_scr[1, s:s + 1, ls] = ci
            cr, ci = (sr * cr - si * ci + lr[s:s + 1], sr * ci + si * cr + li[s:s + 1])
        car_scr[0:1, ls] = cr
        car_scr[1:2, ls] = ci

        def fix(r, carry, ls=ls):
            r0 = pl.multiple_of(r * SUBLANES, SUBLANES)
            pr, pi = pw_ref[0, pl.ds(r, 1), ls], pw_ref[1, pl.ds(r, 1), ls]
            cr, ci = cin_scr[0, :, ls], cin_scr[1, :, ls]
            xr_scr[pl.ds(r0, SUBLANES), ls] += pr * cr - pi * ci
            xi_scr[pl.ds(r0, SUBLANES), ls] += pr * ci + pi * cr
            return carry

        lax.fori_loop(0, seg, fix, 0, unroll=4)

    ys = []
    for kt in range(n_kt):
        ss = slice(kt * sw, (kt + 1) * sw)
        ys.append(_dot(xr_scr[:, ss].astype(BF16), cre_ref[kt]) + _dot(xi_scr[:, ss].astype(BF16), cim_ref[kt]))
    y = jnp.concatenate(ys, axis=1) + d_ref[...] * u
    z = _dot(pmt_ref[...], jax.nn.gelu(y).astype(BF16)).astype(BF16)
    vg = _dot(z, wglu_ref[...])
    o_ref[0] = x + vg[:, :d] * jax.nn.sigmoid(vg[:, d:])


def _s5_tables(b_re, b_im, c_re, c_im, log_dt, a_re, a_im, seg):
    ng, ps, cg = b_re.shape
    per_tile = MXU_DIM // cg
    n_kt = ng // per_tile
    dt = jnp.exp(log_dt)[:, None]
    lam_r, lam_i = dt * a_re, dt * a_im
    mag = jnp.exp(lam_r)
    abar_r, abar_i = mag * jnp.cos(lam_i), mag * jnp.sin(lam_i)
    den = a_re * a_re + a_im * a_im
    coef_r = ((abar_r - 1.0) * a_re + abar_i * a_im) / den
    coef_i = (abar_i * a_re - (abar_r - 1.0) * a_im) / den
    bbar_r = coef_r[..., None] * b_re - coef_i[..., None] * b_im
    bbar_i = coef_r[..., None] * b_im + coef_i[..., None] * b_re
    eye = jnp.eye(per_tile, dtype=F32)
    bd_in = lambda m: jnp.einsum('kgpc,gh->kgchp', m.reshape(n_kt, per_tile, ps, cg), eye).reshape(
        n_kt, per_tile * cg, per_tile * ps).astype(BF16)
    bd_out = lambda m: jnp.einsum('kgcp,gh->kgphc', m.reshape(n_kt, per_tile, cg, ps), eye).reshape(
        n_kt, per_tile * ps, per_tile * cg).astype(BF16)

    def power(n):
        pm = jnp.exp(n * lam_r)
        return (pm * jnp.cos(n * lam_i)).reshape(-1), (pm * jnp.sin(n * lam_i)).reshape(-1)

    a1 = jnp.tile(jnp.stack(power(1.0))[:, None, :], (1, SUBLANES, 1))
    aseg = jnp.stack(power(float(seg)))
    steps = [power(float(i + 1)) for i in range(seg)]
    pw = jnp.stack([jnp.stack([p[0] for p in steps]), jnp.stack([p[1] for p in steps])])
    return bd_in(bbar_r), bd_in(bbar_i), a1, aseg, pw, bd_out(c_re), bd_out(-c_im)


def _s5_layer(x, g_norm, w_in, b_re, b_im, c_re, c_im, d_skip, log_dt, a_re, a_im, w_glu, rows=256):
    b, t, d = x.shape
    rows = min(rows, t)
    bre, bim, a1, aseg, pw, cre, cim = _s5_tables(b_re, b_im, c_re, c_im, log_dt, a_re, a_im, rows // SUBLANES)
    n_state = a1.shape[-1]
    full = lambda a: pl.BlockSpec(a.shape, lambda i, j: (0,) * a.ndim)
    seg = rows // SUBLANES
    new_row = jnp.arange(rows)
    pm = (jnp.arange(rows)[None, :] == (new_row % SUBLANES * seg + new_row // SUBLANES)[:, None]).astype(BF16)
    args = (g_norm[None, :], w_in.astype(BF16), pm, pm.T, bre, bim, a1, aseg, pw, cre, cim, d_skip[None, :],
            w_glu.astype(BF16))
    return pl.pallas_call(
        functools.partial(_s5_kernel, lane_chunk=1024), grid=(b, t // rows),
        in_specs=[pl.BlockSpec((1, rows, d), lambda i, j: (i, j, 0))] + [full(a) for a in args],
        out_specs=pl.BlockSpec((1, rows, d), lambda i, j: (i, j, 0)),
        out_shape=jax.ShapeDtypeStruct((b, t, d), F32),
        scratch_shapes=[pltpu.VMEM((rows, n_state), F32), pltpu.VMEM((rows, n_state), F32),
                        pltpu.VMEM((2, SUBLANES, n_state), F32), pltpu.VMEM((SUBLANES, n_state), F32)],
        compiler_params=_cparams("parallel", "arbitrary"), name="s5_mixer",
    )(x, *args)


def _pad_heads(w, n_heads, width):
    d = w.shape[0]
    return jnp.pad(w.reshape(d, n_heads, width), ((0, 0), (0, 0), (0, LANES - width))).reshape(d, n_heads * LANES)


def _nsa_layer(x, g_norm, w_in, w_out, q_gain, k_gain, cmp_pos, cmp_w1, cmp_b1, cmp_w2, cmp_b2):
    b, t, d = x.shape
    gn, rh, dh = N_KV_GROUPS, HEADS_PER_GROUP, HEAD_DIM
    assert t % SEL_CHUNK == 0 and t >= SEL_CHUNK + WINDOW
    x2 = x.reshape(b * t, d)
    scale = LOG2E / math.sqrt(dh)

    kvw = lambda i: w_in[:, d + i * KV_DIM:d + (i + 1) * KV_DIM]
    wq = _pad_heads(w_in[:, :d], N_HEADS, dh).astype(BF16)
    wk = jnp.concatenate([_pad_heads(kvw(i), gn, dh) for i in (2, 3, 4, 5)], axis=1).astype(BF16)
    wc = jnp.concatenate([kvw(0), kvw(1)], axis=1).astype(BF16)
    wg = _pad_heads(w_in[:, d + 6 * KV_DIM:], gn, rh * N_BRANCHES).astype(BF16)
    pad_gain = lambda v, n: jnp.tile(jnp.pad(v, (0, LANES - dh)), n)[None, :]
    qg_p = pad_gain(q_gain * scale, N_HEADS)
    kg_p = jnp.concatenate([pad_gain(k_gain[1], gn), pad_gain(k_gain[2], gn)], axis=0)
    seg = jnp.arange(2 * LANES) // LANES
    ebd = (seg[:, None] == seg[None, :]).astype(BF16)

    q, ka, va, kw, vw, kc_raw, vc_raw, gates = _nsa_proj(x2, b, t, g_norm[None, :], wq, wk, wc, wg, qg_p, kg_p, ebd)

    tc = t // CMP_STRIDE
    half = CMP_STRIDE * dh
    blk = lambda a: a.reshape(b, tc, CMP_STRIDE, gn, dh).transpose(0, 3, 1, 2, 4).reshape(b, gn, tc, half)
    comp = []
    for i, (raw, norm) in enumerate(((kc_raw, True), (vc_raw, False))):
        w1 = cmp_w1[i].astype(BF16)
        comp.append(_compress(blk(raw), cmp_pos[i].reshape(2, half), w1[:half], w1[half:], cmp_b1[i][None, :],
                              jnp.pad(cmp_w2[i], ((0, 0), (0, LANES - dh))).astype(BF16),
                              jnp.pad(cmp_b2[i], (0, LANES - dh))[None, :],
                              jnp.pad(k_gain[0], (0, LANES - dh))[None, :], norm))
    kc, vc = comp
    vct = vc.transpose(0, 1, 3, 2)

    nsb = t // SEL_BLOCK
    n_i = jnp.arange(tc)
    j_i = jnp.arange(nsb)
    n_cmp = (t - CMP_BLOCK) // CMP_STRIDE + 1
    ovt = ((n_i[None, :] * CMP_STRIDE < (j_i[:, None] + 1) * SEL_BLOCK)
           & (n_i[None, :] * CMP_STRIDE + CMP_BLOCK > j_i[:, None] * SEL_BLOCK)
           & (n_i[None, :] < n_cmp)).astype(BF16)
    ocmp, selt = _cmp_select(q, kc, vct, ovt)

    vrows = dh + 2 * SUBLANES
    vat = va[..., :vrows].transpose(0, 1, 3, 2)
    vwt = vw[..., :vrows].transpose(0, 1, 3, 2)
    o = _sel_win(q, ka, vat, kw, vwt, selt, ocmp, gates)
    return o.reshape(b * t, d), w_out.astype(BF16)


def kernel(x, mix_norm, ffn_norm, nsa_w_in, nsa_w_out, nsa_q_gain, nsa_k_gain, nsa_cmp_pos, nsa_cmp_w1, nsa_cmp_b1, nsa_cmp_w2, nsa_cmp_b2, s5_w_in, s5_b_re, s5_b_im, s5_c_re, s5_c_im, s5_d, s5_log_dt, s5_a_re, s5_a_im, s5_w_glu, ffn_w_gate, ffn_w_up, ffn_w_down):
    b, t, d = x.shape
    depth = mix_norm.shape[0]
    for layer in range(depth):
        i = layer // 2
        attn = None
        if layer % 2 == 0:
            attn = _nsa_layer(x, mix_norm[layer], nsa_w_in[i], nsa_w_out[i], nsa_q_gain[i], nsa_k_gain[i],
                              nsa_cmp_pos[i], nsa_cmp_w1[i], nsa_cmp_b1[i], nsa_cmp_w2[i], nsa_cmp_b2[i])
        else:
            x = _s5_layer(x, mix_norm[layer], s5_w_in[i], s5_b_re[i], s5_b_im[i], s5_c_re[i], s5_c_im[i],
                          s5_d[i], s5_log_dt[i], s5_a_re[i], s5_a_im[i], s5_w_glu[i])
        x = _ffn(x.reshape(b * t, d), ffn_norm[layer][None, :], ffn_w_gate[layer].astype(BF16),
                 ffn_w_up[layer].astype(BF16), ffn_w_down[layer].astype(BF16), attn=attn).reshape(b, t, d)
    return x
```

```python
import functools
import math

import jax
import jax.numpy as jnp
from jax import lax
from jax.experimental import pallas as pl
from jax.experimental.pallas import tpu as pltpu

F32 = jnp.float32
BF16 = jnp.bfloat16

EPS = 1e-6
NEG = -1e30
MASK_BIAS = -1e9
LOG2E = 1.4426950408889634

N_HEADS = 16
HEAD_DIM = 64
N_KV_GROUPS = 4
HEADS_PER_GROUP = 4
KV_DIM = N_KV_GROUPS * HEAD_DIM
CMP_BLOCK = 32
CMP_STRIDE = 16
SEL_BLOCK = 64
SEL_TOPN = 16
N_FORCED = 3
WINDOW = 512
N_BRANCHES = 3
S5_GROUP = 16
S5_STATE = 64

LANES = 128
SUBLANES = 8
MXU_DIM = 256
SEL_CHUNK = 512
V7X_VMEM_LIMIT_BYTES = 56 * 1024 * 1024


def _cparams(*semantics):
    return pltpu.CompilerParams(dimension_semantics=semantics,
                                vmem_limit_bytes=V7X_VMEM_LIMIT_BYTES)


def _rms_rows(x, g):
    var = jnp.mean(x * x, axis=-1, keepdims=True)
    return x * lax.rsqrt(var + EPS) * g


def _dot(a, b):
    return jnp.dot(a, b, preferred_element_type=F32)


def _dot_nt(a, b):
    return lax.dot_general(a, b, (((1,), (1,)), ((), ())), preferred_element_type=F32)


def _nsa_proj_kernel(x_ref, g_ref, wq_ref, wk_ref, wc_ref, wg_ref, qg_ref, kg_ref,
                     q_ref, ka_ref, va_ref, kw_ref, vw_ref, kcr_ref, vcr_ref, gate_ref):
    tm = x_ref.shape[0]
    h = _rms_rows(x_ref[...], g_ref[...]).astype(BF16)
    inv_dh = 1.0 / HEAD_DIM

    def head_norm(v, gain):
        var = jnp.sum(v * v, axis=-1, keepdims=True) * inv_dh
        return v * lax.rsqrt(var + EPS) * gain

    q = _dot(h, wq_ref[...])
    qg = qg_ref[...]
    for hd in range(N_HEADS):
        sl = slice(hd * LANES, (hd + 1) * LANES)
        g, r = divmod(hd, HEADS_PER_GROUP)
        q_ref[0, g, r] = head_norm(q[:, sl], qg[:, sl]).astype(BF16)

    row = lax.rem(pl.program_id(0) * tm, SEL_CHUNK) + lax.broadcasted_iota(jnp.int32, (tm, 1), 0)
    lane = lax.broadcasted_iota(jnp.int32, (1, LANES), 1)
    blk_flag = ((row // SEL_BLOCK) == (lane - HEAD_DIM)).astype(F32)
    one_flag = (lane == HEAD_DIM).astype(F32)

    kv = _dot(h, wk_ref[...])
    kg = kg_ref[...]
    plan = ((ka_ref, 0, blk_flag), (va_ref, None, one_flag), (kw_ref, 1, None), (vw_ref, None, one_flag))
    for part, (ref, gain_row, flag) in enumerate(plan):
        for g in range(N_KV_GROUPS):
            piece = kv[:, (part * N_KV_GROUPS + g) * LANES:(part * N_KV_GROUPS + g + 1) * LANES]
            if gain_row is not None:
                piece = head_norm(piece, kg[gain_row:gain_row + 1, g * LANES:(g + 1) * LANES])
            if flag is not None:
                piece = piece + flag
            ref[0, g] = piece.astype(BF16)

    c = _dot(h, wc_ref[...])
    kcr_ref[...] = c[:, :KV_DIM]
    vcr_ref[...] = c[:, KV_DIM:]
    gt = jax.nn.sigmoid(_dot(h, wg_ref[...]))
    for g in range(N_KV_GROUPS):
        gate_ref[0, g] = gt[:, g * LANES:(g + 1) * LANES]


def _nsa_proj(x2, b, t, g, wq, wk, wc, wg, qg_p, kg_p, tm=512):
    m, d = x2.shape
    assert SEL_CHUNK % tm == 0 and t % SEL_CHUNK == 0
    nt = t // tm
    gn, rh = N_KV_GROUPS, HEADS_PER_GROUP
    full = lambda a: pl.BlockSpec(a.shape, lambda i: (0,) * a.ndim)
    row = lambda n: pl.BlockSpec((tm, n), lambda i: (i, 0))
    head = pl.BlockSpec((1, gn, tm, LANES), lambda i: (i // nt, 0, i % nt, 0))
    outs = [jax.ShapeDtypeStruct((b, gn, rh, t, LANES), BF16)]
    outs += [jax.ShapeDtypeStruct((b, gn, t, LANES), BF16)] * 4
    outs += [jax.ShapeDtypeStruct((m, KV_DIM), F32)] * 2
    outs += [jax.ShapeDtypeStruct((b, gn, t, LANES), F32)]
    return pl.pallas_call(
        _nsa_proj_kernel, grid=(m // tm,),
        in_specs=[row(d), full(g), full(wq), full(wk), full(wc), full(wg), full(qg_p), full(kg_p)],
        out_specs=[pl.BlockSpec((1, gn, rh, tm, LANES), lambda i: (i // nt, 0, 0, i % nt, 0)),
                   head, head, head, head, row(KV_DIM), row(KV_DIM), head],
        out_shape=outs, compiler_params=_cparams("parallel"), name="nsa_proj",
    )(x2, g, wq, wk, wc, wg, qg_p, kg_p)


def _compress_kernel(a_ref, pos_ref, w1a_ref, w1b_ref, b1_ref, w2_ref, b2_ref, gain_ref, o_ref, *, norm):
    a = a_ref[0, 0]
    tc = a.shape[0]
    pos = pos_ref[...]
    h1 = _dot((a + pos[0:1]).astype(BF16), w1a_ref[...])
    h2 = _dot((a + pos[1:2]).astype(BF16), w1b_ref[...])
    h = h1 + pltpu.roll(h2, tc - 1, 0) + b1_ref[...]
    hid = jax.nn.gelu(h).astype(BF16)
    out = _dot(hid, w2_ref[...]) + b2_ref[...]
    if norm:
        var = jnp.sum(out * out, axis=-1, keepdims=True) * (1.0 / HEAD_DIM)
        out = out * lax.rsqrt(var + EPS) * gain_ref[...]
    o_ref[0, 0] = out.astype(o_ref.dtype)


def _compress(a, pos2, w1a, w1b, b1, w2, b2, gain, norm):
    b, g, tc, w = a.shape
    full = lambda z: pl.BlockSpec(z.shape, lambda i, j: (0,) * z.ndim)
    return pl.pallas_call(
        functools.partial(_compress_kernel, norm=norm), grid=(b, g),
        in_specs=[pl.BlockSpec((1, 1, tc, w), lambda i, j: (i, j, 0, 0)),
                  full(pos2), full(w1a), full(w1b), full(b1), full(w2), full(b2), full(gain)],
        out_specs=pl.BlockSpec((1, 1, tc, LANES), lambda i, j: (i, j, 0, 0)),
        out_shape=jax.ShapeDtypeStruct((b, g, tc, LANES), BF16),
        compiler_params=_cparams("parallel", "parallel"), name="nsa_compress",
    )(a, pos2, w1a, w1b, b1, w2, b2, gain)


def _cmp_select_kernel(q_ref, kc_ref, vct_ref, ovt_ref, ocmp_ref, sel_ref, bias_scr, *, top_n, n_tiers):
    tq = q_ref.shape[3]
    tc = kc_ref.shape[2]
    nsb = ovt_ref.shape[0]
    n_tiles = pl.num_programs(1)
    t = pl.program_id(1) * tq + lax.broadcasted_iota(jnp.int32, (1, tq), 1)
    cur = t // SEL_BLOCK
    any_vis = t >= CMP_BLOCK - 1
    n_rest = max(top_n - N_FORCED, 0)

    def tile_body(rows, blocks):
        n_idx = lax.broadcasted_iota(jnp.int32, (rows, 1), 0)
        bias = jnp.where((n_idx * CMP_STRIDE + (CMP_BLOCK - 1)) <= t, 0.0, NEG)
        j_idx = lax.broadcasted_iota(jnp.int32, (blocks, 1), 0)
        forced = (j_idx == 0) | (j_idx == cur) | (j_idx == cur - 1)
        valid = j_idx <= cur
        j_f = j_idx.astype(F32)
        bias_scr[:rows, :] = bias

        def group_body(g, carry):
            kc = kc_ref[0, g, :rows, :]
            vct = vct_ref[0, g, :, :rows]
            psum = jnp.zeros((rows, tq), F32)
            for r in range(HEADS_PER_GROUP):
                qr = q_ref[0, g, r]
                s = jnp.concatenate([_dot_nt(kc[:rows // 2], qr), _dot_nt(kc[rows // 2:], qr)], axis=0)
                s = s + bias_scr[:rows, :]
                e = jnp.exp2(s - jnp.max(s, axis=0, keepdims=True))
                den = jnp.sum(e, axis=0, keepdims=True)
                p = e * jnp.where(any_vis, 1.0 / den, 0.0)
                pb = p.astype(BF16)
                o_t = (_dot(vct[:, :rows // 2], pb[:rows // 2])
                       + _dot(vct[:, rows // 2:], pb[rows // 2:]))
                ocmp_ref[0, g, r] = o_t[:HEAD_DIM]
                psum = psum + p
            hi = psum.astype(BF16)
            lo = (psum - hi.astype(F32)).astype(BF16)
            ovt = ovt_ref[:blocks, :rows]
            psel = _dot(ovt, hi) + _dot(ovt, lo)
            score = jnp.where(valid & ~forced, psel, NEG)
            for _ in range(n_rest):
                mx = jnp.max(score, axis=0, keepdims=True)
                first = jnp.min(jnp.where(score == mx, j_f, float(nsb)), axis=0, keepdims=True)
                score = jnp.where(j_f == first, -jnp.inf, score)
            sel_ref[0, g, :blocks, :] = jnp.where(forced | ((score == -jnp.inf) & valid), 1.0, 0.0)
            if blocks < nsb:
                sel_ref[0, g, blocks:, :] = jnp.zeros((nsb - blocks, tq), F32)
            return carry

        lax.fori_loop(0, N_KV_GROUPS, group_body, 0)

    tier = (pl.program_id(1) * n_tiers) // n_tiles
    for k in range(n_tiers):
        pl.when(tier == k)(functools.partial(tile_body, tc * (k + 1) // n_tiers, nsb * (k + 1) // n_tiers))


def _cmp_select(q, kc, vct, ovt, tq=256):
    b, g, r, t_len, _ = q.shape
    tc = kc.shape[2]
    nsb = ovt.shape[0]
    tq = min(tq, t_len)
    top_n = min(SEL_TOPN, nsb)
    assert top_n >= N_FORCED
    n_tiers = 4
    assert (t_len // tq) % n_tiers == 0 and nsb % (n_tiers * SUBLANES) == 0 and tc % (n_tiers * LANES // 4) == 0
    return pl.pallas_call(
        functools.partial(_cmp_select_kernel, top_n=top_n, n_tiers=n_tiers), grid=(b, t_len // tq),
        in_specs=[pl.BlockSpec((1, g, r, tq, LANES), lambda i, j: (i, 0, 0, j, 0)),
                  pl.BlockSpec((1, g, tc, LANES), lambda i, j: (i, 0, 0, 0)),
                  pl.BlockSpec((1, g, LANES, tc), lambda i, j: (i, 0, 0, 0)),
                  pl.BlockSpec(ovt.shape, lambda i, j: (0, 0))],
        out_specs=[pl.BlockSpec((1, g, r, HEAD_DIM, tq), lambda i, j: (i, 0, 0, 0, j)),
                   pl.BlockSpec((1, g, nsb, tq), lambda i, j: (i, 0, 0, j))],
        out_shape=[jax.ShapeDtypeStruct((b, g, r, HEAD_DIM, t_len), F32),
                   jax.ShapeDtypeStruct((b, g, nsb, t_len), F32)],
        scratch_shapes=[pltpu.VMEM((tc, tq), F32)],
        compiler_params=_cparams("parallel", "parallel"), name="nsa_cmp_select",
    )(q, kc, vct, ovt)


def _sel_win_kernel(q_ref, ka_ref, vat_ref, kw_ref, vwt_ref, selt_ref, ocmp_ref, gate_ref, o_ref,
                    qa_scr, sa_scr, sb_scr, ma_scr, mb_scr, *, kc):
    r_heads, tq = q_ref.shape[2], q_ref.shape[3]
    dh = HEAD_DIM
    cols = r_heads * tq
    t_len = ka_ref.shape[2]
    per_chunk = kc // SEL_BLOCK
    aug_rows = -(-per_chunk // (2 * SUBLANES)) * (2 * SUBLANES)
    t0 = pl.program_id(2) * tq
    t_lane = t0 + lax.broadcasted_iota(jnp.int32, (1, tq), 1)

    for r in range(r_heads):
        qa_scr[:, r * tq:(r + 1) * tq] = q_ref[0, 0, r].astype(F32).T.astype(BF16)

    def softmax_t(s, m_new):
        return jnp.exp2((s - m_new).astype(BF16))

    span = min(tq + WINDOW, t_len)
    w0 = pl.multiple_of(jnp.maximum(t0 + tq - span, 0), tq)
    kpos_w = w0 + lax.broadcasted_iota(jnp.int32, (span, 1), 0)
    in_win = (kpos_w <= t_lane) & (t_lane - kpos_w < WINDOW)
    bias_w = jnp.where(in_win, 0.0, MASK_BIAS)
    s = _dot(kw_ref[0, 0, pl.ds(w0, span), :], qa_scr[...]) + jnp.concatenate([bias_w] * r_heads, axis=1)
    ow = _dot(vwt_ref[0, 0, :, pl.ds(w0, span)], softmax_t(s, jnp.max(s, axis=0, keepdims=True)))
    o_win_t = ow[:dh] / ow[dh:dh + 1]

    n_full = t0 // kc
    last_full = jnp.maximum(n_full - 1, 0)

    def scores(c):
        ci = jnp.minimum(c, n_full)
        sel8 = selt_ref[0, 0, pl.ds(pl.multiple_of(ci * per_chunk, per_chunk), per_chunk), :]
        sb = (jnp.where(c <= n_full, sel8, 0.0) - 1.0) * (-MASK_BIAS)
        if aug_rows > per_chunk:
            sb = jnp.concatenate([sb, jnp.zeros((aug_rows - per_chunk, tq), F32)], axis=0)
        qa_scr[dh:dh + aug_rows, :] = jnp.concatenate([sb.astype(BF16)] * r_heads, axis=1)
        return _dot(ka_ref[0, 0, pl.ds(pl.multiple_of(ci * kc, kc), kc), :], qa_scr[...])

    def update(s, s_max, c, m, acc):
        ci = jnp.minimum(c, n_full)
        m_new = jnp.maximum(m, s_max)
        alpha = jnp.exp2(m - m_new)
        pv = _dot(vat_ref[0, 0, :, pl.ds(pl.multiple_of(ci * kc, kc), kc)], softmax_t(s, m_new))
        return m_new, alpha * acc + pv

    kpos = n_full * kc + lax.broadcasted_iota(jnp.int32, (kc, 1), 0)
    causal = jnp.where(kpos <= t_lane, 0.0, MASK_BIAS)
    s_diag = scores(n_full) + jnp.concatenate([causal] * r_heads, axis=1)
    init = (jnp.full((1, cols), NEG, F32), jnp.zeros((vat_ref.shape[2], cols), F32))
    carry = update(s_diag, jnp.max(s_diag, axis=0, keepdims=True), n_full, *init)

    def stash(c, s_scr, m_scr):
        c = jnp.where(c < n_full, c, n_full + 1)
        ci = jnp.minimum(c, n_full)
        sel8 = selt_ref[0, 0, pl.ds(pl.multiple_of(ci * per_chunk, per_chunk), per_chunk), :]
        sb = (jnp.where(c <= n_full, sel8, 0.0) - 1.0) * (-MASK_BIAS)
        if aug_rows > per_chunk:
            sb = jnp.concatenate([sb, jnp.zeros((aug_rows - per_chunk, tq), F32)], axis=0)
        qa_scr[dh:dh + aug_rows, :] = jnp.concatenate([sb.astype(BF16)] * r_heads, axis=1)
        k_rows = ka_ref[0, 0, pl.ds(pl.multiple_of(ci * kc, kc), kc), :]
        for n in range(r_heads):
            cs = slice(n * tq, (n + 1) * tq)
            s = _dot(k_rows, qa_scr[:, cs])
            s_scr[:, cs] = s
            m_scr[0:1, cs] = jnp.max(s, axis=0, keepdims=True)

    def update_scr(s_scr, m_scr, c, m, acc):
        ci = jnp.minimum(c, n_full)
        v_cols = vat_ref[0, 0, :, pl.ds(pl.multiple_of(ci * kc, kc), kc)]
        m_new = jnp.maximum(m, m_scr[0:1, :])
        alpha = jnp.exp2(m - m_new)
        pvs = []
        for n in range(r_heads):
            cs = slice(n * tq, (n + 1) * tq)
            pvs.append(_dot(v_cols, softmax_t(s_scr[:, cs], m_new[:, cs])))
        return m_new, alpha * acc + jnp.concatenate(pvs, axis=1)

    stash(0, sa_scr, ma_scr)

    def body(i, carry):
        c = 2 * i
        stash(c + 1, sb_scr, mb_scr)
        carry = update_scr(sa_scr, ma_scr, jnp.minimum(c, last_full), *carry)
        stash(c + 2, sa_scr, ma_scr)
        return update_scr(sb_scr, mb_scr, jnp.minimum(c + 1, last_full), *carry)

    carry = lax.fori_loop(0, n_full // 2, body, carry)
    _, acc = lax.cond(n_full % 2 == 1,
                      lambda cr: update_scr(sa_scr, ma_scr, last_full, *cr), lambda cr: cr, carry)
    o_sel_t = acc[:dh] / acc[dh:dh + 1]

    gates_t = gate_ref[0, 0].T
    parts = []
    for r in range(r_heads):
        cs = slice(r * tq, (r + 1) * tq)
        parts.append(gates_t[3 * r:3 * r + 1] * ocmp_ref[0, 0, r]
                     + gates_t[3 * r + 1:3 * r + 2] * o_sel_t[:, cs]
                     + gates_t[3 * r + 2:3 * r + 3] * o_win_t[:, cs])
    o_ref[0] = jnp.concatenate(parts, axis=0).T.astype(o_ref.dtype)


def _sel_win(q, ka, vat, kw, vwt, selt, ocmp, gates, tq=256, kc=SEL_CHUNK):
    b, g, r, t_len, _ = q.shape
    nsb = selt.shape[2]
    vrows = vat.shape[2]
    gw = r * HEAD_DIM
    k_spec = pl.BlockSpec((1, 1, t_len, LANES), lambda i, j, k: (i, j, 0, 0))
    vt_spec = pl.BlockSpec((1, 1, vrows, t_len), lambda i, j, k: (i, j, 0, 0))
    return pl.pallas_call(
        functools.partial(_sel_win_kernel, kc=kc), grid=(b, g, t_len // tq),
        in_specs=[pl.BlockSpec((1, 1, r, tq, LANES), lambda i, j, k: (i, j, 0, k, 0)),
                  k_spec, vt_spec, k_spec, vt_spec,
                  pl.BlockSpec((1, 1, nsb, tq), lambda i, j, k: (i, j, 0, k)),
                  pl.BlockSpec((1, 1, r, HEAD_DIM, tq), lambda i, j, k: (i, j, 0, 0, k)),
                  pl.BlockSpec((1, 1, tq, LANES), lambda i, j, k: (i, j, k, 0))],
        out_specs=pl.BlockSpec((1, tq, gw), lambda i, j, k: (i, k, j)),
        out_shape=jax.ShapeDtypeStruct((b, t_len, g * gw), BF16),
        scratch_shapes=[pltpu.VMEM((LANES, r * tq), BF16),
                        pltpu.VMEM((kc, r * tq), F32), pltpu.VMEM((kc, r * tq), F32),
                        pltpu.VMEM((SUBLANES, r * tq), F32), pltpu.VMEM((SUBLANES, r * tq), F32)],
        compiler_params=_cparams("parallel", "parallel", "arbitrary"), name="nsa_sel_win",
    )(q, ka, vat, kw, vwt, selt, ocmp, gates)


def _ffn_tile(x, g_ref, wg_ref, wu_ref, wd_ref, hc):
    h = _rms_rows(x, g_ref[...]).astype(BF16)
    acc = x
    for c in range(wg_ref.shape[1] // hc):
        sl = slice(c * hc, (c + 1) * hc)
        a = jax.nn.silu(_dot(h, wg_ref[:, sl])) * _dot(h, wu_ref[:, sl])
        acc = acc + _dot(a.astype(BF16), wd_ref[sl, :])
    return acc


def _ffn_kernel(x_ref, g_ref, wg_ref, wu_ref, wd_ref, o_ref, *, hc):
    o_ref[...] = _ffn_tile(x_ref[...], g_ref, wg_ref, wu_ref, wd_ref, hc)


def _attn_out_ffn_kernel(a_ref, wo_ref, x_ref, g_ref, wg_ref, wu_ref, wd_ref, o_ref, *, hc):
    x = x_ref[...] + _dot(a_ref[...], wo_ref[...])
    o_ref[...] = _ffn_tile(x, g_ref, wg_ref, wu_ref, wd_ref, hc)


def _ffn(x2, g, wg, wu, wd, attn=None, tm=512, hc=MXU_DIM):
    m, d = x2.shape
    full = lambda a: pl.BlockSpec(a.shape, lambda i: (0,) * a.ndim)
    row = pl.BlockSpec((tm, d), lambda i: (i, 0))
    if attn is None:
        body, pre, pre_specs = _ffn_kernel, (), []
    else:
        body, pre = _attn_out_ffn_kernel, attn
        pre_specs = [pl.BlockSpec((tm, attn[0].shape[1]), lambda i: (i, 0)), full(attn[1])]
    return pl.pallas_call(
        functools.partial(body, hc=hc), grid=(m // tm,),
        in_specs=pre_specs + [row, full(g), full(wg), full(wu), full(wd)],
        out_specs=row, out_shape=jax.ShapeDtypeStruct((m, d), F32),
        compiler_params=_cparams("parallel"), name="ffn_swiglu",
    )(*pre, x2, g, wg, wu, wd)


def _s5_kernel(x_ref, g_ref, win_ref, pm_ref, pmt_ref, bre_ref, bim_ref, a1_ref, aseg_ref, pw_ref, cre_ref, cim_ref,
               d_ref, wglu_ref, o_ref, xr_scr, xi_scr, cin_scr, car_scr, *, lane_chunk):
    rows, d = x_ref.shape[1], x_ref.shape[2]
    n_state = xr_scr.shape[1]
    n_kt = bre_ref.shape[0]
    kw = bre_ref.shape[1]
    sw = bre_ref.shape[2]
    n_seg = SUBLANES
    seg = rows // n_seg

    @pl.when(pl.program_id(1) == 0)
    def _():
        car_scr[...] = jnp.zeros_like(car_scr)

    x = x_ref[0]
    h = _dot(pm_ref[...], _rms_rows(x, g_ref[...]).astype(BF16)).astype(BF16)
    u = _dot(h, win_ref[...])
    ub = u.astype(BF16)
    for kt in range(n_kt):
        uk = ub[:, kt * kw:(kt + 1) * kw]
        xr_scr[:, kt * sw:(kt + 1) * sw] = _dot(uk, bre_ref[kt])
        xi_scr[:, kt * sw:(kt + 1) * sw] = _dot(uk, bim_ref[kt])

    for lc in range(n_state // lane_chunk):
        ls = slice(lc * lane_chunk, (lc + 1) * lane_chunk)
        ar, ai = a1_ref[0, :, ls], a1_ref[1, :, ls]

        def step(r, carry, ls=ls, ar=ar, ai=ai):
            pr, pi = carry
            r0 = pl.multiple_of(r * SUBLANES, SUBLANES)
            nr = ar * pr - ai * pi + xr_scr[pl.ds(r0, SUBLANES), ls]
            ni = ar * pi + ai * pr + xi_scr[pl.ds(r0, SUBLANES), ls]
            xr_scr[pl.ds(r0, SUBLANES), ls] = nr
            xi_scr[pl.ds(r0, SUBLANES), ls] = ni
            return nr, ni

        zero = jnp.zeros((SUBLANES, lane_chunk), F32)
        lr, li = lax.fori_loop(0, seg, step, (zero, zero), unroll=4)

        sr, si = aseg_ref[0:1, ls], aseg_ref[1:2, ls]
        cr, ci = car_scr[0:1, ls], car_scr[1:2, ls]
        for s in range(n_seg):
            cin_scr[0, s:s + 1, ls] = cr
            cin_scr[1, s:s + 1, ls] = ci
            cr, ci = (sr * cr - si * ci + lr[s:s + 1], sr * ci + si * cr + li[s:s + 1])
        car_scr[0:1, ls] = cr
        car_scr[1:2, ls] = ci

        def fix(r, carry, ls=ls):
            r0 = pl.multiple_of(r * SUBLANES, SUBLANES)
            pr, pi = pw_ref[0, pl.ds(r, 1), ls], pw_ref[1, pl.ds(r, 1), ls]
            cr, ci = cin_scr[0, :, ls], cin_scr[1, :, ls]
            xr_scr[pl.ds(r0, SUBLANES), ls] += pr * cr - pi * ci
            xi_scr[pl.ds(r0, SUBLANES), ls] += pr * ci + pi * cr
            return carry

        lax.fori_loop(0, seg, fix, 0, unroll=4)

    ys = []
    for kt in range(n_kt):
        ss = slice(kt * sw, (kt + 1) * sw)
        ys.append(_dot(xr_scr[:, ss].astype(BF16), cre_ref[kt]) + _dot(xi_scr[:, ss].astype(BF16), cim_ref[kt]))
    y = jnp.concatenate(ys, axis=1) + d_ref[...] * u
    z = _dot(pmt_ref[...], jax.nn.gelu(y).astype(BF16)).astype(BF16)
    vg = _dot(z, wglu_ref[...])
    o_ref[0] = x + vg[:, :d] * jax.nn.sigmoid(vg[:, d:])


def _s5_tables(b_re, b_im, c_re, c_im, log_dt, a_re, a_im, seg):
    ng, ps, cg = b_re.shape
    per_tile = MXU_DIM // cg
    n_kt = ng // per_tile
    dt = jnp.exp(log_dt)[:, None]
    lam_r, lam_i = dt * a_re, dt * a_im
    mag = jnp.exp(lam_r)
    abar_r, abar_i = mag * jnp.cos(lam_i), mag * jnp.sin(lam_i)
    den = a_re * a_re + a_im * a_im
    coef_r = ((abar_r - 1.0) * a_re + abar_i * a_im) / den
    coef_i = (abar_i * a_re - (abar_r - 1.0) * a_im) / den
    bbar_r = coef_r[..., None] * b_re - coef_i[..., None] * b_im
    bbar_i = coef_r[..., None] * b_im + coef_i[..., None] * b_re
    eye = jnp.eye(per_tile, dtype=F32)
    bd_in = lambda m: jnp.einsum('kgpc,gh->kgchp', m.reshape(n_kt, per_tile, ps, cg), eye).reshape(
        n_kt, per_tile * cg, per_tile * ps).astype(BF16)
    bd_out = lambda m: jnp.einsum('kgcp,gh->kgphc', m.reshape(n_kt, per_tile, cg, ps), eye).reshape(
        n_kt, per_tile * ps, per_tile * cg).astype(BF16)

    def power(n):
        pm = jnp.exp(n * lam_r)
        return (pm * jnp.cos(n * lam_i)).reshape(-1), (pm * jnp.sin(n * lam_i)).reshape(-1)

    a1 = jnp.tile(jnp.stack(power(1.0))[:, None, :], (1, SUBLANES, 1))
    aseg = jnp.stack(power(float(seg)))
    steps = [power(float(i + 1)) for i in range(seg)]
    pw = jnp.stack([jnp.stack([p[0] for p in steps]), jnp.stack([p[1] for p in steps])])
    return bd_in(bbar_r), bd_in(bbar_i), a1, aseg, pw, bd_out(c_re), bd_out(-c_im)


def _s5_layer(x, g_norm, w_in, b_re, b_im, c_re, c_im, d_skip, log_dt, a_re, a_im, w_glu, rows=256):
    b, t, d = x.shape
    rows = min(rows, t)
    bre, bim, a1, aseg, pw, cre, cim = _s5_tables(b_re, b_im, c_re, c_im, log_dt, a_re, a_im, rows // SUBLANES)
    n_state = a1.shape[-1]
    full = lambda a: pl.BlockSpec(a.shape, lambda i, j: (0,) * a.ndim)
    seg = rows // SUBLANES
    new_row = jnp.arange(rows)
    pm = (jnp.arange(rows)[None, :] == (new_row % SUBLANES * seg + new_row // SUBLANES)[:, None]).astype(BF16)
    args = (g_norm[None, :], w_in.astype(BF16), pm, pm.T, bre, bim, a1, aseg, pw, cre, cim, d_skip[None, :],
            w_glu.astype(BF16))
    return pl.pallas_call(
        functools.partial(_s5_kernel, lane_chunk=1024), grid=(b, t // rows),
        in_specs=[pl.BlockSpec((1, rows, d), lambda i, j: (i, j, 0))] + [full(a) for a in args],
        out_specs=pl.BlockSpec((1, rows, d), lambda i, j: (i, j, 0)),
        out_shape=jax.ShapeDtypeStruct((b, t, d), F32),
        scratch_shapes=[pltpu.VMEM((rows, n_state), F32), pltpu.VMEM((rows, n_state), F32),
                        pltpu.VMEM((2, SUBLANES, n_state), F32), pltpu.VMEM((SUBLANES, n_state), F32)],
        compiler_params=_cparams("parallel", "arbitrary"), name="s5_mixer",
    )(x, *args)


def _pad_heads(w, n_heads, width):
    d = w.shape[0]
    return jnp.pad(w.reshape(d, n_heads, width), ((0, 0), (0, 0), (0, LANES - width))).reshape(d, n_heads * LANES)


def _nsa_layer(x, g_norm, w_in, w_out, q_gain, k_gain, cmp_pos, cmp_w1, cmp_b1, cmp_w2, cmp_b2):
    b, t, d = x.shape
    gn, rh, dh = N_KV_GROUPS, HEADS_PER_GROUP, HEAD_DIM
    assert t % SEL_CHUNK == 0 and t >= SEL_CHUNK + WINDOW
    x2 = x.reshape(b * t, d)
    scale = LOG2E / math.sqrt(dh)

    kvw = lambda i: w_in[:, d + i * KV_DIM:d + (i + 1) * KV_DIM]
    wq = _pad_heads(w_in[:, :d], N_HEADS, dh).astype(BF16)
    wk = jnp.concatenate([_pad_heads(kvw(i), gn, dh) for i in (2, 3, 4, 5)], axis=1).astype(BF16)
    wc = jnp.concatenate([kvw(0), kvw(1)], axis=1).astype(BF16)
    wg = _pad_heads(w_in[:, d + 6 * KV_DIM:], gn, rh * N_BRANCHES).astype(BF16)
    pad_gain = lambda v, n: jnp.tile(jnp.pad(v, (0, LANES - dh)), n)[None, :]
    qg_p = pad_gain(q_gain * scale, N_HEADS)
    kg_p = jnp.concatenate([pad_gain(k_gain[1], gn), pad_gain(k_gain[2], gn)], axis=0)

    q, ka, va, kw, vw, kc_raw, vc_raw, gates = _nsa_proj(x2, b, t, g_norm[None, :], wq, wk, wc, wg, qg_p, kg_p)

    tc = t // CMP_STRIDE
    half = CMP_STRIDE * dh
    blk = lambda a: a.reshape(b, tc, CMP_STRIDE, gn, dh).transpose(0, 3, 1, 2, 4).reshape(b, gn, tc, half)
    comp = []
    for i, (raw, norm) in enumerate(((kc_raw, True), (vc_raw, False))):
        w1 = cmp_w1[i].astype(BF16)
        comp.append(_compress(blk(raw), cmp_pos[i].reshape(2, half), w1[:half], w1[half:], cmp_b1[i][None, :],
                              jnp.pad(cmp_w2[i], ((0, 0), (0, LANES - dh))).astype(BF16),
                              jnp.pad(cmp_b2[i], (0, LANES - dh))[None, :],
                              jnp.pad(k_gain[0], (0, LANES - dh))[None, :], norm))
    kc, vc = comp
    vct = vc.transpose(0, 1, 3, 2)

    nsb = t // SEL_BLOCK
    n_i = jnp.arange(tc)
    j_i = jnp.arange(nsb)
    n_cmp = (t - CMP_BLOCK) // CMP_STRIDE + 1
    ovt = ((n_i[None, :] * CMP_STRIDE < (j_i[:, None] + 1) * SEL_BLOCK)
           & (n_i[None, :] * CMP_STRIDE + CMP_BLOCK > j_i[:, None] * SEL_BLOCK)
           & (n_i[None, :] < n_cmp)).astype(BF16)
    ocmp, selt = _cmp_select(q, kc, vct, ovt)

    vrows = dh + 2 * SUBLANES
    vat = va[..., :vrows].transpose(0, 1, 3, 2)
    vwt = vw[..., :vrows].transpose(0, 1, 3, 2)
    o = _sel_win(q, ka, vat, kw, vwt, selt, ocmp, gates)
    return o.reshape(b * t, d), w_out.astype(BF16)


def kernel(x, mix_norm, ffn_norm, nsa_w_in, nsa_w_out, nsa_q_gain, nsa_k_gain, nsa_cmp_pos, nsa_cmp_w1, nsa_cmp_b1, nsa_cmp_w2, nsa_cmp_b2, s5_w_in, s5_b_re, s5_b_im, s5_c_re, s5_c_im, s5_d, s5_log_dt, s5_a_re, s5_a_im, s5_w_glu, ffn_w_gate, ffn_w_up, ffn_w_down):
    b, t, d = x.shape
    depth = mix_norm.shape[0]
    for layer in range(depth):
        i = layer // 2
        attn = None
        if layer % 2 == 0:
            attn = _nsa_layer(x, mix_norm[layer], nsa_w_in[i], nsa_w_out[i], nsa_q_gain[i], nsa_k_gain[i],
                              nsa_cmp_pos[i], nsa_cmp_w1[i], nsa_cmp_b1[i], nsa_cmp_w2[i], nsa_cmp_b2[i])
        else:
            x = _s5_layer(x, mix_norm[layer], s5_w_in[i], s5_b_re[i], s5_b_im[i], s5_c_re[i], s5_c_im[i],
                          s5_d[i], s5_log_dt[i], s5_a_re[i], s5_a_im[i], s5_w_glu[i])
        x = _ffn(x.reshape(b * t, d), ffn_norm[layer][None, :], ffn_w_gate[layer].astype(BF16),
                 ffn_w_up[layer].astype(BF16), ffn_w_down[layer].astype(BF16), attn=attn).reshape(b, t, d)
    return x
```

```python
import functools
import math

import jax
import jax.numpy as jnp
from jax import lax
from jax.experimental import pallas as pl
from jax.experimental.pallas import tpu as pltpu

F32 = jnp.float32
BF16 = jnp.bfloat16

EPS = 1e-6
NEG = -1e30
MASK_BIAS = -1e9
LOG2E = 1.4426950408889634

N_HEADS = 16
HEAD_DIM = 64
N_KV_GROUPS = 4
HEADS_PER_GROUP = 4
KV_DIM = N_KV_GROUPS * HEAD_DIM
CMP_BLOCK = 32
CMP_STRIDE = 16
SEL_BLOCK = 64
SEL_TOPN = 16
N_FORCED = 3
WINDOW = 512
N_BRANCHES = 3
S5_GROUP = 16
S5_STATE = 64

LANES = 128
SUBLANES = 8
MXU_DIM = 256
SEL_CHUNK = 512
V7X_VMEM_LIMIT_BYTES = 56 * 1024 * 1024


def _cparams(*semantics):
    return pltpu.CompilerParams(dimension_semantics=semantics,
                                vmem_limit_bytes=V7X_VMEM_LIMIT_BYTES)


def _rms_rows(x, g):
    var = jnp.mean(x * x, axis=-1, keepdims=True)
    return x * lax.rsqrt(var + EPS) * g


def _dot(a, b):
    return jnp.dot(a, b, preferred_element_type=F32)


def _dot_nt(a, b):
    return lax.dot_general(a, b, (((1,), (1,)), ((), ())), preferred_element_type=F32)


def _nsa_proj_kernel(x_ref, g_ref, wq_ref, wk_ref, wc_ref, wg_ref, qg_ref, kg_ref,
                     q_ref, ka_ref, va_ref, kw_ref, vw_ref, kcr_ref, vcr_ref, gate_ref):
    tm = x_ref.shape[0]
    h = _rms_rows(x_ref[...], g_ref[...]).astype(BF16)
    inv_dh = 1.0 / HEAD_DIM

    def head_norm(v, gain):
        var = jnp.sum(v * v, axis=-1, keepdims=True) * inv_dh
        return v * lax.rsqrt(var + EPS) * gain

    q = _dot(h, wq_ref[...])
    qg = qg_ref[...]
    for hd in range(N_HEADS):
        sl = slice(hd * LANES, (hd + 1) * LANES)
        g, r = divmod(hd, HEADS_PER_GROUP)
        q_ref[0, g, r] = head_norm(q[:, sl], qg[:, sl]).astype(BF16)

    row = lax.rem(pl.program_id(0) * tm, SEL_CHUNK) + lax.broadcasted_iota(jnp.int32, (tm, 1), 0)
    lane = lax.broadcasted_iota(jnp.int32, (1, LANES), 1)
    blk_flag = ((row // SEL_BLOCK) == (lane - HEAD_DIM)).astype(F32)
    one_flag = (lane == HEAD_DIM).astype(F32)

    kv = _dot(h, wk_ref[...])
    kg = kg_ref[...]
    plan = ((ka_ref, 0, blk_flag), (va_ref, None, one_flag), (kw_ref, 1, None), (vw_ref, None, one_flag))
    for part, (ref, gain_row, flag) in enumerate(plan):
        for g in range(N_KV_GROUPS):
            piece = kv[:, (part * N_KV_GROUPS + g) * LANES:(part * N_KV_GROUPS + g + 1) * LANES]
            if gain_row is not None:
                piece = head_norm(piece, kg[gain_row:gain_row + 1, g * LANES:(g + 1) * LANES])
            if flag is not None:
                piece = piece + flag
            ref[0, g] = piece.astype(BF16)

    c = _dot(h, wc_ref[...])
    kcr_ref[...] = c[:, :KV_DIM]
    vcr_ref[...] = c[:, KV_DIM:]
    gt = jax.nn.sigmoid(_dot(h, wg_ref[...]))
    for g in range(N_KV_GROUPS):
        gate_ref[0, g] = gt[:, g * LANES:(g + 1) * LANES]


def _nsa_proj(x2, b, t, g, wq, wk, wc, wg, qg_p, kg_p, tm=512):
    m, d = x2.shape
    assert SEL_CHUNK % tm == 0 and t % SEL_CHUNK == 0
    nt = t // tm
    gn, rh = N_KV_GROUPS, HEADS_PER_GROUP
    full = lambda a: pl.BlockSpec(a.shape, lambda i: (0,) * a.ndim)
    row = lambda n: pl.BlockSpec((tm, n), lambda i: (i, 0))
    head = pl.BlockSpec((1, gn, tm, LANES), lambda i: (i // nt, 0, i % nt, 0))
    outs = [jax.ShapeDtypeStruct((b, gn, rh, t, LANES), BF16)]
    outs += [jax.ShapeDtypeStruct((b, gn, t, LANES), BF16)] * 4
    outs += [jax.ShapeDtypeStruct((m, KV_DIM), F32)] * 2
    outs += [jax.ShapeDtypeStruct((b, gn, t, LANES), F32)]
    return pl.pallas_call(
        _nsa_proj_kernel, grid=(m // tm,),
        in_specs=[row(d), full(g), full(wq), full(wk), full(wc), full(wg), full(qg_p), full(kg_p)],
        out_specs=[pl.BlockSpec((1, gn, rh, tm, LANES), lambda i: (i // nt, 0, 0, i % nt, 0)),
                   head, head, head, head, row(KV_DIM), row(KV_DIM), head],
        out_shape=outs, compiler_params=_cparams("parallel"), name="nsa_proj",
    )(x2, g, wq, wk, wc, wg, qg_p, kg_p)


def _compress_kernel(a_ref, pos_ref, w1a_ref, w1b_ref, b1_ref, w2_ref, b2_ref, gain_ref, o_ref, *, norm):
    a = a_ref[0, 0]
    tc = a.shape[0]
    pos = pos_ref[...]
    h1 = _dot((a + pos[0:1]).astype(BF16), w1a_ref[...])
    h2 = _dot((a + pos[1:2]).astype(BF16), w1b_ref[...])
    h = h1 + pltpu.roll(h2, tc - 1, 0) + b1_ref[...]
    hid = jax.nn.gelu(h).astype(BF16)
    out = _dot(hid, w2_ref[...]) + b2_ref[...]
    if norm:
        var = jnp.sum(out * out, axis=-1, keepdims=True) * (1.0 / HEAD_DIM)
        out = out * lax.rsqrt(var + EPS) * gain_ref[...]
    o_ref[0, 0] = out.astype(o_ref.dtype)


def _compress(a, pos2, w1a, w1b, b1, w2, b2, gain, norm):
    b, g, tc, w = a.shape
    full = lambda z: pl.BlockSpec(z.shape, lambda i, j: (0,) * z.ndim)
    return pl.pallas_call(
        functools.partial(_compress_kernel, norm=norm), grid=(b, g),
        in_specs=[pl.BlockSpec((1, 1, tc, w), lambda i, j: (i, j, 0, 0)),
                  full(pos2), full(w1a), full(w1b), full(b1), full(w2), full(b2), full(gain)],
        out_specs=pl.BlockSpec((1, 1, tc, LANES), lambda i, j: (i, j, 0, 0)),
        out_shape=jax.ShapeDtypeStruct((b, g, tc, LANES), BF16),
        compiler_params=_cparams("parallel", "parallel"), name="nsa_compress",
    )(a, pos2, w1a, w1b, b1, w2, b2, gain)


def _cmp_select_kernel(q_ref, kc_ref, vct_ref, ovt_ref, ocmp_ref, sel_ref, bias_scr, *, top_n, n_tiers):
    tq = q_ref.shape[3]
    tc = kc_ref.shape[2]
    nsb = ovt_ref.shape[0]
    n_tiles = pl.num_programs(1)
    t = pl.program_id(1) * tq + lax.broadcasted_iota(jnp.int32, (1, tq), 1)
    cur = t // SEL_BLOCK
    any_vis = t >= CMP_BLOCK - 1
    n_rest = max(top_n - N_FORCED, 0)

    def tile_body(rows, blocks):
        n_idx = lax.broadcasted_iota(jnp.int32, (rows, 1), 0)
        bias = jnp.where((n_idx * CMP_STRIDE + (CMP_BLOCK - 1)) <= t, 0.0, NEG)
        j_idx = lax.broadcasted_iota(jnp.int32, (blocks, 1), 0)
        forced = (j_idx == 0) | (j_idx == cur) | (j_idx == cur - 1)
        valid = j_idx <= cur
        j_f = j_idx.astype(F32)
        bias_scr[:rows, :] = bias

        def group_body(g, carry):
            kc = kc_ref[0, g, :rows, :]
            vct = vct_ref[0, g, :, :rows]
            psum = jnp.zeros((rows, tq), F32)
            for r in range(HEADS_PER_GROUP):
                qr = q_ref[0, g, r]
                s = jnp.concatenate([_dot_nt(kc[:rows // 2], qr), _dot_nt(kc[rows // 2:], qr)], axis=0)
                s = s + bias_scr[:rows, :]
                e = jnp.exp2(s - jnp.max(s, axis=0, keepdims=True))
                den = jnp.sum(e, axis=0, keepdims=True)
                p = e * jnp.where(any_vis, 1.0 / den, 0.0)
                pb = p.astype(BF16)
                o_t = (_dot(vct[:, :rows // 2], pb[:rows // 2])
                       + _dot(vct[:, rows // 2:], pb[rows // 2:]))
                ocmp_ref[0, g, r] = o_t[:HEAD_DIM]
                psum = psum + p
            hi = psum.astype(BF16)
            lo = (psum - hi.astype(F32)).astype(BF16)
            ovt = ovt_ref[:blocks, :rows]
            psel = _dot(ovt, hi) + _dot(ovt, lo)
            score = jnp.where(valid & ~forced, psel, NEG)
            for _ in range(n_rest):
                mx = jnp.max(score, axis=0, keepdims=True)
                first = jnp.min(jnp.where(score == mx, j_f, float(nsb)), axis=0, keepdims=True)
                score = jnp.where(j_f == first, -jnp.inf, score)
            sel_ref[0, g, :blocks, :] = jnp.where(forced | ((score == -jnp.inf) & valid), 1.0, 0.0)
            if blocks < nsb:
                sel_ref[0, g, blocks:, :] = jnp.zeros((nsb - blocks, tq), F32)
            return carry

        lax.fori_loop(0, N_KV_GROUPS, group_body, 0)

    tier = (pl.program_id(1) * n_tiers) // n_tiles
    for k in range(n_tiers):
        pl.when(tier == k)(functools.partial(tile_body, tc * (k + 1) // n_tiers, nsb * (k + 1) // n_tiers))


def _cmp_select(q, kc, vct, ovt, tq=256):
    b, g, r, t_len, _ = q.shape
    tc = kc.shape[2]
    nsb = ovt.shape[0]
    tq = min(tq, t_len)
    top_n = min(SEL_TOPN, nsb)
    assert top_n >= N_FORCED
    n_tiers = 4
    assert (t_len // tq) % n_tiers == 0 and nsb % (n_tiers * SUBLANES) == 0 and tc % (n_tiers * LANES // 4) == 0
    return pl.pallas_call(
        functools.partial(_cmp_select_kernel, top_n=top_n, n_tiers=n_tiers), grid=(b, t_len // tq),
        in_specs=[pl.BlockSpec((1, g, r, tq, LANES), lambda i, j: (i, 0, 0, j, 0)),
                  pl.BlockSpec((1, g, tc, LANES), lambda i, j: (i, 0, 0, 0)),
                  pl.BlockSpec((1, g, LANES, tc), lambda i, j: (i, 0, 0, 0)),
                  pl.BlockSpec(ovt.shape, lambda i, j: (0, 0))],
        out_specs=[pl.BlockSpec((1, g, r, HEAD_DIM, tq), lambda i, j: (i, 0, 0, 0, j)),
                   pl.BlockSpec((1, g, nsb, tq), lambda i, j: (i, 0, 0, j))],
        out_shape=[jax.ShapeDtypeStruct((b, g, r, HEAD_DIM, t_len), F32),
                   jax.ShapeDtypeStruct((b, g, nsb, t_len), F32)],
        scratch_shapes=[pltpu.VMEM((tc, tq), F32)],
        compiler_params=_cparams("parallel", "parallel"), name="nsa_cmp_select",
    )(q, kc, vct, ovt)


def _sel_win_kernel(q_ref, ka_ref, vat_ref, kw_ref, vwt_ref, selt_ref, ocmp_ref, gate_ref, o_ref,
                    qa_scr, sa_scr, sb_scr, ma_scr, mb_scr, *, kc):
    r_heads, tq = q_ref.shape[2], q_ref.shape[3]
    dh = HEAD_DIM
    t_len = ka_ref.shape[2]
    per_chunk = kc // SEL_BLOCK
    aug_rows = -(-per_chunk // (2 * SUBLANES)) * (2 * SUBLANES)
    t0 = pl.program_id(2) * tq
    t_lane = t0 + lax.broadcasted_iota(jnp.int32, (1, tq), 1)

    for r in range(r_heads):
        qa_scr[:, r * tq:(r + 1) * tq] = q_ref[0, 0, r].astype(F32).T.astype(BF16)

    def softmax_t(s, m_new):
        return jnp.exp2(s - m_new).astype(BF16)

    span = min(tq + WINDOW, t_len)
    w0 = pl.multiple_of(jnp.maximum(t0 + tq - span, 0), tq)
    kpos_w = w0 + lax.broadcasted_iota(jnp.int32, (span, 1), 0)
    in_win = (kpos_w <= t_lane) & (t_lane - kpos_w < WINDOW)
    bias_w = jnp.where(in_win, 0.0, MASK_BIAS)
    s = _dot(kw_ref[0, 0, pl.ds(w0, span), :], qa_scr[...]) + jnp.concatenate([bias_w] * r_heads, axis=1)
    ow = _dot(vwt_ref[0, 0, :, pl.ds(w0, span)], softmax_t(s, jnp.max(s, axis=0, keepdims=True)))
    o_win_t = ow[:dh] / ow[dh:dh + 1]

    n_full = t0 // kc
    last_full = jnp.maximum(n_full - 1, 0)

    def stash_steps(c, s_scr, m_scr, diagonal=False):
        if not diagonal:
            c = jnp.where(c < n_full, c, n_full + 1)
        ci = jnp.minimum(c, n_full)

        def write_mask_rows():
            sel8 = selt_ref[0, 0, pl.ds(pl.multiple_of(ci * per_chunk, per_chunk), per_chunk), :]
            sb = (jnp.where(c <= n_full, sel8, 0.0) - 1.0) * (-MASK_BIAS)
            if aug_rows > per_chunk:
                sb = jnp.concatenate([sb, jnp.zeros((aug_rows - per_chunk, tq), F32)], axis=0)
            qa_scr[dh:dh + aug_rows, :] = jnp.concatenate([sb.astype(BF16)] * r_heads, axis=1)

        def head(n):
            cs = slice(n * tq, (n + 1) * tq)
            s = _dot(ka_ref[0, 0, pl.ds(pl.multiple_of(ci * kc, kc), kc), :], qa_scr[:, cs])
            s_scr[:, cs] = s
            m_scr[0:1, cs] = jnp.max(s, axis=0, keepdims=True)

        return write_mask_rows, head

    def update_head(n, s_scr, m_scr, c, m, acc):
        cs = slice(n * tq, (n + 1) * tq)
        ci = jnp.minimum(c, n_full)
        m_new = jnp.maximum(m, m_scr[0:1, cs])
        pv = _dot(vat_ref[0, 0, :, pl.ds(pl.multiple_of(ci * kc, kc), kc)], softmax_t(s_scr[:, cs], m_new))
        return m_new, jnp.exp2(m - m_new) * acc + pv

    def half_trip(c_next, s_next, m_next, c_cur, s_cur, m_cur, state):
        mask_rows, stash_head = stash_steps(c_next, s_next, m_next)
        mask_rows()
        state = list(state)
        for n in range(r_heads):
            stash_head(n)
            state[n] = update_head(n, s_cur, m_cur, c_cur, *state[n])
        return state

    def stash(c, s_scr, m_scr):
        mask_rows, stash_head = stash_steps(c, s_scr, m_scr)
        mask_rows()
        for n in range(r_heads):
            stash_head(n)

    mask_rows, _ = stash_steps(n_full, sa_scr, ma_scr, diagonal=True)
    mask_rows()
    kd0 = pl.multiple_of(n_full * kc, kc)
    causal = jnp.where(kd0 + lax.broadcasted_iota(jnp.int32, (kc, 1), 0) <= t_lane, 0.0, MASK_BIAS)
    s = _dot(ka_ref[0, 0, pl.ds(kd0, kc), :], qa_scr[...]) + jnp.concatenate([causal] * r_heads, axis=1)
    m0 = jnp.max(s, axis=0, keepdims=True)
    acc0 = _dot(vat_ref[0, 0, :, pl.ds(kd0, kc)], softmax_t(s, m0))
    state = [(m0[:, n * tq:(n + 1) * tq], acc0[:, n * tq:(n + 1) * tq]) for n in range(r_heads)]
    flat = lambda st: tuple(x for pair in st for x in pair)
    unflat = lambda fl: [(fl[2 * n], fl[2 * n + 1]) for n in range(r_heads)]

    stash(0, sa_scr, ma_scr)

    def body(i, fl):
        c = 2 * i
        st = half_trip(c + 1, sb_scr, mb_scr, jnp.minimum(c, last_full), sa_scr, ma_scr, unflat(fl))
        st = half_trip(c + 2, sa_scr, ma_scr, jnp.minimum(c + 1, last_full), sb_scr, mb_scr, st)
        return flat(st)

    fl = lax.fori_loop(0, n_full // 2, body, flat(state))

    def tail(fl):
        st = unflat(fl)
        return flat([update_head(n, sa_scr, ma_scr, last_full, *st[n]) for n in range(r_heads)])

    fl = lax.cond(n_full % 2 == 1, tail, lambda fl: fl, fl)
    acc = jnp.concatenate([fl[2 * n + 1] for n in range(r_heads)], axis=1)
    o_sel_t = acc[:dh] / acc[dh:dh + 1]

    gates_t = gate_ref[0, 0].T
    parts = []
    for r in range(r_heads):
        cs = slice(r * tq, (r + 1) * tq)
        parts.append(gates_t[3 * r:3 * r + 1] * ocmp_ref[0, 0, r]
                     + gates_t[3 * r + 1:3 * r + 2] * o_sel_t[:, cs]
                     + gates_t[3 * r + 2:3 * r + 3] * o_win_t[:, cs])
    o_ref[0] = jnp.concatenate(parts, axis=0).T.astype(o_ref.dtype)


def _sel_win(q, ka, vat, kw, vwt, selt, ocmp, gates, tq=256, kc=SEL_CHUNK):
    b, g, r, t_len, _ = q.shape
    nsb = selt.shape[2]
    vrows = vat.shape[2]
    gw = r * HEAD_DIM
    k_spec = pl.BlockSpec((1, 1, t_len, LANES), lambda i, j, k: (i, j, 0, 0))
    vt_spec = pl.BlockSpec((1, 1, vrows, t_len), lambda i, j, k: (i, j, 0, 0))
    return pl.pallas_call(
        functools.partial(_sel_win_kernel, kc=kc), grid=(b, g, t_len // tq),
        in_specs=[pl.BlockSpec((1, 1, r, tq, LANES), lambda i, j, k: (i, j, 0, k, 0)),
                  k_spec, vt_spec, k_spec, vt_spec,
                  pl.BlockSpec((1, 1, nsb, tq), lambda i, j, k: (i, j, 0, k)),
                  pl.BlockSpec((1, 1, r, HEAD_DIM, tq), lambda i, j, k: (i, j, 0, 0, k)),
                  pl.BlockSpec((1, 1, tq, LANES), lambda i, j, k: (i, j, k, 0))],
        out_specs=pl.BlockSpec((1, tq, gw), lambda i, j, k: (i, k, j)),
        out_shape=jax.ShapeDtypeStruct((b, t_len, g * gw), BF16),
        scratch_shapes=[pltpu.VMEM((LANES, r * tq), BF16),
                        pltpu.VMEM((kc, r * tq), F32), pltpu.VMEM((kc, r * tq), F32),
                        pltpu.VMEM((SUBLANES, r * tq), F32), pltpu.VMEM((SUBLANES, r * tq), F32)],
        compiler_params=_cparams("parallel", "parallel", "arbitrary"), name="nsa_sel_win",
    )(q, ka, vat, kw, vwt, selt, ocmp, gates)


def _ffn_tile(x, g_ref, wg_ref, wu_ref, wd_ref, hc):
    h = _rms_rows(x, g_ref[...]).astype(BF16)
    acc = x
    for c in range(wg_ref.shape[1] // hc):
        sl = slice(c * hc, (c + 1) * hc)
        a = jax.nn.silu(_dot(h, wg_ref[:, sl])) * _dot(h, wu_ref[:, sl])
        acc = acc + _dot(a.astype(BF16), wd_ref[sl, :])
    return acc


def _ffn_kernel(x_ref, g_ref, wg_ref, wu_ref, wd_ref, o_ref, *, hc):
    o_ref[...] = _ffn_tile(x_ref[...], g_ref, wg_ref, wu_ref, wd_ref, hc)


def _attn_out_ffn_kernel(a_ref, wo_ref, x_ref, g_ref, wg_ref, wu_ref, wd_ref, o_ref, *, hc):
    x = x_ref[...] + _dot(a_ref[...], wo_ref[...])
    o_ref[...] = _ffn_tile(x, g_ref, wg_ref, wu_ref, wd_ref, hc)


def _ffn(x2, g, wg, wu, wd, attn=None, tm=512, hc=MXU_DIM):
    m, d = x2.shape
    full = lambda a: pl.BlockSpec(a.shape, lambda i: (0,) * a.ndim)
    row = pl.BlockSpec((tm, d), lambda i: (i, 0))
    if attn is None:
        body, pre, pre_specs = _ffn_kernel, (), []
    else:
        body, pre = _attn_out_ffn_kernel, attn
        pre_specs = [pl.BlockSpec((tm, attn[0].shape[1]), lambda i: (i, 0)), full(attn[1])]
    return pl.pallas_call(
        functools.partial(body, hc=hc), grid=(m // tm,),
        in_specs=pre_specs + [row, full(g), full(wg), full(wu), full(wd)],
        out_specs=row, out_shape=jax.ShapeDtypeStruct((m, d), F32),
        compiler_params=_cparams("parallel"), name="ffn_swiglu",
    )(*pre, x2, g, wg, wu, wd)


def _s5_kernel(x_ref, g_ref, win_ref, pm_ref, pmt_ref, bre_ref, bim_ref, a1_ref, aseg_ref, pw_ref, cre_ref, cim_ref,
               d_ref, wglu_ref, o_ref, xr_scr, xi_scr, cin_scr, car_scr, *, lane_chunk):
    rows, d = x_ref.shape[1], x_ref.shape[2]
    n_state = xr_scr.shape[1]
    n_kt = bre_ref.shape[0]
    kw = bre_ref.shape[1]
    sw = bre_ref.shape[2]
    n_seg = SUBLANES
    seg = rows // n_seg

    @pl.when(pl.program_id(1) == 0)
    def _():
        car_scr[...] = jnp.zeros_like(car_scr)

    x = x_ref[0]
    h = _dot(pm_ref[...], _rms_rows(x, g_ref[...]).astype(BF16)).astype(BF16)
    u = _dot(h, win_ref[...])
    ub = u.astype(BF16)
    for kt in range(n_kt):
        uk = ub[:, kt * kw:(kt + 1) * kw]
        xr_scr[:, kt * sw:(kt + 1) * sw] = _dot(uk, bre_ref[kt])
        xi_scr[:, kt * sw:(kt + 1) * sw] = _dot(uk, bim_ref[kt])

    for lc in range(n_state // lane_chunk):
        ls = slice(lc * lane_chunk, (lc + 1) * lane_chunk)
        ar, ai = a1_ref[0, :, ls], a1_ref[1, :, ls]

        def step(r, carry, ls=ls, ar=ar, ai=ai):
            pr, pi = carry
            r0 = pl.multiple_of(r * SUBLANES, SUBLANES)
            nr = ar * pr - ai * pi + xr_scr[pl.ds(r0, SUBLANES), ls]
            ni = ar * pi + ai * pr + xi_scr[pl.ds(r0, SUBLANES), ls]
            xr_scr[pl.ds(r0, SUBLANES), ls] = nr
            xi_scr[pl.ds(r0, SUBLANES), ls] = ni
            return nr, ni

        zero = jnp.zeros((SUBLANES, lane_chunk), F32)
        lr, li = lax.fori_loop(0, seg, step, (zero, zero), unroll=4)

        sr, si = aseg_ref[0:1, ls], aseg_ref[1:2, ls]
        cr, ci = car_scr[0:1, ls], car_scr[1:2, ls]
        for s in range(n_seg):
            cin_scr[0, s:s + 1, ls] = cr
            cin_scr[1, s:s + 1, ls] = ci
            cr, ci = (sr * cr - si * ci + lr[s:s + 1], sr * ci + si * cr + li[s:s + 1])
        car_scr[0:1, ls] = cr
        car_scr[1:2, ls] = ci

        def fix(r, carry, ls=ls):
            r0 = pl.multiple_of(r * SUBLANES, SUBLANES)
            pr, pi = pw_ref[0, pl.ds(r, 1), ls], pw_ref[1, pl.ds(r, 1), ls]
            cr, ci = cin_scr[0, :, ls], cin_scr[1, :, ls]
            xr_scr[pl.ds(r0, SUBLANES), ls] += pr * cr - pi * ci
            xi_scr[pl.ds(r0, SUBLANES), ls] += pr * ci + pi * cr
            return carry

        lax.fori_loop(0, seg, fix, 0, unroll=4)

    ys = []
    for kt in range(n_kt):
        ss = slice(kt * sw, (kt + 1) * sw)
        ys.append(_dot(xr_scr[:, ss].astype(BF16), cre_ref[kt]) + _dot(xi_scr[:, ss].astype(BF16), cim_ref[kt]))
    y = jnp.concatenate(ys, axis=1) + d_ref[...] * u
    z = _dot(pmt_ref[...], jax.nn.gelu(y).astype(BF16)).astype(BF16)
    vg = _dot(z, wglu_ref[...])
    o_ref[0] = x + vg[:, :d] * jax.nn.sigmoid(vg[:, d:])


def _s5_tables(b_re, b_im, c_re, c_im, log_dt, a_re, a_im, seg):
    ng, ps, cg = b_re.shape
    per_tile = MXU_DIM // cg
    n_kt = ng // per_tile
    dt = jnp.exp(log_dt)[:, None]
    lam_r, lam_i = dt * a_re, dt * a_im
    mag = jnp.exp(lam_r)
    abar_r, abar_i = mag * jnp.cos(lam_i), mag * jnp.sin(lam_i)
    den = a_re * a_re + a_im * a_im
    coef_r = ((abar_r - 1.0) * a_re + abar_i * a_im) / den
    coef_i = (abar_i * a_re - (abar_r - 1.0) * a_im) / den
    bbar_r = coef_r[..., None] * b_re - coef_i[..., None] * b_im
    bbar_i = coef_r[..., None] * b_im + coef_i[..., None] * b_re
    eye = jnp.eye(per_tile, dtype=F32)
    bd_in = lambda m: jnp.einsum('kgpc,gh->kgchp', m.reshape(n_kt, per_tile, ps, cg), eye).reshape(
        n_kt, per_tile * cg, per_tile * ps).astype(BF16)
    bd_out = lambda m: jnp.einsum('kgcp,gh->kgphc', m.reshape(n_kt, per_tile, cg, ps), eye).reshape(
        n_kt, per_tile * ps, per_tile * cg).astype(BF16)

    def power(n):
        pm = jnp.exp(n * lam_r)
        return (pm * jnp.cos(n * lam_i)).reshape(-1), (pm * jnp.sin(n * lam_i)).reshape(-1)

    a1 = jnp.tile(jnp.stack(power(1.0))[:, None, :], (1, SUBLANES, 1))
    aseg = jnp.stack(power(float(seg)))
    steps = [power(float(i + 1)) for i in range(seg)]
    pw = jnp.stack([jnp.stack([p[0] for p in steps]), jnp.stack([p[1] for p in steps])])
    return bd_in(bbar_r), bd_in(bbar_i), a1, aseg, pw, bd_out(c_re), bd_out(-c_im)


def _s5_layer(x, g_norm, w_in, b_re, b_im, c_re, c_im, d_skip, log_dt, a_re, a_im, w_glu, rows=256):
    b, t, d = x.shape
    rows = min(rows, t)
    bre, bim, a1, aseg, pw, cre, cim = _s5_tables(b_re, b_im, c_re, c_im, log_dt, a_re, a_im, rows // SUBLANES)
    n_state = a1.shape[-1]
    full = lambda a: pl.BlockSpec(a.shape, lambda i, j: (0,) * a.ndim)
    seg = rows // SUBLANES
    new_row = jnp.arange(rows)
    pm = (jnp.arange(rows)[None, :] == (new_row % SUBLANES * seg + new_row // SUBLANES)[:, None]).astype(BF16)
    args = (g_norm[None, :], w_in.astype(BF16), pm, pm.T, bre, bim, a1, aseg, pw, cre, cim, d_skip[None, :],
            w_glu.astype(BF16))
    return pl.pallas_call(
        functools.partial(_s5_kernel, lane_chunk=1024), grid=(b, t // rows),
        in_specs=[pl.BlockSpec((1, rows, d), lambda i, j: (i, j, 0))] + [full(a) for a in args],
        out_specs=pl.BlockSpec((1, rows, d), lambda i, j: (i, j, 0)),
        out_shape=jax.ShapeDtypeStruct((b, t, d), F32),
        scratch_shapes=[pltpu.VMEM((rows, n_state), F32), pltpu.VMEM((rows, n_state), F32),
                        pltpu.VMEM((2, SUBLANES, n_state), F32), pltpu.VMEM((SUBLANES, n_state), F32)],
        compiler_params=_cparams("parallel", "arbitrary"), name="s5_mixer",
    )(x, *args)


def _pad_heads(w, n_heads, width):
    d = w.shape[0]
    return jnp.pad(w.reshape(d, n_heads, width), ((0, 0), (0, 0), (0, LANES - width))).reshape(d, n_heads * LANES)


def _nsa_layer(x, g_norm, w_in, w_out, q_gain, k_gain, cmp_pos, cmp_w1, cmp_b1, cmp_w2, cmp_b2):
    b, t, d = x.shape
    gn, rh, dh = N_KV_GROUPS, HEADS_PER_GROUP, HEAD_DIM
    assert t % SEL_CHUNK == 0 and t >= SEL_CHUNK + WINDOW
    x2 = x.reshape(b * t, d)
    scale = LOG2E / math.sqrt(dh)

    kvw = lambda i: w_in[:, d + i * KV_DIM:d + (i + 1) * KV_DIM]
    wq = _pad_heads(w_in[:, :d], N_HEADS, dh).astype(BF16)
    wk = jnp.concatenate([_pad_heads(kvw(i), gn, dh) for i in (2, 3, 4, 5)], axis=1).astype(BF16)
    wc = jnp.concatenate([kvw(0), kvw(1)], axis=1).astype(BF16)
    wg = _pad_heads(w_in[:, d + 6 * KV_DIM:], gn, rh * N_BRANCHES).astype(BF16)
    pad_gain = lambda v, n: jnp.tile(jnp.pad(v, (0, LANES - dh)), n)[None, :]
    qg_p = pad_gain(q_gain * scale, N_HEADS)
    kg_p = jnp.concatenate([pad_gain(k_gain[1], gn), pad_gain(k_gain[2], gn)], axis=0)

    q, ka, va, kw, vw, kc_raw, vc_raw, gates = _nsa_proj(x2, b, t, g_norm[None, :], wq, wk, wc, wg, qg_p, kg_p)

    tc = t // CMP_STRIDE
    half = CMP_STRIDE * dh
    blk = lambda a: a.reshape(b, tc, CMP_STRIDE, gn, dh).transpose(0, 3, 1, 2, 4).reshape(b, gn, tc, half)
    comp = []
    for i, (raw, norm) in enumerate(((kc_raw, True), (vc_raw, False))):
        w1 = cmp_w1[i].astype(BF16)
        comp.append(_compress(blk(raw), cmp_pos[i].reshape(2, half), w1[:half], w1[half:], cmp_b1[i][None, :],
                              jnp.pad(cmp_w2[i], ((0, 0), (0, LANES - dh))).astype(BF16),
                              jnp.pad(cmp_b2[i], (0, LANES - dh))[None, :],
                              jnp.pad(k_gain[0], (0, LANES - dh))[None, :], norm))
    kc, vc = comp
    vct = vc.transpose(0, 1, 3, 2)

    nsb = t // SEL_BLOCK
    n_i = jnp.arange(tc)
    j_i = jnp.arange(nsb)
    n_cmp = (t - CMP_BLOCK) // CMP_STRIDE + 1
    ovt = ((n_i[None, :] * CMP_STRIDE < (j_i[:, None] + 1) * SEL_BLOCK)
           & (n_i[None, :] * CMP_STRIDE + CMP_BLOCK > j_i[:, None] * SEL_BLOCK)
           & (n_i[None, :] < n_cmp)).astype(BF16)
    ocmp, selt = _cmp_select(q, kc, vct, ovt)

    vrows = dh + 2 * SUBLANES
    vat = va[..., :vrows].transpose(0, 1, 3, 2)
    vwt = vw[..., :vrows].transpose(0, 1, 3, 2)
    o = _sel_win(q, ka, vat, kw, vwt, selt, ocmp, gates)
    return o.reshape(b * t, d), w_out.astype(BF16)


def kernel(x, mix_norm, ffn_norm, nsa_w_in, nsa_w_out, nsa_q_gain, nsa_k_gain, nsa_cmp_pos, nsa_cmp_w1, nsa_cmp_b1, nsa_cmp_w2, nsa_cmp_b2, s5_w_in, s5_b_re, s5_b_im, s5_c_re, s5_c_im, s5_d, s5_log_dt, s5_a_re, s5_a_im, s5_w_glu, ffn_w_gate, ffn_w_up, ffn_w_down):
    b, t, d = x.shape
    depth = mix_norm.shape[0]
    for layer in range(depth):
        i = layer // 2
        attn = None
        if layer % 2 == 0:
            attn = _nsa_layer(x, mix_norm[layer], nsa_w_in[i], nsa_w_out[i], nsa_q_gain[i], nsa_k_gain[i],
                              nsa_cmp_pos[i], nsa_cmp_w1[i], nsa_cmp_b1[i], nsa_cmp_w2[i], nsa_cmp_b2[i])
        else:
            x = _s5_layer(x, mix_norm[layer], s5_w_in[i], s5_b_re[i], s5_b_im[i], s5_c_re[i], s5_c_im[i],
                          s5_d[i], s5_log_dt[i], s5_a_re[i], s5_a_im[i], s5_w_glu[i])
        x = _ffn(x.reshape(b * t, d), ffn_norm[layer][None, :], ffn_w_gate[layer].astype(BF16),
                 ffn_w_up[layer].astype(BF16), ffn_w_down[layer].astype(BF16), attn=attn).reshape(b, t, d)
    return x
```

```python
import functools
import math

import jax
import jax.numpy as jnp
from jax import lax
from jax.experimental import pallas as pl
from jax.experimental.pallas import tpu as pltpu

F32 = jnp.float32
BF16 = jnp.bfloat16

EPS = 1e-6
NEG = -1e30
MASK_BIAS = -1e9
LOG2E = 1.4426950408889634

N_HEADS = 16
HEAD_DIM = 64
N_KV_GROUPS = 4
HEADS_PER_GROUP = 4
KV_DIM = N_KV_GROUPS * HEAD_DIM
CMP_BLOCK = 32
CMP_STRIDE = 16
SEL_BLOCK = 64
SEL_TOPN = 16
N_FORCED = 3
WINDOW = 512
N_BRANCHES = 3
S5_GROUP = 16
S5_STATE = 64

LANES = 128
SUBLANES = 8
MXU_DIM = 256
SEL_CHUNK = 512
V7X_VMEM_LIMIT_BYTES = 56 * 1024 * 1024


def _cparams(*semantics):
    return pltpu.CompilerParams(dimension_semantics=semantics,
                                vmem_limit_bytes=V7X_VMEM_LIMIT_BYTES)


def _rms_rows(x, g):
    var = jnp.mean(x * x, axis=-1, keepdims=True)
    return x * lax.rsqrt(var + EPS) * g


def _dot(a, b):
    return jnp.dot(a, b, preferred_element_type=F32)


def _dot_nt(a, b):
    return lax.dot_general(a, b, (((1,), (1,)), ((), ())), preferred_element_type=F32)


def _nsa_proj_kernel(x_ref, g_ref, wq_ref, wk_ref, wc_ref, wg_ref, qg_ref, kg_ref,
                     q_ref, ka_ref, va_ref, kw_ref, vw_ref, kcr_ref, vcr_ref, gate_ref):
    tm = x_ref.shape[0]
    h = _rms_rows(x_ref[...], g_ref[...]).astype(BF16)
    inv_dh = 1.0 / HEAD_DIM

    def head_norm(v, gain):
        var = jnp.sum(v * v, axis=-1, keepdims=True) * inv_dh
        return v * lax.rsqrt(var + EPS) * gain

    q = _dot(h, wq_ref[...])
    qg = qg_ref[...]
    for hd in range(N_HEADS):
        sl = slice(hd * LANES, (hd + 1) * LANES)
        g, r = divmod(hd, HEADS_PER_GROUP)
        q_ref[0, g, r] = head_norm(q[:, sl], qg[:, sl]).astype(BF16)

    row = lax.rem(pl.program_id(0) * tm, SEL_CHUNK) + lax.broadcasted_iota(jnp.int32, (tm, 1), 0)
    lane = lax.broadcasted_iota(jnp.int32, (1, LANES), 1)
    blk_flag = ((row // SEL_BLOCK) == (lane - HEAD_DIM)).astype(F32)
    one_flag = (lane == HEAD_DIM).astype(F32)

    kv = _dot(h, wk_ref[...])
    kg = kg_ref[...]
    plan = ((ka_ref, 0, blk_flag), (va_ref, None, one_flag), (kw_ref, 1, None), (vw_ref, None, one_flag))
    for part, (ref, gain_row, flag) in enumerate(plan):
        for g in range(N_KV_GROUPS):
            piece = kv[:, (part * N_KV_GROUPS + g) * LANES:(part * N_KV_GROUPS + g + 1) * LANES]
            if gain_row is not None:
                piece = head_norm(piece, kg[gain_row:gain_row + 1, g * LANES:(g + 1) * LANES])
            if flag is not None:
                piece = piece + flag
            ref[0, g] = piece.astype(BF16)

    c = _dot(h, wc_ref[...])
    kcr_ref[...] = c[:, :KV_DIM]
    vcr_ref[...] = c[:, KV_DIM:]
    gt = jax.nn.sigmoid(_dot(h, wg_ref[...]))
    for g in range(N_KV_GROUPS):
        gate_ref[0, g] = gt[:, g * LANES:(g + 1) * LANES]


def _nsa_proj(x2, b, t, g, wq, wk, wc, wg, qg_p, kg_p, tm=512):
    m, d = x2.shape
    assert SEL_CHUNK % tm == 0 and t % SEL_CHUNK == 0
    nt = t // tm
    gn, rh = N_KV_GROUPS, HEADS_PER_GROUP
    full = lambda a: pl.BlockSpec(a.shape, lambda i: (0,) * a.ndim)
    row = lambda n: pl.BlockSpec((tm, n), lambda i: (i, 0))
    head = pl.BlockSpec((1, gn, tm, LANES), lambda i: (i // nt, 0, i % nt, 0))
    outs = [jax.ShapeDtypeStruct((b, gn, rh, t, LANES), BF16)]
    outs += [jax.ShapeDtypeStruct((b, gn, t, LANES), BF16)] * 4
    outs += [jax.ShapeDtypeStruct((m, KV_DIM), F32)] * 2
    outs += [jax.ShapeDtypeStruct((b, gn, t, LANES), F32)]
    return pl.pallas_call(
        _nsa_proj_kernel, grid=(m // tm,),
        in_specs=[row(d), full(g), full(wq), full(wk), full(wc), full(wg), full(qg_p), full(kg_p)],
        out_specs=[pl.BlockSpec((1, gn, rh, tm, LANES), lambda i: (i // nt, 0, 0, i % nt, 0)),
                   head, head, head, head, row(KV_DIM), row(KV_DIM), head],
        out_shape=outs, compiler_params=_cparams("parallel"), name="nsa_proj",
    )(x2, g, wq, wk, wc, wg, qg_p, kg_p)


def _compress_kernel(a_ref, pos_ref, w1a_ref, w1b_ref, b1_ref, w2_ref, b2_ref, gain_ref, o_ref, *, norm):
    a = a_ref[0, 0]
    tc = a.shape[0]
    pos = pos_ref[...]
    h1 = _dot((a + pos[0:1]).astype(BF16), w1a_ref[...])
    h2 = _dot((a + pos[1:2]).astype(BF16), w1b_ref[...])
    h = h1 + pltpu.roll(h2, tc - 1, 0) + b1_ref[...]
    hid = jax.nn.gelu(h).astype(BF16)
    out = _dot(hid, w2_ref[...]) + b2_ref[...]
    if norm:
        var = jnp.sum(out * out, axis=-1, keepdims=True) * (1.0 / HEAD_DIM)
        out = out * lax.rsqrt(var + EPS) * gain_ref[...]
    o_ref[0, 0] = out.astype(o_ref.dtype)


def _compress(a, pos2, w1a, w1b, b1, w2, b2, gain, norm):
    b, g, tc, w = a.shape
    full = lambda z: pl.BlockSpec(z.shape, lambda i, j: (0,) * z.ndim)
    return pl.pallas_call(
        functools.partial(_compress_kernel, norm=norm), grid=(b, g),
        in_specs=[pl.BlockSpec((1, 1, tc, w), lambda i, j: (i, j, 0, 0)),
                  full(pos2), full(w1a), full(w1b), full(b1), full(w2), full(b2), full(gain)],
        out_specs=pl.BlockSpec((1, 1, tc, LANES), lambda i, j: (i, j, 0, 0)),
        out_shape=jax.ShapeDtypeStruct((b, g, tc, LANES), BF16),
        compiler_params=_cparams("parallel", "parallel"), name="nsa_compress",
    )(a, pos2, w1a, w1b, b1, w2, b2, gain)


def _cmp_select_kernel(q_ref, kc_ref, vct_ref, ovt_ref, ocmp_ref, sel_ref, bias_scr, *, top_n, n_tiers):
    tq = q_ref.shape[3]
    tc = kc_ref.shape[2]
    nsb = ovt_ref.shape[0]
    n_tiles = pl.num_programs(1)
    t = pl.program_id(1) * tq + lax.broadcasted_iota(jnp.int32, (1, tq), 1)
    cur = t // SEL_BLOCK
    any_vis = t >= CMP_BLOCK - 1
    n_rest = max(top_n - N_FORCED, 0)

    def tile_body(rows, blocks):
        n_idx = lax.broadcasted_iota(jnp.int32, (rows, 1), 0)
        bias = jnp.where((n_idx * CMP_STRIDE + (CMP_BLOCK - 1)) <= t, 0.0, NEG)
        j_idx = lax.broadcasted_iota(jnp.int32, (blocks, 1), 0)
        forced = (j_idx == 0) | (j_idx == cur) | (j_idx == cur - 1)
        valid = j_idx <= cur
        j_f = j_idx.astype(F32)
        bias_scr[:rows, :] = bias

        def group_body(g, carry):
            kc = kc_ref[0, g, :rows, :]
            vct = vct_ref[0, g, :, :rows]
            psum = jnp.zeros((rows, tq), F32)
            scores = []
            for r in range(HEADS_PER_GROUP):
                qr = q_ref[0, g, r]
                scores.append(jnp.concatenate([_dot_nt(kc[:rows // 2], qr), _dot_nt(kc[rows // 2:], qr)], axis=0))
            for r in range(HEADS_PER_GROUP):
                s = scores[r] + bias_scr[:rows, :]
                e = jnp.exp2(s - jnp.max(s, axis=0, keepdims=True))
                den = jnp.sum(e, axis=0, keepdims=True)
                p = e * jnp.where(any_vis, 1.0 / den, 0.0)
                pb = p.astype(BF16)
                o_t = (_dot(vct[:, :rows // 2], pb[:rows // 2])
                       + _dot(vct[:, rows // 2:], pb[rows // 2:]))
                ocmp_ref[0, g, r] = o_t[:HEAD_DIM]
                psum = psum + p
            hi = psum.astype(BF16)
            lo = (psum - hi.astype(F32)).astype(BF16)
            ovt = ovt_ref[:blocks, :rows]
            psel = _dot(ovt, hi) + _dot(ovt, lo)
            score = jnp.where(valid & ~forced, psel, NEG)
            for _ in range(n_rest):
                mx = jnp.max(score, axis=0, keepdims=True)
                first = jnp.min(jnp.where(score == mx, j_f, float(nsb)), axis=0, keepdims=True)
                score = jnp.where(j_f == first, -jnp.inf, score)
            sel_ref[0, g, :blocks, :] = jnp.where(forced | ((score == -jnp.inf) & valid), 1.0, 0.0)
            if blocks < nsb:
                sel_ref[0, g, blocks:, :] = jnp.zeros((nsb - blocks, tq), F32)
            return carry

        lax.fori_loop(0, N_KV_GROUPS, group_body, 0)

    tier = (pl.program_id(1) * n_tiers) // n_tiles
    for k in range(n_tiers):
        pl.when(tier == k)(functools.partial(tile_body, tc * (k + 1) // n_tiers, nsb * (k + 1) // n_tiers))


def _cmp_select(q, kc, vct, ovt, tq=256):
    b, g, r, t_len, _ = q.shape
    tc = kc.shape[2]
    nsb = ovt.shape[0]
    tq = min(tq, t_len)
    top_n = min(SEL_TOPN, nsb)
    assert top_n >= N_FORCED
    n_tiers = 4
    assert (t_len // tq) % n_tiers == 0 and nsb % (n_tiers * SUBLANES) == 0 and tc % (n_tiers * LANES // 4) == 0
    return pl.pallas_call(
        functools.partial(_cmp_select_kernel, top_n=top_n, n_tiers=n_tiers), grid=(b, t_len // tq),
        in_specs=[pl.BlockSpec((1, g, r, tq, LANES), lambda i, j: (i, 0, 0, j, 0)),
                  pl.BlockSpec((1, g, tc, LANES), lambda i, j: (i, 0, 0, 0)),
                  pl.BlockSpec((1, g, LANES, tc), lambda i, j: (i, 0, 0, 0)),
                  pl.BlockSpec(ovt.shape, lambda i, j: (0, 0))],
        out_specs=[pl.BlockSpec((1, g, r, HEAD_DIM, tq), lambda i, j: (i, 0, 0, 0, j)),
                   pl.BlockSpec((1, g, nsb, tq), lambda i, j: (i, 0, 0, j))],
        out_shape=[jax.ShapeDtypeStruct((b, g, r, HEAD_DIM, t_len), F32),
                   jax.ShapeDtypeStruct((b, g, nsb, t_len), F32)],
        scratch_shapes=[pltpu.VMEM((tc, tq), F32)],
        compiler_params=_cparams("parallel", "parallel"), name="nsa_cmp_select",
    )(q, kc, vct, ovt)


def _sel_win_kernel(q_ref, ka_ref, vat_ref, kw_ref, vwt_ref, selt_ref, ocmp_ref, gate_ref, o_ref,
                    qa_scr, sa_scr, sb_scr, ma_scr, mb_scr, *, kc):
    r_heads, tq = q_ref.shape[2], q_ref.shape[3]
    dh = HEAD_DIM
    t_len = ka_ref.shape[2]
    per_chunk = kc // SEL_BLOCK
    aug_rows = -(-per_chunk // (2 * SUBLANES)) * (2 * SUBLANES)
    t0 = pl.program_id(2) * tq
    t_lane = t0 + lax.broadcasted_iota(jnp.int32, (1, tq), 1)

    for r in range(r_heads):
        qa_scr[:, r * tq:(r + 1) * tq] = q_ref[0, 0, r].astype(F32).T.astype(BF16)

    def softmax_t(s, m_new):
        return jnp.exp2(s - m_new).astype(BF16)

    span = min(tq + WINDOW, t_len)
    w0 = pl.multiple_of(jnp.maximum(t0 + tq - span, 0), tq)
    kpos_w = w0 + lax.broadcasted_iota(jnp.int32, (span, 1), 0)
    in_win = (kpos_w <= t_lane) & (t_lane - kpos_w < WINDOW)
    bias_w = jnp.where(in_win, 0.0, MASK_BIAS)
    s_win = _dot(kw_ref[0, 0, pl.ds(w0, span), :], qa_scr[...]) + jnp.concatenate([bias_w] * r_heads, axis=1)

    n_full = t0 // kc
    last_full = jnp.maximum(n_full - 1, 0)

    def stash_steps(c, s_scr, m_scr, diagonal=False):
        if not diagonal:
            c = jnp.where(c < n_full, c, n_full + 1)
        ci = jnp.minimum(c, n_full)

        def write_mask_rows():
            sel8 = selt_ref[0, 0, pl.ds(pl.multiple_of(ci * per_chunk, per_chunk), per_chunk), :]
            sb = (jnp.where(c <= n_full, sel8, 0.0) - 1.0) * (-MASK_BIAS)
            if aug_rows > per_chunk:
                sb = jnp.concatenate([sb, jnp.zeros((aug_rows - per_chunk, tq), F32)], axis=0)
            qa_scr[dh:dh + aug_rows, :] = jnp.concatenate([sb.astype(BF16)] * r_heads, axis=1)

        def head(n):
            cs = slice(n * tq, (n + 1) * tq)
            s = _dot(ka_ref[0, 0, pl.ds(pl.multiple_of(ci * kc, kc), kc), :], qa_scr[:, cs])
            s_scr[:, cs] = s
            m_scr[0:1, cs] = jnp.max(s, axis=0, keepdims=True)

        return write_mask_rows, head

    def update_head(n, s_scr, m_scr, c, m, acc):
        cs = slice(n * tq, (n + 1) * tq)
        ci = jnp.minimum(c, n_full)
        m_new = jnp.maximum(m, m_scr[0:1, cs])
        pv = _dot(vat_ref[0, 0, :, pl.ds(pl.multiple_of(ci * kc, kc), kc)], softmax_t(s_scr[:, cs], m_new))
        return m_new, jnp.exp2(m - m_new) * acc + pv

    def half_trip(c_next, s_next, m_next, c_cur, s_cur, m_cur, state):
        mask_rows, stash_head = stash_steps(c_next, s_next, m_next)
        mask_rows()
        state = list(state)
        for n in range(r_heads):
            stash_head(n)
            state[n] = update_head(n, s_cur, m_cur, c_cur, *state[n])
        return state

    def stash(c, s_scr, m_scr):
        mask_rows, stash_head = stash_steps(c, s_scr, m_scr)
        mask_rows()
        for n in range(r_heads):
            stash_head(n)

    mask_rows, _ = stash_steps(n_full, sa_scr, ma_scr, diagonal=True)
    mask_rows()
    kd0 = pl.multiple_of(n_full * kc, kc)
    causal = jnp.where(kd0 + lax.broadcasted_iota(jnp.int32, (kc, 1), 0) <= t_lane, 0.0, MASK_BIAS)
    s_diag = _dot(ka_ref[0, 0, pl.ds(kd0, kc), :], qa_scr[...]) + jnp.concatenate([causal] * r_heads, axis=1)

    ow = _dot(vwt_ref[0, 0, :, pl.ds(w0, span)], softmax_t(s_win, jnp.max(s_win, axis=0, keepdims=True)))
    o_win_t = ow[:dh] / ow[dh:dh + 1]
    stash(0, sa_scr, ma_scr)
    m0 = jnp.max(s_diag, axis=0, keepdims=True)
    acc0 = _dot(vat_ref[0, 0, :, pl.ds(kd0, kc)], softmax_t(s_diag, m0))
    state = [(m0[:, n * tq:(n + 1) * tq], acc0[:, n * tq:(n + 1) * tq]) for n in range(r_heads)]
    flat = lambda st: tuple(x for pair in st for x in pair)
    unflat = lambda fl: [(fl[2 * n], fl[2 * n + 1]) for n in range(r_heads)]

    def body(i, fl):
        c = 2 * i
        st = half_trip(c + 1, sb_scr, mb_scr, jnp.minimum(c, last_full), sa_scr, ma_scr, unflat(fl))
        st = half_trip(c + 2, sa_scr, ma_scr, jnp.minimum(c + 1, last_full), sb_scr, mb_scr, st)
        return flat(st)

    fl = lax.fori_loop(0, n_full // 2, body, flat(state))

    def tail(fl):
        st = unflat(fl)
        return flat([update_head(n, sa_scr, ma_scr, last_full, *st[n]) for n in range(r_heads)])

    fl = lax.cond(n_full % 2 == 1, tail, lambda fl: fl, fl)
    acc = jnp.concatenate([fl[2 * n + 1] for n in range(r_heads)], axis=1)
    o_sel_t = acc[:dh] / acc[dh:dh + 1]

    gates_t = gate_ref[0, 0].T
    parts = []
    for r in range(r_heads):
        cs = slice(r * tq, (r + 1) * tq)
        parts.append(gates_t[3 * r:3 * r + 1] * ocmp_ref[0, 0, r]
                     + gates_t[3 * r + 1:3 * r + 2] * o_sel_t[:, cs]
                     + gates_t[3 * r + 2:3 * r + 3] * o_win_t[:, cs])
    o_ref[0] = jnp.concatenate(parts, axis=0).T.astype(o_ref.dtype)


def _sel_win(q, ka, vat, kw, vwt, selt, ocmp, gates, tq=256, kc=SEL_CHUNK):
    b, g, r, t_len, _ = q.shape
    nsb = selt.shape[2]
    vrows = vat.shape[2]
    gw = r * HEAD_DIM
    k_spec = pl.BlockSpec((1, 1, t_len, LANES), lambda i, j, k: (i, j, 0, 0))
    vt_spec = pl.BlockSpec((1, 1, vrows, t_len), lambda i, j, k: (i, j, 0, 0))
    return pl.pallas_call(
        functools.partial(_sel_win_kernel, kc=kc), grid=(b, g, t_len // tq),
        in_specs=[pl.BlockSpec((1, 1, r, tq, LANES), lambda i, j, k: (i, j, 0, k, 0)),
                  k_spec, vt_spec, k_spec, vt_spec,
                  pl.BlockSpec((1, 1, nsb, tq), lambda i, j, k: (i, j, 0, k)),
                  pl.BlockSpec((1, 1, r, HEAD_DIM, tq), lambda i, j, k: (i, j, 0, 0, k)),
                  pl.BlockSpec((1, 1, tq, LANES), lambda i, j, k: (i, j, k, 0))],
        out_specs=pl.BlockSpec((1, tq, gw), lambda i, j, k: (i, k, j)),
        out_shape=jax.ShapeDtypeStruct((b, t_len, g * gw), BF16),
        scratch_shapes=[pltpu.VMEM((LANES, r * tq), BF16),
                        pltpu.VMEM((kc, r * tq), F32), pltpu.VMEM((kc, r * tq), F32),
                        pltpu.VMEM((SUBLANES, r * tq), F32), pltpu.VMEM((SUBLANES, r * tq), F32)],
        compiler_params=_cparams("parallel", "parallel", "arbitrary"), name="nsa_sel_win",
    )(q, ka, vat, kw, vwt, selt, ocmp, gates)


def _ffn_tile(x, g_ref, wg_ref, wu_ref, wd_ref, hc):
    h = _rms_rows(x, g_ref[...]).astype(BF16)
    acc = x
    for c in range(wg_ref.shape[1] // hc):
        sl = slice(c * hc, (c + 1) * hc)
        a = jax.nn.silu(_dot(h, wg_ref[:, sl])) * _dot(h, wu_ref[:, sl])
        acc = acc + _dot(a.astype(BF16), wd_ref[sl, :])
    return acc


def _ffn_kernel(x_ref, g_ref, wg_ref, wu_ref, wd_ref, o_ref, *, hc):
    o_ref[...] = _ffn_tile(x_ref[...], g_ref, wg_ref, wu_ref, wd_ref, hc)


def _attn_out_ffn_kernel(a_ref, wo_ref, x_ref, g_ref, wg_ref, wu_ref, wd_ref, o_ref, *, hc):
    x = x_ref[...] + _dot(a_ref[...], wo_ref[...])
    o_ref[...] = _ffn_tile(x, g_ref, wg_ref, wu_ref, wd_ref, hc)


def _ffn(x2, g, wg, wu, wd, attn=None, tm=512, hc=MXU_DIM):
    m, d = x2.shape
    full = lambda a: pl.BlockSpec(a.shape, lambda i: (0,) * a.ndim)
    row = pl.BlockSpec((tm, d), lambda i: (i, 0))
    if attn is None:
        body, pre, pre_specs = _ffn_kernel, (), []
    else:
        body, pre = _attn_out_ffn_kernel, attn
        pre_specs = [pl.BlockSpec((tm, attn[0].shape[1]), lambda i: (i, 0)), full(attn[1])]
    return pl.pallas_call(
        functools.partial(body, hc=hc), grid=(m // tm,),
        in_specs=pre_specs + [row, full(g), full(wg), full(wu), full(wd)],
        out_specs=row, out_shape=jax.ShapeDtypeStruct((m, d), F32),
        compiler_params=_cparams("parallel"), name="ffn_swiglu",
    )(*pre, x2, g, wg, wu, wd)


def _s5_kernel(x_ref, g_ref, win_ref, pm_ref, pmt_ref, bre_ref, bim_ref, a1_ref, aseg_ref, pw_ref, cre_ref, cim_ref,
               d_ref, wglu_ref, o_ref, xr_scr, xi_scr, cin_scr, car_scr, *, lane_chunk):
    rows, d = x_ref.shape[1], x_ref.shape[2]
    n_state = xr_scr.shape[1]
    n_kt = bre_ref.shape[0]
    kw = bre_ref.shape[1]
    sw = bre_ref.shape[2]
    n_seg = SUBLANES
    seg = rows // n_seg

    @pl.when(pl.program_id(1) == 0)
    def _():
        car_scr[...] = jnp.zeros_like(car_scr)

    x = x_ref[0]
    h = _dot(pm_ref[...], _rms_rows(x, g_ref[...]).astype(BF16)).astype(BF16)
    u = _dot(h, win_ref[...])
    ub = u.astype(BF16)
    for kt in range(n_kt):
        uk = ub[:, kt * kw:(kt + 1) * kw]
        xr_scr[:, kt * sw:(kt + 1) * sw] = _dot(uk, bre_ref[kt])
        xi_scr[:, kt * sw:(kt + 1) * sw] = _dot(uk, bim_ref[kt])

    for lc in range(n_state // lane_chunk):
        ls = slice(lc * lane_chunk, (lc + 1) * lane_chunk)
        ar, ai = a1_ref[0, :, ls], a1_ref[1, :, ls]

        def step(r, carry, ls=ls, ar=ar, ai=ai):
            pr, pi = carry
            r0 = pl.multiple_of(r * SUBLANES, SUBLANES)
            nr = ar * pr - ai * pi + xr_scr[pl.ds(r0, SUBLANES), ls]
            ni = ar * pi + ai * pr + xi_scr[pl.ds(r0, SUBLANES), ls]
            xr_scr[pl.ds(r0, SUBLANES), ls] = nr
            xi_scr[pl.ds(r0, SUBLANES), ls] = ni
            return nr, ni

        zero = jnp.zeros((SUBLANES, lane_chunk), F32)
        lr, li = lax.fori_loop(0, seg, step, (zero, zero), unroll=4)

        sr, si = aseg_ref[0:1, ls], aseg_ref[1:2, ls]
        cr, ci = car_scr[0:1, ls], car_scr[1:2, ls]
        for s in range(n_seg):
            cin_scr[0, s:s + 1, ls] = cr
            cin_scr[1, s:s + 1, ls] = ci
            cr, ci = (sr * cr - si * ci + lr[s:s + 1], sr * ci + si * cr + li[s:s + 1])
        car_scr[0:1, ls] = cr
        car_scr[1:2, ls] = ci

        def fix(r, carry, ls=ls):
            r0 = pl.multiple_of(r * SUBLANES, SUBLANES)
            pr, pi = pw_ref[0, pl.ds(r, 1), ls], pw_ref[1, pl.ds(r, 1), ls]
            cr, ci = cin_scr[0, :, ls], cin_scr[1, :, ls]
            xr_scr[pl.ds(r0, SUBLANES), ls] += pr * cr - pi * ci
            xi_scr[pl.ds(r0, SUBLANES), ls] += pr * ci + pi * cr
            return carry

        lax.fori_loop(0, seg, fix, 0, unroll=4)

    ys = []
    for kt in range(n_kt):
        ss = slice(kt * sw, (kt + 1) * sw)
        ys.append(_dot(xr_scr[:, ss].astype(BF16), cre_ref[kt]) + _dot(xi_scr[:, ss].astype(BF16), cim_ref[kt]))
    y = jnp.concatenate(ys, axis=1) + d_ref[...] * u
    z = _dot(pmt_ref[...], jax.nn.gelu(y).astype(BF16)).astype(BF16)
    vg = _dot(z, wglu_ref[...])
    o_ref[0] = x + vg[:, :d] * jax.nn.sigmoid(vg[:, d:])


def _s5_tables(b_re, b_im, c_re, c_im, log_dt, a_re, a_im, seg):
    ng, ps, cg = b_re.shape
    per_tile = MXU_DIM // cg
    n_kt = ng // per_tile
    dt = jnp.exp(log_dt)[:, None]
    lam_r, lam_i = dt * a_re, dt * a_im
    mag = jnp.exp(lam_r)
    abar_r, abar_i = mag * jnp.cos(lam_i), mag * jnp.sin(lam_i)
    den = a_re * a_re + a_im * a_im
    coef_r = ((abar_r - 1.0) * a_re + abar_i * a_im) / den
    coef_i = (abar_i * a_re - (abar_r - 1.0) * a_im) / den
    bbar_r = coef_r[..., None] * b_re - coef_i[..., None] * b_im
    bbar_i = coef_r[..., None] * b_im + coef_i[..., None] * b_re
    eye = jnp.eye(per_tile, dtype=F32)
    bd_in = lambda m: jnp.einsum('kgpc,gh->kgchp', m.reshape(n_kt, per_tile, ps, cg), eye).reshape(
        n_kt, per_tile * cg, per_tile * ps).astype(BF16)
    bd_out = lambda m: jnp.einsum('kgcp,gh->kgphc', m.reshape(n_kt, per_tile, cg, ps), eye).reshape(
        n_kt, per_tile * ps, per_tile * cg).astype(BF16)

    def power(n):
        pm = jnp.exp(n * lam_r)
        return (pm * jnp.cos(n * lam_i)).reshape(-1), (pm * jnp.sin(n * lam_i)).reshape(-1)

    a1 = jnp.tile(jnp.stack(power(1.0))[:, None, :], (1, SUBLANES, 1))
    aseg = jnp.stack(power(float(seg)))
    steps = [power(float(i + 1)) for i in range(seg)]
    pw = jnp.stack([jnp.stack([p[0] for p in steps]), jnp.stack([p[1] for p in steps])])
    return bd_in(bbar_r), bd_in(bbar_i), a1, aseg, pw, bd_out(c_re), bd_out(-c_im)


def _s5_layer(x, g_norm, w_in, b_re, b_im, c_re, c_im, d_skip, log_dt, a_re, a_im, w_glu, rows=256):
    b, t, d = x.shape
    rows = min(rows, t)
    bre, bim, a1, aseg, pw, cre, cim = _s5_tables(b_re, b_im, c_re, c_im, log_dt, a_re, a_im, rows // SUBLANES)
    n_state = a1.shape[-1]
    full = lambda a: pl.BlockSpec(a.shape, lambda i, j: (0,) * a.ndim)
    seg = rows // SUBLANES
    new_row = jnp.arange(rows)
    pm = (jnp.arange(rows)[None, :] == (new_row % SUBLANES * seg + new_row // SUBLANES)[:, None]).astype(BF16)
    args = (g_norm[None, :], w_in.astype(BF16), pm, pm.T, bre, bim, a1, aseg, pw, cre, cim, d_skip[None, :],
            w_glu.astype(BF16))
    return pl.pallas_call(
        functools.partial(_s5_kernel, lane_chunk=1024), grid=(b, t // rows),
        in_specs=[pl.BlockSpec((1, rows, d), lambda i, j: (i, j, 0))] + [full(a) for a in args],
        out_specs=pl.BlockSpec((1, rows, d), lambda i, j: (i, j, 0)),
        out_shape=jax.ShapeDtypeStruct((b, t, d), F32),
        scratch_shapes=[pltpu.VMEM((rows, n_state), F32), pltpu.VMEM((rows, n_state), F32),
                        pltpu.VMEM((2, SUBLANES, n_state), F32), pltpu.VMEM((SUBLANES, n_state), F32)],
        compiler_params=_cparams("parallel", "arbitrary"), name="s5_mixer",
    )(x, *args)


def _pad_heads(w, n_heads, width):
    d = w.shape[0]
    return jnp.pad(w.reshape(d, n_heads, width), ((0, 0), (0, 0), (0, LANES - width))).reshape(d, n_heads * LANES)


def _nsa_layer(x, g_norm, w_in, w_out, q_gain, k_gain, cmp_pos, cmp_w1, cmp_b1, cmp_w2, cmp_b2):
    b, t, d = x.shape
    gn, rh, dh = N_KV_GROUPS, HEADS_PER_GROUP, HEAD_DIM
    assert t % SEL_CHUNK == 0 and t >= SEL_CHUNK + WINDOW
    x2 = x.reshape(b * t, d)
    scale = LOG2E / math.sqrt(dh)

    kvw = lambda i: w_in[:, d + i * KV_DIM:d + (i + 1) * KV_DIM]
    wq = _pad_heads(w_in[:, :d], N_HEADS, dh).astype(BF16)
    wk = jnp.concatenate([_pad_heads(kvw(i), gn, dh) for i in (2, 3, 4, 5)], axis=1).astype(BF16)
    wc = jnp.concatenate([kvw(0), kvw(1)], axis=1).astype(BF16)
    wg = _pad_heads(w_in[:, d + 6 * KV_DIM:], gn, rh * N_BRANCHES).astype(BF16)
    pad_gain = lambda v, n: jnp.tile(jnp.pad(v, (0, LANES - dh)), n)[None, :]
    qg_p = pad_gain(q_gain * scale, N_HEADS)
    kg_p = jnp.concatenate([pad_gain(k_gain[1], gn), pad_gain(k_gain[2], gn)], axis=0)

    q, ka, va, kw, vw, kc_raw, vc_raw, gates = _nsa_proj(x2, b, t, g_norm[None, :], wq, wk, wc, wg, qg_p, kg_p)

    tc = t // CMP_STRIDE
    half = CMP_STRIDE * dh
    blk = lambda a: a.reshape(b, tc, CMP_STRIDE, gn, dh).transpose(0, 3, 1, 2, 4).reshape(b, gn, tc, half)
    comp = []
    for i, (raw, norm) in enumerate(((kc_raw, True), (vc_raw, False))):
        w1 = cmp_w1[i].astype(BF16)
        comp.append(_compress(blk(raw), cmp_pos[i].reshape(2, half), w1[:half], w1[half:], cmp_b1[i][None, :],
                              jnp.pad(cmp_w2[i], ((0, 0), (0, LANES - dh))).astype(BF16),
                              jnp.pad(cmp_b2[i], (0, LANES - dh))[None, :],
                              jnp.pad(k_gain[0], (0, LANES - dh))[None, :], norm))
    kc, vc = comp
    vct = vc.transpose(0, 1, 3, 2)

    nsb = t // SEL_BLOCK
    n_i = jnp.arange(tc)
    j_i = jnp.arange(nsb)
    n_cmp = (t - CMP_BLOCK) // CMP_STRIDE + 1
    ovt = ((n_i[None, :] * CMP_STRIDE < (j_i[:, None] + 1) * SEL_BLOCK)
           & (n_i[None, :] * CMP_STRIDE + CMP_BLOCK > j_i[:, None] * SEL_BLOCK)
           & (n_i[None, :] < n_cmp)).astype(BF16)
    ocmp, selt = _cmp_select(q, kc, vct, ovt)

    vrows = dh + 2 * SUBLANES
    vat = va[..., :vrows].transpose(0, 1, 3, 2)
    vwt = vw[..., :vrows].transpose(0, 1, 3, 2)
    o = _sel_win(q, ka, vat, kw, vwt, selt, ocmp, gates)
    return o.reshape(b * t, d), w_out.astype(BF16)


def kernel(x, mix_norm, ffn_norm, nsa_w_in, nsa_w_out, nsa_q_gain, nsa_k_gain, nsa_cmp_pos, nsa_cmp_w1, nsa_cmp_b1, nsa_cmp_w2, nsa_cmp_b2, s5_w_in, s5_b_re, s5_b_im, s5_c_re, s5_c_im, s5_d, s5_log_dt, s5_a_re, s5_a_im, s5_w_glu, ffn_w_gate, ffn_w_up, ffn_w_down):
    b, t, d = x.shape
    depth = mix_norm.shape[0]
    for layer in range(depth):
        i = layer // 2
        attn = None
        if layer % 2 == 0:
            attn = _nsa_layer(x, mix_norm[layer], nsa_w_in[i], nsa_w_out[i], nsa_q_gain[i], nsa_k_gain[i],
                              nsa_cmp_pos[i], nsa_cmp_w1[i], nsa_cmp_b1[i], nsa_cmp_w2[i], nsa_cmp_b2[i])
        else:
            x = _s5_layer(x, mix_norm[layer], s5_w_in[i], s5_b_re[i], s5_b_im[i], s5_c_re[i], s5_c_im[i],
                          s5_d[i], s5_log_dt[i], s5_a_re[i], s5_a_im[i], s5_w_glu[i])
        x = _ffn(x.reshape(b * t, d), ffn_norm[layer][None, :], ffn_w_gate[layer].astype(BF16),
                 ffn_w_up[layer].astype(BF16), ffn_w_down[layer].astype(BF16), attn=attn).reshape(b, t, d)
    return x
```

```python
import functools
import math

import jax
import jax.numpy as jnp
from jax import lax
from jax.experimental import pallas as pl
from jax.experimental.pallas import tpu as pltpu

F32 = jnp.float32
BF16 = jnp.bfloat16

EPS = 1e-6
NEG = -1e30
MASK_BIAS = -1e9
LOG2E = 1.4426950408889634

N_HEADS = 16
HEAD_DIM = 64
N_KV_GROUPS = 4
HEADS_PER_GROUP = 4
KV_DIM = N_KV_GROUPS * HEAD_DIM
CMP_BLOCK = 32
CMP_STRIDE = 16
SEL_BLOCK = 64
SEL_TOPN = 16
N_FORCED = 3
WINDOW = 512
N_BRANCHES = 3
S5_GROUP = 16
S5_STATE = 64

LANES = 128
SUBLANES = 8
MXU_DIM = 256
SEL_CHUNK = 512
V7X_VMEM_LIMIT_BYTES = 56 * 1024 * 1024


def _cparams(*semantics):
    return pltpu.CompilerParams(dimension_semantics=semantics,
                                vmem_limit_bytes=V7X_VMEM_LIMIT_BYTES)


def _rms_rows(x, g):
    var = jnp.mean(x * x, axis=-1, keepdims=True)
    return x * lax.rsqrt(var + EPS) * g


def _dot(a, b):
    return jnp.dot(a, b, preferred_element_type=F32)


def _dot_nt(a, b):
    return lax.dot_general(a, b, (((1,), (1,)), ((), ())), preferred_element_type=F32)


def _nsa_proj_kernel(x_ref, g_ref, wq_ref, wk_ref, wc_ref, wg_ref, qg_ref, kg_ref,
                     q_ref, ka_ref, va_ref, kw_ref, vw_ref, kcr_ref, vcr_ref, gate_ref):
    tm = x_ref.shape[0]
    h = _rms_rows(x_ref[...], g_ref[...]).astype(BF16)
    inv_dh = 1.0 / HEAD_DIM

    def head_norm(v, gain):
        var = jnp.sum(v * v, axis=-1, keepdims=True) * inv_dh
        return v * lax.rsqrt(var + EPS) * gain

    q = _dot(h, wq_ref[...])
    qg = qg_ref[...]
    for hd in range(N_HEADS):
        sl = slice(hd * LANES, (hd + 1) * LANES)
        g, r = divmod(hd, HEADS_PER_GROUP)
        q_ref[0, g, r] = head_norm(q[:, sl], qg[:, sl]).astype(BF16)

    row = lax.rem(pl.program_id(0) * tm, SEL_CHUNK) + lax.broadcasted_iota(jnp.int32, (tm, 1), 0)
    lane = lax.broadcasted_iota(jnp.int32, (1, LANES), 1)
    blk_flag = ((row // SEL_BLOCK) == (lane - HEAD_DIM)).astype(F32)
    one_flag = (lane == HEAD_DIM).astype(F32)

    kv = _dot(h, wk_ref[...])
    kg = kg_ref[...]
    plan = ((ka_ref, 0, blk_flag), (va_ref, None, one_flag), (kw_ref, 1, None), (vw_ref, None, one_flag))
    for part, (ref, gain_row, flag) in enumerate(plan):
        for g in range(N_KV_GROUPS):
            piece = kv[:, (part * N_KV_GROUPS + g) * LANES:(part * N_KV_GROUPS + g + 1) * LANES]
            if gain_row is not None:
                piece = head_norm(piece, kg[gain_row:gain_row + 1, g * LANES:(g + 1) * LANES])
            if flag is not None:
                piece = piece + flag
            ref[0, g] = piece.astype(BF16)

    c = _dot(h, wc_ref[...])
    kcr_ref[...] = c[:, :KV_DIM]
    vcr_ref[...] = c[:, KV_DIM:]
    gt = jax.nn.sigmoid(_dot(h, wg_ref[...]))
    for g in range(N_KV_GROUPS):
        gate_ref[0, g] = gt[:, g * LANES:(g + 1) * LANES]


def _nsa_proj(x2, b, t, g, wq, wk, wc, wg, qg_p, kg_p, tm=512):
    m, d = x2.shape
    assert SEL_CHUNK % tm == 0 and t % SEL_CHUNK == 0
    nt = t // tm
    gn, rh = N_KV_GROUPS, HEADS_PER_GROUP
    full = lambda a: pl.BlockSpec(a.shape, lambda i: (0,) * a.ndim)
    row = lambda n: pl.BlockSpec((tm, n), lambda i: (i, 0))
    head = pl.BlockSpec((1, gn, tm, LANES), lambda i: (i // nt, 0, i % nt, 0))
    outs = [jax.ShapeDtypeStruct((b, gn, rh, t, LANES), BF16)]
    outs += [jax.ShapeDtypeStruct((b, gn, t, LANES), BF16)] * 4
    outs += [jax.ShapeDtypeStruct((m, KV_DIM), F32)] * 2
    outs += [jax.ShapeDtypeStruct((b, gn, t, LANES), F32)]
    return pl.pallas_call(
        _nsa_proj_kernel, grid=(m // tm,),
        in_specs=[row(d), full(g), full(wq), full(wk), full(wc), full(wg), full(qg_p), full(kg_p)],
        out_specs=[pl.BlockSpec((1, gn, rh, tm, LANES), lambda i: (i // nt, 0, 0, i % nt, 0)),
                   head, head, head, head, row(KV_DIM), row(KV_DIM), head],
        out_shape=outs, compiler_params=_cparams("parallel"), name="nsa_proj",
    )(x2, g, wq, wk, wc, wg, qg_p, kg_p)


def _compress_kernel(a_ref, pos_ref, w1a_ref, w1b_ref, b1_ref, w2_ref, b2_ref, gain_ref, o_ref, *, norm):
    a = a_ref[0, 0]
    tc = a.shape[0]
    pos = pos_ref[...]
    h1 = _dot((a + pos[0:1]).astype(BF16), w1a_ref[...])
    h2 = _dot((a + pos[1:2]).astype(BF16), w1b_ref[...])
    h = h1 + pltpu.roll(h2, tc - 1, 0) + b1_ref[...]
    hid = jax.nn.gelu(h).astype(BF16)
    out = _dot(hid, w2_ref[...]) + b2_ref[...]
    if norm:
        var = jnp.sum(out * out, axis=-1, keepdims=True) * (1.0 / HEAD_DIM)
        out = out * lax.rsqrt(var + EPS) * gain_ref[...]
    o_ref[0, 0] = out.astype(o_ref.dtype)


def _compress(a, pos2, w1a, w1b, b1, w2, b2, gain, norm):
    b, g, tc, w = a.shape
    full = lambda z: pl.BlockSpec(z.shape, lambda i, j: (0,) * z.ndim)
    return pl.pallas_call(
        functools.partial(_compress_kernel, norm=norm), grid=(b, g),
        in_specs=[pl.BlockSpec((1, 1, tc, w), lambda i, j: (i, j, 0, 0)),
                  full(pos2), full(w1a), full(w1b), full(b1), full(w2), full(b2), full(gain)],
        out_specs=pl.BlockSpec((1, 1, tc, LANES), lambda i, j: (i, j, 0, 0)),
        out_shape=jax.ShapeDtypeStruct((b, g, tc, LANES), BF16),
        compiler_params=_cparams("parallel", "parallel"), name="nsa_compress",
    )(a, pos2, w1a, w1b, b1, w2, b2, gain)


def _cmp_select_kernel(q_ref, kc_ref, vct_ref, ovt_ref, ocmp_ref, sel_ref, bias_scr, *, top_n, n_tiers):
    tq = q_ref.shape[3]
    tc = kc_ref.shape[2]
    nsb = ovt_ref.shape[0]
    n_tiles = pl.num_programs(1)
    t = pl.program_id(1) * tq + lax.broadcasted_iota(jnp.int32, (1, tq), 1)
    cur = t // SEL_BLOCK
    any_vis = t >= CMP_BLOCK - 1
    n_rest = max(top_n - N_FORCED, 0)

    def tile_body(rows, blocks):
        n_idx = lax.broadcasted_iota(jnp.int32, (rows, 1), 0)
        bias = jnp.where((n_idx * CMP_STRIDE + (CMP_BLOCK - 1)) <= t, 0.0, NEG)
        j_idx = lax.broadcasted_iota(jnp.int32, (blocks, 1), 0)
        forced = (j_idx == 0) | (j_idx == cur) | (j_idx == cur - 1)
        valid = j_idx <= cur
        j_f = j_idx.astype(F32)
        bias_scr[:rows, :] = bias

        def group_body(g, carry):
            kc = kc_ref[0, g, :rows, :]
            vct = vct_ref[0, g, :, :rows]
            psum = jnp.zeros((rows, tq), F32)
            scores = []
            for r in range(HEADS_PER_GROUP):
                qr = q_ref[0, g, r]
                scores.append(jnp.concatenate([_dot_nt(kc[:rows // 2], qr), _dot_nt(kc[rows // 2:], qr)], axis=0))
            for r in range(HEADS_PER_GROUP):
                s = scores[r] + bias_scr[:rows, :]
                e = jnp.exp2(s - jnp.max(s, axis=0, keepdims=True))
                den = jnp.sum(e, axis=0, keepdims=True)
                p = e * jnp.where(any_vis, 1.0 / den, 0.0)
                pb = p.astype(BF16)
                o_t = (_dot(vct[:, :rows // 2], pb[:rows // 2])
                       + _dot(vct[:, rows // 2:], pb[rows // 2:]))
                ocmp_ref[0, g, r] = o_t[:HEAD_DIM]
                psum = psum + p
            hi = psum.astype(BF16)
            lo = (psum - hi.astype(F32)).astype(BF16)
            ovt = ovt_ref[:blocks, :rows]
            psel = _dot(ovt, hi) + _dot(ovt, lo)
            score = jnp.where(valid & ~forced, psel, NEG)
            for _ in range(n_rest):
                mx = jnp.max(score, axis=0, keepdims=True)
                first = jnp.min(jnp.where(score == mx, j_f, float(nsb)), axis=0, keepdims=True)
                score = jnp.where(j_f == first, -jnp.inf, score)
            sel_ref[0, g, :blocks, :] = jnp.where(forced | ((score == -jnp.inf) & valid), 1.0, 0.0)
            if blocks < nsb:
                sel_ref[0, g, blocks:, :] = jnp.zeros((nsb - blocks, tq), F32)
            return carry

        lax.fori_loop(0, N_KV_GROUPS, group_body, 0)

    tier = (pl.program_id(1) * n_tiers) // n_tiles
    for k in range(n_tiers):
        pl.when(tier == k)(functools.partial(tile_body, tc * (k + 1) // n_tiers, nsb * (k + 1) // n_tiers))


def _cmp_select(q, kc, vct, ovt, tq=256):
    b, g, r, t_len, _ = q.shape
    tc = kc.shape[2]
    nsb = ovt.shape[0]
    tq = min(tq, t_len)
    top_n = min(SEL_TOPN, nsb)
    assert top_n >= N_FORCED
    n_tiers = 4
    assert (t_len // tq) % n_tiers == 0 and nsb % (n_tiers * SUBLANES) == 0 and tc % (n_tiers * LANES // 4) == 0
    return pl.pallas_call(
        functools.partial(_cmp_select_kernel, top_n=top_n, n_tiers=n_tiers), grid=(b, t_len // tq),
        in_specs=[pl.BlockSpec((1, g, r, tq, LANES), lambda i, j: (i, 0, 0, j, 0)),
                  pl.BlockSpec((1, g, tc, LANES), lambda i, j: (i, 0, 0, 0)),
                  pl.BlockSpec((1, g, LANES, tc), lambda i, j: (i, 0, 0, 0)),
                  pl.BlockSpec(ovt.shape, lambda i, j: (0, 0))],
        out_specs=[pl.BlockSpec((1, g, r, HEAD_DIM, tq), lambda i, j: (i, 0, 0, 0, j)),
                   pl.BlockSpec((1, g, nsb, tq), lambda i, j: (i, 0, 0, j))],
        out_shape=[jax.ShapeDtypeStruct((b, g, r, HEAD_DIM, t_len), F32),
                   jax.ShapeDtypeStruct((b, g, nsb, t_len), F32)],
        scratch_shapes=[pltpu.VMEM((tc, tq), F32)],
        compiler_params=_cparams("parallel", "parallel"), name="nsa_cmp_select",
    )(q, kc, vct, ovt)


def _sel_win_kernel(q_ref, ka_ref, vat_ref, kw_ref, vwt_ref, selt_ref, ocmp_ref, gate_ref, o_ref,
                    qa_scr, sa_scr, sb_scr, ma_scr, mb_scr, *, kc):
    r_heads, tq = q_ref.shape[2], q_ref.shape[3]
    dh = HEAD_DIM
    t_len = ka_ref.shape[2]
    per_chunk = kc // SEL_BLOCK
    aug_rows = -(-per_chunk // (2 * SUBLANES)) * (2 * SUBLANES)
    t0 = pl.program_id(2) * tq
    t_lane = t0 + lax.broadcasted_iota(jnp.int32, (1, tq), 1)

    for r in range(r_heads):
        qa_scr[:, r * tq:(r + 1) * tq] = q_ref[0, 0, r].astype(F32).T.astype(BF16)

    def softmax_t(s, m_new):
        return jnp.exp2(s - m_new).astype(BF16)

    span = min(tq + WINDOW, t_len)
    w0 = pl.multiple_of(jnp.maximum(t0 + tq - span, 0), tq)
    kpos_w = w0 + lax.broadcasted_iota(jnp.int32, (span, 1), 0)
    in_win = (kpos_w <= t_lane) & (t_lane - kpos_w < WINDOW)
    bias_w = jnp.where(in_win, 0.0, MASK_BIAS)
    s_win = _dot(kw_ref[0, 0, pl.ds(w0, span), :], qa_scr[...]) + jnp.concatenate([bias_w] * r_heads, axis=1)

    n_full = t0 // kc
    last_full = jnp.maximum(n_full - 1, 0)

    def stash_steps(c, s_scr, m_scr, diagonal=False):
        if not diagonal:
            c = jnp.where(c < n_full, c, n_full + 1)
        ci = jnp.minimum(c, n_full)

        def write_mask_rows():
            sel8 = selt_ref[0, 0, pl.ds(pl.multiple_of(ci * per_chunk, per_chunk), per_chunk), :]
            sb = (jnp.where(c <= n_full, sel8, 0.0) - 1.0) * (-MASK_BIAS)
            if aug_rows > per_chunk:
                sb = jnp.concatenate([sb, jnp.zeros((aug_rows - per_chunk, tq), F32)], axis=0)
            qa_scr[dh:dh + aug_rows, :] = jnp.concatenate([sb.astype(BF16)] * r_heads, axis=1)

        def head(n):
            cs = slice(n * tq, (n + 1) * tq)
            s = _dot(ka_ref[0, 0, pl.ds(pl.multiple_of(ci * kc, kc), kc), :], qa_scr[:, cs])
            s_scr[:, cs] = s
            m_scr[0:1, cs] = jnp.max(s, axis=0, keepdims=True)

        return write_mask_rows, head

    def update_head(n, s_scr, m_scr, c, m, acc):
        cs = slice(n * tq, (n + 1) * tq)
        ci = jnp.minimum(c, n_full)
        m_new = jnp.maximum(m, m_scr[0:1, cs])
        pv = _dot(vat_ref[0, 0, :, pl.ds(pl.multiple_of(ci * kc, kc), kc)], softmax_t(s_scr[:, cs], m_new))
        return m_new, jnp.exp2(m - m_new) * acc + pv

    def half_trip(c_next, s_next, m_next, c_cur, s_cur, m_cur, state):
        mask_rows, stash_head = stash_steps(c_next, s_next, m_next)
        mask_rows()
        state = list(state)
        for n in range(r_heads):
            stash_head(n)
            state[n] = update_head(n, s_cur, m_cur, c_cur, *state[n])
        return state

    def stash(c, s_scr, m_scr):
        mask_rows, stash_head = stash_steps(c, s_scr, m_scr)
        mask_rows()
        for n in range(r_heads):
            stash_head(n)

    mask_rows, _ = stash_steps(n_full, sa_scr, ma_scr, diagonal=True)
    mask_rows()
    kd0 = pl.multiple_of(n_full * kc, kc)
    causal = jnp.where(kd0 + lax.broadcasted_iota(jnp.int32, (kc, 1), 0) <= t_lane, 0.0, MASK_BIAS)
    s_diag = _dot(ka_ref[0, 0, pl.ds(kd0, kc), :], qa_scr[...]) + jnp.concatenate([causal] * r_heads, axis=1)

    ow = _dot(vwt_ref[0, 0, :, pl.ds(w0, span)], softmax_t(s_win, jnp.max(s_win, axis=0, keepdims=True)))
    o_win_t = ow[:dh] / ow[dh:dh + 1]
    stash(0, sa_scr, ma_scr)
    m0 = jnp.max(s_diag, axis=0, keepdims=True)
    acc0 = _dot(vat_ref[0, 0, :, pl.ds(kd0, kc)], softmax_t(s_diag, m0))
    state = [(m0[:, n * tq:(n + 1) * tq], acc0[:, n * tq:(n + 1) * tq]) for n in range(r_heads)]
    flat = lambda st: tuple(x for pair in st for x in pair)
    unflat = lambda fl: [(fl[2 * n], fl[2 * n + 1]) for n in range(r_heads)]

    def body(i, fl):
        c = 2 * i
        st = half_trip(c + 1, sb_scr, mb_scr, jnp.minimum(c, last_full), sa_scr, ma_scr, unflat(fl))
        st = half_trip(c + 2, sa_scr, ma_scr, jnp.minimum(c + 1, last_full), sb_scr, mb_scr, st)
        return flat(st)

    fl = lax.fori_loop(0, n_full // 2, body, flat(state))

    def tail(fl):
        st = unflat(fl)
        return flat([update_head(n, sa_scr, ma_scr, last_full, *st[n]) for n in range(r_heads)])

    fl = lax.cond(n_full % 2 == 1, tail, lambda fl: fl, fl)
    acc = jnp.concatenate([fl[2 * n + 1] for n in range(r_heads)], axis=1)
    o_sel_t = acc[:dh] / acc[dh:dh + 1]

    gates_t = gate_ref[0, 0].T
    parts = []
    for r in range(r_heads):
        cs = slice(r * tq, (r + 1) * tq)
        parts.append(gates_t[3 * r:3 * r + 1] * ocmp_ref[0, 0, r]
                     + gates_t[3 * r + 1:3 * r + 2] * o_sel_t[:, cs]
                     + gates_t[3 * r + 2:3 * r + 3] * o_win_t[:, cs])
    o_ref[0] = jnp.concatenate(parts, axis=0).T.astype(o_ref.dtype)


def _sel_win(q, ka, vat, kw, vwt, selt, ocmp, gates, tq=256, kc=SEL_CHUNK):
    b, g, r, t_len, _ = q.shape
    nsb = selt.shape[2]
    vrows = vat.shape[2]
    gw = r * HEAD_DIM
    k_spec = pl.BlockSpec((1, 1, t_len, LANES), lambda i, j, k: (i, j, 0, 0))
    vt_spec = pl.BlockSpec((1, 1, vrows, t_len), lambda i, j, k: (i, j, 0, 0))
    return pl.pallas_call(
        functools.partial(_sel_win_kernel, kc=kc), grid=(b, g, t_len // tq),
        in_specs=[pl.BlockSpec((1, 1, r, tq, LANES), lambda i, j, k: (i, j, 0, k, 0)),
                  k_spec, vt_spec, k_spec, vt_spec,
                  pl.BlockSpec((1, 1, nsb, tq), lambda i, j, k: (i, j, 0, k)),
                  pl.BlockSpec((1, 1, r, HEAD_DIM, tq), lambda i, j, k: (i, j, 0, 0, k)),
                  pl.BlockSpec((1, 1, tq, LANES), lambda i, j, k: (i, j, k, 0))],
        out_specs=pl.BlockSpec((1, tq, gw), lambda i, j, k: (i, k, j)),
        out_shape=jax.ShapeDtypeStruct((b, t_len, g * gw), BF16),
        scratch_shapes=[pltpu.VMEM((LANES, r * tq), BF16),
                        pltpu.VMEM((kc, r * tq), F32), pltpu.VMEM((kc, r * tq), F32),
                        pltpu.VMEM((SUBLANES, r * tq), F32), pltpu.VMEM((SUBLANES, r * tq), F32)],
        compiler_params=_cparams("parallel", "parallel", "arbitrary"), name="nsa_sel_win",
    )(q, ka, vat, kw, vwt, selt, ocmp, gates)


def _ffn_tile(x, g_ref, wg_ref, wu_ref, wd_ref, hc):
    h = _rms_rows(x, g_ref[...]).astype(BF16)
    acc = x
    for c in range(wg_ref.shape[1] // hc):
        sl = slice(c * hc, (c + 1) * hc)
        a = jax.nn.silu(_dot(h, wg_ref[:, sl])) * _dot(h, wu_ref[:, sl])
        acc = acc + _dot(a.astype(BF16), wd_ref[sl, :])
    return acc


def _ffn_kernel(x_ref, g_ref, wg_ref, wu_ref, wd_ref, o_ref, *, hc):
    o_ref[...] = _ffn_tile(x_ref[...], g_ref, wg_ref, wu_ref, wd_ref, hc)


def _attn_out_ffn_kernel(a_ref, wo_ref, x_ref, g_ref, wg_ref, wu_ref, wd_ref, o_ref, *, hc):
    x = x_ref[...] + _dot(a_ref[...], wo_ref[...])
    o_ref[...] = _ffn_tile(x, g_ref, wg_ref, wu_ref, wd_ref, hc)


def _ffn(x2, g, wg, wu, wd, attn=None, tm=512, hc=MXU_DIM):
    m, d = x2.shape
    full = lambda a: pl.BlockSpec(a.shape, lambda i: (0,) * a.ndim)
    row = pl.BlockSpec((tm, d), lambda i: (i, 0))
    if attn is None:
        body, pre, pre_specs = _ffn_kernel, (), []
    else:
        body, pre = _attn_out_ffn_kernel, attn
        pre_specs = [pl.BlockSpec((tm, attn[0].shape[1]), lambda i: (i, 0)), full(attn[1])]
    return pl.pallas_call(
        functools.partial(body, hc=hc), grid=(m // tm,),
        in_specs=pre_specs + [row, full(g), full(wg), full(wu), full(wd)],
        out_specs=row, out_shape=jax.ShapeDtypeStruct((m, d), F32),
        compiler_params=_cparams("parallel"), name="ffn_swiglu",
    )(*pre, x2, g, wg, wu, wd)


def _s5_kernel(x_ref, g_ref, win_ref, pm_ref, pmt_ref, bre_ref, bim_ref, a1_ref, aseg_ref, pw_ref, cre_ref, cim_ref,
               d_ref, wglu_ref, o_ref, xr_scr, xi_scr, cin_scr, car_scr, *, lane_chunk):
    rows, d = x_ref.shape[1], x_ref.shape[2]
    n_state = xr_scr.shape[1]
    n_kt = bre_ref.shape[0]
    kw = bre_ref.shape[1]
    sw = bre_ref.shape[2]
    n_seg = SUBLANES
    seg = rows // n_seg

    @pl.when(pl.program_id(1) == 0)
    def _():
        car_scr[...] = jnp.zeros_like(car_scr)

    x = x_ref[0]
    h = _dot(pm_ref[...], _rms_rows(x, g_ref[...]).astype(BF16)).astype(BF16)
    u = _dot(h, win_ref[...])
    ub = u.astype(BF16)
    for kt in range(n_kt):
        uk = ub[:, kt * kw:(kt + 1) * kw]
        xr_scr[:, kt * sw:(kt + 1) * sw] = _dot(uk, bre_ref[kt])
        xi_scr[:, kt * sw:(kt + 1) * sw] = _dot(uk, bim_ref[kt])

    for lc in range(n_state // lane_chunk):
        ls = slice(lc * lane_chunk, (lc + 1) * lane_chunk)
        ar, ai = a1_ref[0, :, ls], a1_ref[1, :, ls]

        def step(r, carry, ls=ls, ar=ar, ai=ai):
            pr, pi = carry
            r0 = pl.multiple_of(r * SUBLANES, SUBLANES)
            nr = ar * pr - ai * pi + xr_scr[pl.ds(r0, SUBLANES), ls]
            ni = ar * pi + ai * pr + xi_scr[pl.ds(r0, SUBLANES), ls]
            xr_scr[pl.ds(r0, SUBLANES), ls] = nr
            xi_scr[pl.ds(r0, SUBLANES), ls] = ni
            return nr, ni

        zero = jnp.zeros((SUBLANES, lane_chunk), F32)
        lr, li = lax.fori_loop(0, seg, step, (zero, zero), unroll=True)

        sr, si = aseg_ref[0:1, ls], aseg_ref[1:2, ls]
        cr, ci = car_scr[0:1, ls], car_scr[1:2, ls]
        for s in range(n_seg):
            cin_scr[0, s:s + 1, ls] = cr
            cin_scr[1, s:s + 1, ls] = ci
            cr, ci = (sr * cr - si * ci + lr[s:s + 1], sr * ci + si * cr + li[s:s + 1])
        car_scr[0:1, ls] = cr
        car_scr[1:2, ls] = ci

        def fix(r, carry, ls=ls):
            r0 = pl.multiple_of(r * SUBLANES, SUBLANES)
            pr, pi = pw_ref[0, pl.ds(r, 1), ls], pw_ref[1, pl.ds(r, 1), ls]
            cr, ci = cin_scr[0, :, ls], cin_scr[1, :, ls]
            xr_scr[pl.ds(r0, SUBLANES), ls] += pr * cr - pi * ci
            xi_scr[pl.ds(r0, SUBLANES), ls] += pr * ci + pi * cr
            return carry

        lax.fori_loop(0, seg, fix, 0, unroll=True)

    ys = []
    for kt in range(n_kt):
        ss = slice(kt * sw, (kt + 1) * sw)
        ys.append(_dot(xr_scr[:, ss].astype(BF16), cre_ref[kt]) + _dot(xi_scr[:, ss].astype(BF16), cim_ref[kt]))
    y = jnp.concatenate(ys, axis=1) + d_ref[...] * u
    z = _dot(pmt_ref[...], jax.nn.gelu(y).astype(BF16)).astype(BF16)
    vg = _dot(z, wglu_ref[...])
    o_ref[0] = x + vg[:, :d] * jax.nn.sigmoid(vg[:, d:])


def _s5_tables(b_re, b_im, c_re, c_im, log_dt, a_re, a_im, seg):
    ng, ps, cg = b_re.shape
    per_tile = MXU_DIM // cg
    n_kt = ng // per_tile
    dt = jnp.exp(log_dt)[:, None]
    lam_r, lam_i = dt * a_re, dt * a_im
    mag = jnp.exp(lam_r)
    abar_r, abar_i = mag * jnp.cos(lam_i), mag * jnp.sin(lam_i)
    den = a_re * a_re + a_im * a_im
    coef_r = ((abar_r - 1.0) * a_re + abar_i * a_im) / den
    coef_i = (abar_i * a_re - (abar_r - 1.0) * a_im) / den
    bbar_r = coef_r[..., None] * b_re - coef_i[..., None] * b_im
    bbar_i = coef_r[..., None] * b_im + coef_i[..., None] * b_re
    eye = jnp.eye(per_tile, dtype=F32)
    bd_in = lambda m: jnp.einsum('kgpc,gh->kgchp', m.reshape(n_kt, per_tile, ps, cg), eye).reshape(
        n_kt, per_tile * cg, per_tile * ps).astype(BF16)
    bd_out = lambda m: jnp.einsum('kgcp,gh->kgphc', m.reshape(n_kt, per_tile, cg, ps), eye).reshape(
        n_kt, per_tile * ps, per_tile * cg).astype(BF16)

    def power(n):
        pm = jnp.exp(n * lam_r)
        return (pm * jnp.cos(n * lam_i)).reshape(-1), (pm * jnp.sin(n * lam_i)).reshape(-1)

    a1 = jnp.tile(jnp.stack(power(1.0))[:, None, :], (1, SUBLANES, 1))
    aseg = jnp.stack(power(float(seg)))
    steps = [power(float(i + 1)) for i in range(seg)]
    pw = jnp.stack([jnp.stack([p[0] for p in steps]), jnp.stack([p[1] for p in steps])])
    return bd_in(bbar_r), bd_in(bbar_i), a1, aseg, pw, bd_out(c_re), bd_out(-c_im)


def _s5_layer(x, g_norm, w_in, b_re, b_im, c_re, c_im, d_skip, log_dt, a_re, a_im, w_glu, rows=256):
    b, t, d = x.shape
    rows = min(rows, t)
    bre, bim, a1, aseg, pw, cre, cim = _s5_tables(b_re, b_im, c_re, c_im, log_dt, a_re, a_im, rows // SUBLANES)
    n_state = a1.shape[-1]
    full = lambda a: pl.BlockSpec(a.shape, lambda i, j: (0,) * a.ndim)
    seg = rows // SUBLANES
    new_row = jnp.arange(rows)
    pm = (jnp.arange(rows)[None, :] == (new_row % SUBLANES * seg + new_row // SUBLANES)[:, None]).astype(BF16)
    args = (g_norm[None, :], w_in.astype(BF16), pm, pm.T, bre, bim, a1, aseg, pw, cre, cim, d_skip[None, :],
            w_glu.astype(BF16))
    return pl.pallas_call(
        functools.partial(_s5_kernel, lane_chunk=512), grid=(b, t // rows),
        in_specs=[pl.BlockSpec((1, rows, d), lambda i, j: (i, j, 0))] + [full(a) for a in args],
        out_specs=pl.BlockSpec((1, rows, d), lambda i, j: (i, j, 0)),
        out_shape=jax.ShapeDtypeStruct((b, t, d), F32),
        scratch_shapes=[pltpu.VMEM((rows, n_state), F32), pltpu.VMEM((rows, n_state), F32),
                        pltpu.VMEM((2, SUBLANES, n_state), F32), pltpu.VMEM((SUBLANES, n_state), F32)],
        compiler_params=_cparams("parallel", "arbitrary"), name="s5_mixer",
    )(x, *args)


def _pad_heads(w, n_heads, width):
    d = w.shape[0]
    return jnp.pad(w.reshape(d, n_heads, width), ((0, 0), (0, 0), (0, LANES - width))).reshape(d, n_heads * LANES)


def _nsa_layer(x, g_norm, w_in, w_out, q_gain, k_gain, cmp_pos, cmp_w1, cmp_b1, cmp_w2, cmp_b2):
    b, t, d = x.shape
    gn, rh, dh = N_KV_GROUPS, HEADS_PER_GROUP, HEAD_DIM
    assert t % SEL_CHUNK == 0 and t >= SEL_CHUNK + WINDOW
    x2 = x.reshape(b * t, d)
    scale = LOG2E / math.sqrt(dh)

    kvw = lambda i: w_in[:, d + i * KV_DIM:d + (i + 1) * KV_DIM]
    wq = _pad_heads(w_in[:, :d], N_HEADS, dh).astype(BF16)
    wk = jnp.concatenate([_pad_heads(kvw(i), gn, dh) for i in (2, 3, 4, 5)], axis=1).astype(BF16)
    wc = jnp.concatenate([kvw(0), kvw(1)], axis=1).astype(BF16)
    wg = _pad_heads(w_in[:, d + 6 * KV_DIM:], gn, rh * N_BRANCHES).astype(BF16)
    pad_gain = lambda v, n: jnp.tile(jnp.pad(v, (0, LANES - dh)), n)[None, :]
    qg_p = pad_gain(q_gain * scale, N_HEADS)
    kg_p = jnp.concatenate([pad_gain(k_gain[1], gn), pad_gain(k_gain[2], gn)], axis=0)

    q, ka, va, kw, vw, kc_raw, vc_raw, gates = _nsa_proj(x2, b, t, g_norm[None, :], wq, wk, wc, wg, qg_p, kg_p)

    tc = t // CMP_STRIDE
    half = CMP_STRIDE * dh
    blk = lambda a: a.reshape(b, tc, CMP_STRIDE, gn, dh).transpose(0, 3, 1, 2, 4).reshape(b, gn, tc, half)
    comp = []
    for i, (raw, norm) in enumerate(((kc_raw, True), (vc_raw, False))):
        w1 = cmp_w1[i].astype(BF16)
        comp.append(_compress(blk(raw), cmp_pos[i].reshape(2, half), w1[:half], w1[half:], cmp_b1[i][None, :],
                              jnp.pad(cmp_w2[i], ((0, 0), (0, LANES - dh))).astype(BF16),
                              jnp.pad(cmp_b2[i], (0, LANES - dh))[None, :],
                              jnp.pad(k_gain[0], (0, LANES - dh))[None, :], norm))
    kc, vc = comp
    vct = vc.transpose(0, 1, 3, 2)

    nsb = t // SEL_BLOCK
    n_i = jnp.arange(tc)
    j_i = jnp.arange(nsb)
    n_cmp = (t - CMP_BLOCK) // CMP_STRIDE + 1
    ovt = ((n_i[None, :] * CMP_STRIDE < (j_i[:, None] + 1) * SEL_BLOCK)
           & (n_i[None, :] * CMP_STRIDE + CMP_BLOCK > j_i[:, None] * SEL_BLOCK)
           & (n_i[None, :] < n_cmp)).astype(BF16)
    ocmp, selt = _cmp_select(q, kc, vct, ovt)

    vrows = dh + 2 * SUBLANES
    vat = va[..., :vrows].transpose(0, 1, 3, 2)
    vwt = vw[..., :vrows].transpose(0, 1, 3, 2)
    o = _sel_win(q, ka, vat, kw, vwt, selt, ocmp, gates)
    return o.reshape(b * t, d), w_out.astype(BF16)


def kernel(x, mix_norm, ffn_norm, nsa_w_in, nsa_w_out, nsa_q_gain, nsa_k_gain, nsa_cmp_pos, nsa_cmp_w1, nsa_cmp_b1, nsa_cmp_w2, nsa_cmp_b2, s5_w_in, s5_b_re, s5_b_im, s5_c_re, s5_c_im, s5_d, s5_log_dt, s5_a_re, s5_a_im, s5_w_glu, ffn_w_gate, ffn_w_up, ffn_w_down):
    b, t, d = x.shape
    depth = mix_norm.shape[0]
    for layer in range(depth):
        i = layer // 2
        attn = None
        if layer % 2 == 0:
            attn = _nsa_layer(x, mix_norm[layer], nsa_w_in[i], nsa_w_out[i], nsa_q_gain[i], nsa_k_gain[i],
                              nsa_cmp_pos[i], nsa_cmp_w1[i], nsa_cmp_b1[i], nsa_cmp_w2[i], nsa_cmp_b2[i])
        else:
            x = _s5_layer(x, mix_norm[layer], s5_w_in[i], s5_b_re[i], s5_b_im[i], s5_c_re[i], s5_c_im[i],
                          s5_d[i], s5_log_dt[i], s5_a_re[i], s5_a_im[i], s5_w_glu[i])
        x = _ffn(x.reshape(b * t, d), ffn_norm[layer][None, :], ffn_w_gate[layer].astype(BF16),
                 ffn_w_up[layer].astype(BF16), ffn_w_down[layer].astype(BF16), attn=attn).reshape(b, t, d)
    return x
```

```python
import functools
import math

import jax
import jax.numpy as jnp
from jax import lax
from jax.experimental import pallas as pl
from jax.experimental.pallas import tpu as pltpu

F32 = jnp.float32
BF16 = jnp.bfloat16

EPS = 1e-6
NEG = -1e30
MASK_BIAS = -1e9
LOG2E = 1.4426950408889634

N_HEADS = 16
HEAD_DIM = 64
N_KV_GROUPS = 4
HEADS_PER_GROUP = 4
KV_DIM = N_KV_GROUPS * HEAD_DIM
CMP_BLOCK = 32
CMP_STRIDE = 16
SEL_BLOCK = 64
SEL_TOPN = 16
N_FORCED = 3
WINDOW = 512
N_BRANCHES = 3
S5_GROUP = 16
S5_STATE = 64

LANES = 128
SUBLANES = 8
MXU_DIM = 256
SEL_CHUNK = 512
V7X_VMEM_LIMIT_BYTES = 56 * 1024 * 1024


def _cparams(*semantics):
    return pltpu.CompilerParams(dimension_semantics=semantics,
                                vmem_limit_bytes=V7X_VMEM_LIMIT_BYTES)


def _rms_rows(x, g):
    var = jnp.mean(x * x, axis=-1, keepdims=True)
    return x * lax.rsqrt(var + EPS) * g


def _dot(a, b):
    return jnp.dot(a, b, preferred_element_type=F32)


def _dot_nt(a, b):
    return lax.dot_general(a, b, (((1,), (1,)), ((), ())), preferred_element_type=F32)


def _nsa_proj_kernel(x_ref, g_ref, wq_ref, wk_ref, wc_ref, wg_ref, qg_ref, kg_ref,
                     q_ref, ka_ref, va_ref, kw_ref, vw_ref, kcr_ref, vcr_ref, gate_ref):
    tm = x_ref.shape[0]
    h = _rms_rows(x_ref[...], g_ref[...]).astype(BF16)
    inv_dh = 1.0 / HEAD_DIM

    def head_norm(v, gain):
        var = jnp.sum(v * v, axis=-1, keepdims=True) * inv_dh
        return v * lax.rsqrt(var + EPS) * gain

    lane = lax.broadcasted_iota(jnp.int32, (1, LANES), 1)
    low_half = lane < HEAD_DIM

    def head_tile(v, i):
        tile = v[:, (i // 2) * LANES:(i // 2 + 1) * LANES]
        if i % 2:
            tile = pltpu.roll(tile, HEAD_DIM, 1)
        return jnp.where(low_half, tile, 0.0)

    q = _dot(h, wq_ref[...])
    qg = qg_ref[...]
    for hd in range(N_HEADS):
        g, r = divmod(hd, HEADS_PER_GROUP)
        q_ref[0, g, r] = head_norm(head_tile(q, hd), qg[:, hd * LANES:(hd + 1) * LANES]).astype(BF16)

    row = lax.rem(pl.program_id(0) * tm, SEL_CHUNK) + lax.broadcasted_iota(jnp.int32, (tm, 1), 0)
    blk_flag = ((row // SEL_BLOCK) == (lane - HEAD_DIM)).astype(F32)
    one_flag = (lane == HEAD_DIM).astype(F32)

    kv = _dot(h, wk_ref[...])
    kg = kg_ref[...]
    plan = ((ka_ref, 0, blk_flag), (va_ref, None, one_flag), (kw_ref, 1, None), (vw_ref, None, one_flag))
    for part, (ref, gain_row, flag) in enumerate(plan):
        for g in range(N_KV_GROUPS):
            piece = head_tile(kv, part * N_KV_GROUPS + g)
            if gain_row is not None:
                piece = head_norm(piece, kg[gain_row:gain_row + 1, g * LANES:(g + 1) * LANES])
            if flag is not None:
                piece = piece + flag
            ref[0, g] = piece.astype(BF16)

    c = _dot(h, wc_ref[...])
    kcr_ref[...] = c[:, :KV_DIM]
    vcr_ref[...] = c[:, KV_DIM:]
    gt = jax.nn.sigmoid(_dot(h, wg_ref[...]))
    for g in range(N_KV_GROUPS):
        gate_ref[0, g] = gt[:, g * LANES:(g + 1) * LANES]


def _nsa_proj(x2, b, t, g, wq, wk, wc, wg, qg_p, kg_p, tm=512):
    m, d = x2.shape
    assert SEL_CHUNK % tm == 0 and t % SEL_CHUNK == 0
    nt = t // tm
    gn, rh = N_KV_GROUPS, HEADS_PER_GROUP
    full = lambda a: pl.BlockSpec(a.shape, lambda i: (0,) * a.ndim)
    row = lambda n: pl.BlockSpec((tm, n), lambda i: (i, 0))
    head = pl.BlockSpec((1, gn, tm, LANES), lambda i: (i // nt, 0, i % nt, 0))
    outs = [jax.ShapeDtypeStruct((b, gn, rh, t, LANES), BF16)]
    outs += [jax.ShapeDtypeStruct((b, gn, t, LANES), BF16)] * 4
    outs += [jax.ShapeDtypeStruct((m, KV_DIM), F32)] * 2
    outs += [jax.ShapeDtypeStruct((b, gn, t, LANES), F32)]
    return pl.pallas_call(
        _nsa_proj_kernel, grid=(m // tm,),
        in_specs=[row(d), full(g), full(wq), full(wk), full(wc), full(wg), full(qg_p), full(kg_p)],
        out_specs=[pl.BlockSpec((1, gn, rh, tm, LANES), lambda i: (i // nt, 0, 0, i % nt, 0)),
                   head, head, head, head, row(KV_DIM), row(KV_DIM), head],
        out_shape=outs, compiler_params=_cparams("parallel"), name="nsa_proj",
    )(x2, g, wq, wk, wc, wg, qg_p, kg_p)


def _compress_kernel(a_ref, pos_ref, w1a_ref, w1b_ref, b1_ref, w2_ref, b2_ref, gain_ref, o_ref, *, norm):
    a = a_ref[0, 0]
    tc = a.shape[0]
    pos = pos_ref[...]
    h1 = _dot((a + pos[0:1]).astype(BF16), w1a_ref[...])
    h2 = _dot((a + pos[1:2]).astype(BF16), w1b_ref[...])
    h = h1 + pltpu.roll(h2, tc - 1, 0) + b1_ref[...]
    hid = jax.nn.gelu(h).astype(BF16)
    out = _dot(hid, w2_ref[...]) + b2_ref[...]
    if norm:
        var = jnp.sum(out * out, axis=-1, keepdims=True) * (1.0 / HEAD_DIM)
        out = out * lax.rsqrt(var + EPS) * gain_ref[...]
    o_ref[0, 0] = out.astype(o_ref.dtype)


def _compress(a, pos2, w1a, w1b, b1, w2, b2, gain, norm):
    b, g, tc, w = a.shape
    full = lambda z: pl.BlockSpec(z.shape, lambda i, j: (0,) * z.ndim)
    return pl.pallas_call(
        functools.partial(_compress_kernel, norm=norm), grid=(b, g),
        in_specs=[pl.BlockSpec((1, 1, tc, w), lambda i, j: (i, j, 0, 0)),
                  full(pos2), full(w1a), full(w1b), full(b1), full(w2), full(b2), full(gain)],
        out_specs=pl.BlockSpec((1, 1, tc, LANES), lambda i, j: (i, j, 0, 0)),
        out_shape=jax.ShapeDtypeStruct((b, g, tc, LANES), BF16),
        compiler_params=_cparams("parallel", "parallel"), name="nsa_compress",
    )(a, pos2, w1a, w1b, b1, w2, b2, gain)


def _cmp_select_kernel(q_ref, kc_ref, vct_ref, ovt_ref, ocmp_ref, sel_ref, bias_scr, *, top_n, n_tiers):
    tq = q_ref.shape[3]
    tc = kc_ref.shape[2]
    nsb = ovt_ref.shape[0]
    n_tiles = pl.num_programs(1)
    t = pl.program_id(1) * tq + lax.broadcasted_iota(jnp.int32, (1, tq), 1)
    cur = t // SEL_BLOCK
    any_vis = t >= CMP_BLOCK - 1
    n_rest = max(top_n - N_FORCED, 0)

    def tile_body(rows, blocks):
        n_idx = lax.broadcasted_iota(jnp.int32, (rows, 1), 0)
        bias = jnp.where((n_idx * CMP_STRIDE + (CMP_BLOCK - 1)) <= t, 0.0, NEG)
        j_idx = lax.broadcasted_iota(jnp.int32, (blocks, 1), 0)
        forced = (j_idx == 0) | (j_idx == cur) | (j_idx == cur - 1)
        valid = j_idx <= cur
        j_f = j_idx.astype(F32)
        bias_scr[:rows, :] = bias

        def group_body(g, carry):
            kc = kc_ref[0, g, :rows, :]
            vct = vct_ref[0, g, :, :rows]
            psum = jnp.zeros((rows, tq), F32)
            scores = []
            for r in range(HEADS_PER_GROUP):
                qr = q_ref[0, g, r]
                scores.append(jnp.concatenate([_dot_nt(kc[:rows // 2], qr), _dot_nt(kc[rows // 2:], qr)], axis=0))
            for r in range(HEADS_PER_GROUP):
                s = scores[r] + bias_scr[:rows, :]
                e = jnp.exp2(s - jnp.max(s, axis=0, keepdims=True))
                den = jnp.sum(e, axis=0, keepdims=True)
                p = e * jnp.where(any_vis, 1.0 / den, 0.0)
                pb = p.astype(BF16)
                o_t = (_dot(vct[:, :rows // 2], pb[:rows // 2])
                       + _dot(vct[:, rows // 2:], pb[rows // 2:]))
                ocmp_ref[0, g, r] = o_t[:HEAD_DIM]
                psum = psum + p
            hi = psum.astype(BF16)
            lo = (psum - hi.astype(F32)).astype(BF16)
            ovt = ovt_ref[:blocks, :rows]
            psel = _dot(ovt, hi) + _dot(ovt, lo)
            score = jnp.where(valid & ~forced, psel, NEG)
            for _ in range(n_rest):
                mx = jnp.max(score, axis=0, keepdims=True)
                first = jnp.min(jnp.where(score == mx, j_f, float(nsb)), axis=0, keepdims=True)
                score = jnp.where(j_f == first, -jnp.inf, score)
            sel_ref[0, g, :blocks, :] = jnp.where(forced | ((score == -jnp.inf) & valid), 1.0, 0.0)
            if blocks < nsb:
                sel_ref[0, g, blocks:, :] = jnp.zeros((nsb - blocks, tq), F32)
            return carry

        lax.fori_loop(0, N_KV_GROUPS, group_body, 0, unroll=2)

    tier = (pl.program_id(1) * n_tiers) // n_tiles
    for k in range(n_tiers):
        pl.when(tier == k)(functools.partial(tile_body, tc * (k + 1) // n_tiers, nsb * (k + 1) // n_tiers))


def _cmp_select(q, kc, vct, ovt, tq=256):
    b, g, r, t_len, _ = q.shape
    tc = kc.shape[2]
    nsb = ovt.shape[0]
    tq = min(tq, t_len)
    top_n = min(SEL_TOPN, nsb)
    assert top_n >= N_FORCED
    n_tiers = 4
    assert (t_len // tq) % n_tiers == 0 and nsb % (n_tiers * SUBLANES) == 0 and tc % (n_tiers * LANES // 4) == 0
    return pl.pallas_call(
        functools.partial(_cmp_select_kernel, top_n=top_n, n_tiers=n_tiers), grid=(b, t_len // tq),
        in_specs=[pl.BlockSpec((1, g, r, tq, LANES), lambda i, j: (i, 0, 0, j, 0)),
                  pl.BlockSpec((1, g, tc, LANES), lambda i, j: (i, 0, 0, 0)),
                  pl.BlockSpec((1, g, LANES, tc), lambda i, j: (i, 0, 0, 0)),
                  pl.BlockSpec(ovt.shape, lambda i, j: (0, 0))],
        out_specs=[pl.BlockSpec((1, g, r, HEAD_DIM, tq), lambda i, j: (i, 0, 0, 0, j)),
                   pl.BlockSpec((1, g, nsb, tq), lambda i, j: (i, 0, 0, j))],
        out_shape=[jax.ShapeDtypeStruct((b, g, r, HEAD_DIM, t_len), F32),
                   jax.ShapeDtypeStruct((b, g, nsb, t_len), F32)],
        scratch_shapes=[pltpu.VMEM((tc, tq), F32)],
        compiler_params=_cparams("parallel", "parallel"), name="nsa_cmp_select",
    )(q, kc, vct, ovt)


def _sel_win_kernel(q_ref, ka_ref, vat_ref, kw_ref, vwt_ref, selt_ref, ocmp_ref, gate_ref, o_ref,
                    qa_scr, sa_scr, sb_scr, ma_scr, mb_scr, p_scr, *, kc):
    r_heads, tq = q_ref.shape[2], q_ref.shape[3]
    dh = HEAD_DIM
    t_len = ka_ref.shape[2]
    per_chunk = kc // SEL_BLOCK
    aug_rows = -(-per_chunk // (2 * SUBLANES)) * (2 * SUBLANES)
    t0 = pl.program_id(2) * tq
    t_lane = t0 + lax.broadcasted_iota(jnp.int32, (1, tq), 1)

    for r in range(r_heads):
        qa_scr[:, r * tq:(r + 1) * tq] = q_ref[0, 0, r].astype(F32).T.astype(BF16)

    def softmax_t(s, m_new):
        return jnp.exp2(s - m_new).astype(BF16)

    span = min(tq + WINDOW, t_len)
    w0 = pl.multiple_of(jnp.maximum(t0 + tq - span, 0), tq)
    kpos_w = w0 + lax.broadcasted_iota(jnp.int32, (span, 1), 0)
    in_win = (kpos_w <= t_lane) & (t_lane - kpos_w < WINDOW)
    bias_w = jnp.where(in_win, 0.0, MASK_BIAS)
    s_win = _dot(kw_ref[0, 0, pl.ds(w0, span), :], qa_scr[...]) + jnp.concatenate([bias_w] * r_heads, axis=1)

    n_full = t0 // kc
    last_full = jnp.maximum(n_full - 1, 0)

    def stash_steps(c, s_scr, m_scr, diagonal=False):
        if not diagonal:
            c = jnp.where(c < n_full, c, n_full + 1)
        ci = jnp.minimum(c, n_full)

        def write_mask_rows():
            sel8 = selt_ref[0, 0, pl.ds(pl.multiple_of(ci * per_chunk, per_chunk), per_chunk), :]
            sb = (jnp.where(c <= n_full, sel8, 0.0) - 1.0) * (-MASK_BIAS)
            if aug_rows > per_chunk:
                sb = jnp.concatenate([sb, jnp.zeros((aug_rows - per_chunk, tq), F32)], axis=0)
            qa_scr[dh:dh + aug_rows, :] = jnp.concatenate([sb.astype(BF16)] * r_heads, axis=1)

        def head(n):
            cs = slice(n * tq, (n + 1) * tq)
            s = _dot(ka_ref[0, 0, pl.ds(pl.multiple_of(ci * kc, kc), kc), :], qa_scr[:, cs])
            s_scr[:, cs] = s
            m_scr[0:1, cs] = jnp.max(s, axis=0, keepdims=True)

        return write_mask_rows, head

    def update_head(n, s_scr, m_scr, c, m, acc):
        cs = slice(n * tq, (n + 1) * tq)
        ci = jnp.minimum(c, n_full)
        m_new = jnp.maximum(m, m_scr[0:1, cs])
        p_scr[:, cs] = softmax_t(s_scr[:, cs], m_new)
        pv = _dot(vat_ref[0, 0, :, pl.ds(pl.multiple_of(ci * kc, kc), kc)], p_scr[:, cs])
        return m_new, jnp.exp2(m - m_new) * acc + pv

    def half_trip(c_next, s_next, m_next, c_cur, s_cur, m_cur, state):
        mask_rows, stash_head = stash_steps(c_next, s_next, m_next)
        mask_rows()
        state = list(state)
        for n in range(r_heads):
            stash_head(n)
            state[n] = update_head(n, s_cur, m_cur, c_cur, *state[n])
        return state

    def stash(c, s_scr, m_scr):
        mask_rows, stash_head = stash_steps(c, s_scr, m_scr)
        mask_rows()
        for n in range(r_heads):
            stash_head(n)

    mask_rows, _ = stash_steps(n_full, sa_scr, ma_scr, diagonal=True)
    mask_rows()
    kd0 = pl.multiple_of(n_full * kc, kc)
    causal = jnp.where(kd0 + lax.broadcasted_iota(jnp.int32, (kc, 1), 0) <= t_lane, 0.0, MASK_BIAS)
    s_diag = _dot(ka_ref[0, 0, pl.ds(kd0, kc), :], qa_scr[...]) + jnp.concatenate([causal] * r_heads, axis=1)

    ow = _dot(vwt_ref[0, 0, :, pl.ds(w0, span)], softmax_t(s_win, jnp.max(s_win, axis=0, keepdims=True)))
    o_win_t = ow[:dh] / ow[dh:dh + 1]
    stash(0, sa_scr, ma_scr)
    m0 = jnp.max(s_diag, axis=0, keepdims=True)
    acc0 = _dot(vat_ref[0, 0, :, pl.ds(kd0, kc)], softmax_t(s_diag, m0))
    state = [(m0[:, n * tq:(n + 1) * tq], acc0[:, n * tq:(n + 1) * tq]) for n in range(r_heads)]
    flat = lambda st: tuple(x for pair in st for x in pair)
    unflat = lambda fl: [(fl[2 * n], fl[2 * n + 1]) for n in range(r_heads)]

    def body(i, fl):
        c = 2 * i
        st = half_trip(c + 1, sb_scr, mb_scr, jnp.minimum(c, last_full), sa_scr, ma_scr, unflat(fl))
        st = half_trip(c + 2, sa_scr, ma_scr, jnp.minimum(c + 1, last_full), sb_scr, mb_scr, st)
        return flat(st)

    fl = lax.fori_loop(0, n_full // 2, body, flat(state))

    def tail(fl):
        st = unflat(fl)
        return flat([update_head(n, sa_scr, ma_scr, last_full, *st[n]) for n in range(r_heads)])

    fl = lax.cond(n_full % 2 == 1, tail, lambda fl: fl, fl)
    acc = jnp.concatenate([fl[2 * n + 1] for n in range(r_heads)], axis=1)
    o_sel_t = acc[:dh] / acc[dh:dh + 1]

    gates_t = gate_ref[0, 0].T
    parts = []
    for r in range(r_heads):
        cs = slice(r * tq, (r + 1) * tq)
        parts.append(gates_t[3 * r:3 * r + 1] * ocmp_ref[0, 0, r]
                     + gates_t[3 * r + 1:3 * r + 2] * o_sel_t[:, cs]
                     + gates_t[3 * r + 2:3 * r + 3] * o_win_t[:, cs])
    o_ref[0] = jnp.concatenate(parts, axis=0).T.astype(o_ref.dtype)


def _sel_win(q, ka, vat, kw, vwt, selt, ocmp, gates, tq=256, kc=SEL_CHUNK):
    b, g, r, t_len, _ = q.shape
    nsb = selt.shape[2]
    vrows = vat.shape[2]
    gw = r * HEAD_DIM
    k_spec = pl.BlockSpec((1, 1, t_len, LANES), lambda i, j, k: (i, j, 0, 0))
    vt_spec = pl.BlockSpec((1, 1, vrows, t_len), lambda i, j, k: (i, j, 0, 0))
    return pl.pallas_call(
        functools.partial(_sel_win_kernel, kc=kc), grid=(b, g, t_len // tq),
        in_specs=[pl.BlockSpec((1, 1, r, tq, LANES), lambda i, j, k: (i, j, 0, k, 0)),
                  k_spec, vt_spec, k_spec, vt_spec,
                  pl.BlockSpec((1, 1, nsb, tq), lambda i, j, k: (i, j, 0, k)),
                  pl.BlockSpec((1, 1, r, HEAD_DIM, tq), lambda i, j, k: (i, j, 0, 0, k)),
                  pl.BlockSpec((1, 1, tq, LANES), lambda i, j, k: (i, j, k, 0))],
        out_specs=pl.BlockSpec((1, tq, gw), lambda i, j, k: (i, k, j)),
        out_shape=jax.ShapeDtypeStruct((b, t_len, g * gw), BF16),
        scratch_shapes=[pltpu.VMEM((LANES, r * tq), BF16),
                        pltpu.VMEM((kc, r * tq), F32), pltpu.VMEM((kc, r * tq), F32),
                        pltpu.VMEM((SUBLANES, r * tq), F32), pltpu.VMEM((SUBLANES, r * tq), F32),
                        pltpu.VMEM((kc, r * tq), BF16)],
        compiler_params=_cparams("parallel", "parallel", "arbitrary"), name="nsa_sel_win",
    )(q, ka, vat, kw, vwt, selt, ocmp, gates)


def _ffn_tile(x, g_ref, wg_ref, wu_ref, wd_ref, hc):
    h = _rms_rows(x, g_ref[...]).astype(BF16)
    acc = x
    for c in range(wg_ref.shape[1] // hc):
        sl = slice(c * hc, (c + 1) * hc)
        a = jax.nn.silu(_dot(h, wg_ref[:, sl])) * _dot(h, wu_ref[:, sl])
        acc = acc + _dot(a.astype(BF16), wd_ref[sl, :])
    return acc


def _ffn_kernel(x_ref, g_ref, wg_ref, wu_ref, wd_ref, o_ref, *, hc):
    o_ref[...] = _ffn_tile(x_ref[...], g_ref, wg_ref, wu_ref, wd_ref, hc)


def _attn_out_ffn_kernel(a_ref, wo_ref, x_ref, g_ref, wg_ref, wu_ref, wd_ref, o_ref, *, hc):
    x = x_ref[...] + _dot(a_ref[...], wo_ref[...])
    o_ref[...] = _ffn_tile(x, g_ref, wg_ref, wu_ref, wd_ref, hc)


def _ffn(x2, g, wg, wu, wd, attn=None, tm=512, hc=MXU_DIM):
    m, d = x2.shape
    full = lambda a: pl.BlockSpec(a.shape, lambda i: (0,) * a.ndim)
    row = pl.BlockSpec((tm, d), lambda i: (i, 0))
    if attn is None:
        body, pre, pre_specs = _ffn_kernel, (), []
    else:
        body, pre = _attn_out_ffn_kernel, attn
        pre_specs = [pl.BlockSpec((tm, attn[0].shape[1]), lambda i: (i, 0)), full(attn[1])]
    return pl.pallas_call(
        functools.partial(body, hc=hc), grid=(m // tm,),
        in_specs=pre_specs + [row, full(g), full(wg), full(wu), full(wd)],
        out_specs=row, out_shape=jax.ShapeDtypeStruct((m, d), F32),
        compiler_params=_cparams("parallel"), name="ffn_swiglu",
    )(*pre, x2, g, wg, wu, wd)


def _s5_kernel(x_ref, g_ref, win_ref, pm_ref, pmt_ref, bre_ref, bim_ref, a1_ref, aseg_ref, pw_ref, cre_ref, cim_ref,
               d_ref, wglu_ref, o_ref, xr_scr, xi_scr, cin_scr, car_scr, *, lane_chunk):
    rows, d = x_ref.shape[1], x_ref.shape[2]
    n_state = xr_scr.shape[1]
    n_kt = bre_ref.shape[0]
    kw = bre_ref.shape[1]
    sw = bre_ref.shape[2]
    n_seg = SUBLANES
    seg = rows // n_seg

    @pl.when(pl.program_id(1) == 0)
    def _():
        car_scr[...] = jnp.zeros_like(car_scr)

    x = x_ref[0]
    h = _dot(pm_ref[...], _rms_rows(x, g_ref[...]).astype(BF16)).astype(BF16)
    u = _dot(h, win_ref[...])
    ub = u.astype(BF16)
    for kt in range(n_kt):
        uk = ub[:, kt * kw:(kt + 1) * kw]
        xr_scr[:, kt * sw:(kt + 1) * sw] = _dot(uk, bre_ref[kt])
        xi_scr[:, kt * sw:(kt + 1) * sw] = _dot(uk, bim_ref[kt])

    for lc in range(n_state // lane_chunk):
        ls = slice(lc * lane_chunk, (lc + 1) * lane_chunk)
        ar, ai = a1_ref[0, :, ls], a1_ref[1, :, ls]

        def step(r, carry, ls=ls, ar=ar, ai=ai):
            pr, pi = carry
            r0 = pl.multiple_of(r * SUBLANES, SUBLANES)
            nr = ar * pr - ai * pi + xr_scr[pl.ds(r0, SUBLANES), ls]
            ni = ar * pi + ai * pr + xi_scr[pl.ds(r0, SUBLANES), ls]
            xr_scr[pl.ds(r0, SUBLANES), ls] = nr
            xi_scr[pl.ds(r0, SUBLANES), ls] = ni
            return nr, ni

        zero = jnp.zeros((SUBLANES, lane_chunk), F32)
        lr, li = lax.fori_loop(0, seg, step, (zero, zero), unroll=True)

        sr, si = aseg_ref[0:1, ls], aseg_ref[1:2, ls]
        cr, ci = car_scr[0:1, ls], car_scr[1:2, ls]
        for s in range(n_seg):
            cin_scr[0, s:s + 1, ls] = cr
            cin_scr[1, s:s + 1, ls] = ci
            cr, ci = (sr * cr - si * ci + lr[s:s + 1], sr * ci + si * cr + li[s:s + 1])
        car_scr[0:1, ls] = cr
        car_scr[1:2, ls] = ci

        def fix(r, carry, ls=ls):
            r0 = pl.multiple_of(r * SUBLANES, SUBLANES)
            pr, pi = pw_ref[0, pl.ds(r, 1), ls], pw_ref[1, pl.ds(r, 1), ls]
            cr, ci = cin_scr[0, :, ls], cin_scr[1, :, ls]
            xr_scr[pl.ds(r0, SUBLANES), ls] += pr * cr - pi * ci
            xi_scr[pl.ds(r0, SUBLANES), ls] += pr * ci + pi * cr
            return carry

        lax.fori_loop(0, seg, fix, 0, unroll=True)

    ys = []
    for kt in range(n_kt):
        ss = slice(kt * sw, (kt + 1) * sw)
        ys.append(_dot(xr_scr[:, ss].astype(BF16), cre_ref[kt]) + _dot(xi_scr[:, ss].astype(BF16), cim_ref[kt]))
    y = jnp.concatenate(ys, axis=1) + d_ref[...] * u
    z = _dot(pmt_ref[...], jax.nn.gelu(y).astype(BF16)).astype(BF16)
    vg = _dot(z, wglu_ref[...])
    o_ref[0] = x + vg[:, :d] * jax.nn.sigmoid(vg[:, d:])


def _s5_tables(b_re, b_im, c_re, c_im, log_dt, a_re, a_im, seg):
    ng, ps, cg = b_re.shape
    per_tile = MXU_DIM // cg
    n_kt = ng // per_tile
    dt = jnp.exp(log_dt)[:, None]
    lam_r, lam_i = dt * a_re, dt * a_im
    mag = jnp.exp(lam_r)
    abar_r, abar_i = mag * jnp.cos(lam_i), mag * jnp.sin(lam_i)
    den = a_re * a_re + a_im * a_im
    coef_r = ((abar_r - 1.0) * a_re + abar_i * a_im) / den
    coef_i = (abar_i * a_re - (abar_r - 1.0) * a_im) / den
    bbar_r = coef_r[..., None] * b_re - coef_i[..., None] * b_im
    bbar_i = coef_r[..., None] * b_im + coef_i[..., None] * b_re
    eye = jnp.eye(per_tile, dtype=F32)
    bd_in = lambda m: jnp.einsum('kgpc,gh->kgchp', m.reshape(n_kt, per_tile, ps, cg), eye).reshape(
        n_kt, per_tile * cg, per_tile * ps).astype(BF16)
    bd_out = lambda m: jnp.einsum('kgcp,gh->kgphc', m.reshape(n_kt, per_tile, cg, ps), eye).reshape(
        n_kt, per_tile * ps, per_tile * cg).astype(BF16)

    def power(n):
        pm = jnp.exp(n * lam_r)
        return (pm * jnp.cos(n * lam_i)).reshape(-1), (pm * jnp.sin(n * lam_i)).reshape(-1)

    a1 = jnp.tile(jnp.stack(power(1.0))[:, None, :], (1, SUBLANES, 1))
    aseg = jnp.stack(power(float(seg)))
    steps = [power(float(i + 1)) for i in range(seg)]
    pw = jnp.stack([jnp.stack([p[0] for p in steps]), jnp.stack([p[1] for p in steps])])
    return bd_in(bbar_r), bd_in(bbar_i), a1, aseg, pw, bd_out(c_re), bd_out(-c_im)


def _s5_layer(x, g_norm, w_in, b_re, b_im, c_re, c_im, d_skip, log_dt, a_re, a_im, w_glu, rows=256):
    b, t, d = x.shape
    rows = min(rows, t)
    bre, bim, a1, aseg, pw, cre, cim = _s5_tables(b_re, b_im, c_re, c_im, log_dt, a_re, a_im, rows // SUBLANES)
    n_state = a1.shape[-1]
    full = lambda a: pl.BlockSpec(a.shape, lambda i, j: (0,) * a.ndim)
    seg = rows // SUBLANES
    new_row = jnp.arange(rows)
    pm = (jnp.arange(rows)[None, :] == (new_row % SUBLANES * seg + new_row // SUBLANES)[:, None]).astype(BF16)
    args = (g_norm[None, :], w_in.astype(BF16), pm, pm.T, bre, bim, a1, aseg, pw, cre, cim, d_skip[None, :],
            w_glu.astype(BF16))
    return pl.pallas_call(
        functools.partial(_s5_kernel, lane_chunk=512), grid=(b, t // rows),
        in_specs=[pl.BlockSpec((1, rows, d), lambda i, j: (i, j, 0))] + [full(a) for a in args],
        out_specs=pl.BlockSpec((1, rows, d), lambda i, j: (i, j, 0)),
        out_shape=jax.ShapeDtypeStruct((b, t, d), F32),
        scratch_shapes=[pltpu.VMEM((rows, n_state), F32), pltpu.VMEM((rows, n_state), F32),
                        pltpu.VMEM((2, SUBLANES, n_state), F32), pltpu.VMEM((SUBLANES, n_state), F32)],
        compiler_params=_cparams("parallel", "arbitrary"), name="s5_mixer",
    )(x, *args)


def _pad_heads(w, n_heads, width):
    d = w.shape[0]
    return jnp.pad(w.reshape(d, n_heads, width), ((0, 0), (0, 0), (0, LANES - width))).reshape(d, n_heads * LANES)


def _nsa_layer(x, g_norm, w_in, w_out, q_gain, k_gain, cmp_pos, cmp_w1, cmp_b1, cmp_w2, cmp_b2):
    b, t, d = x.shape
    gn, rh, dh = N_KV_GROUPS, HEADS_PER_GROUP, HEAD_DIM
    assert t % SEL_CHUNK == 0 and t >= SEL_CHUNK + WINDOW
    x2 = x.reshape(b * t, d)
    scale = LOG2E / math.sqrt(dh)

    kvw = lambda i: w_in[:, d + i * KV_DIM:d + (i + 1) * KV_DIM]
    wq = w_in[:, :d].astype(BF16)
    wk = jnp.concatenate([kvw(i) for i in (2, 3, 4, 5)], axis=1).astype(BF16)
    wc = jnp.concatenate([kvw(0), kvw(1)], axis=1).astype(BF16)
    wg = _pad_heads(w_in[:, d + 6 * KV_DIM:], gn, rh * N_BRANCHES).astype(BF16)
    pad_gain = lambda v, n: jnp.tile(jnp.pad(v, (0, LANES - dh)), n)[None, :]
    qg_p = pad_gain(q_gain * scale, N_HEADS)
    kg_p = jnp.concatenate([pad_gain(k_gain[1], gn), pad_gain(k_gain[2], gn)], axis=0)

    q, ka, va, kw, vw, kc_raw, vc_raw, gates = _nsa_proj(x2, b, t, g_norm[None, :], wq, wk, wc, wg, qg_p, kg_p)

    tc = t // CMP_STRIDE
    half = CMP_STRIDE * dh
    blk = lambda a: a.reshape(b, tc, CMP_STRIDE, gn, dh).transpose(0, 3, 1, 2, 4).reshape(b, gn, tc, half)
    comp = []
    for i, (raw, norm) in enumerate(((kc_raw, True), (vc_raw, False))):
        w1 = cmp_w1[i].astype(BF16)
        comp.append(_compress(blk(raw), cmp_pos[i].reshape(2, half), w1[:half], w1[half:], cmp_b1[i][None, :],
                              jnp.pad(cmp_w2[i], ((0, 0), (0, LANES - dh))).astype(BF16),
                              jnp.pad(cmp_b2[i], (0, LANES - dh))[None, :],
                              jnp.pad(k_gain[0], (0, LANES - dh))[None, :], norm))
    kc, vc = comp
    vct = vc.transpose(0, 1, 3, 2)

    nsb = t // SEL_BLOCK
    n_i = jnp.arange(tc)
    j_i = jnp.arange(nsb)
    n_cmp = (t - CMP_BLOCK) // CMP_STRIDE + 1
    ovt = ((n_i[None, :] * CMP_STRIDE < (j_i[:, None] + 1) * SEL_BLOCK)
           & (n_i[None, :] * CMP_STRIDE + CMP_BLOCK > j_i[:, None] * SEL_BLOCK)
           & (n_i[None, :] < n_cmp)).astype(BF16)
    ocmp, selt = _cmp_select(q, kc, vct, ovt)

    vrows = dh + 2 * SUBLANES
    vat = va[..., :vrows].transpose(0, 1, 3, 2)
    vwt = vw[..., :vrows].transpose(0, 1, 3, 2)
    o = _sel_win(q, ka, vat, kw, vwt, selt, ocmp, gates)
    return o.reshape(b * t, d), w_out.astype(BF16)


def kernel(x, mix_norm, ffn_norm, nsa_w_in, nsa_w_out, nsa_q_gain, nsa_k_gain, nsa_cmp_pos, nsa_cmp_w1, nsa_cmp_b1, nsa_cmp_w2, nsa_cmp_b2, s5_w_in, s5_b_re, s5_b_im, s5_c_re, s5_c_im, s5_d, s5_log_dt, s5_a_re, s5_a_im, s5_w_glu, ffn_w_gate, ffn_w_up, ffn_w_down):
    b, t, d = x.shape
    depth = mix_norm.shape[0]
    for layer in range(depth):
        i = layer // 2
        attn = None
        if layer % 2 == 0:
            attn = _nsa_layer(x, mix_norm[layer], nsa_w_in[i], nsa_w_out[i], nsa_q_gain[i], nsa_k_gain[i],
                              nsa_cmp_pos[i], nsa_cmp_w1[i], nsa_cmp_b1[i], nsa_cmp_w2[i], nsa_cmp_b2[i])
        else:
            x = _s5_layer(x, mix_norm[layer], s5_w_in[i], s5_b_re[i], s5_b_im[i], s5_c_re[i], s5_c_im[i],
                          s5_d[i], s5_log_dt[i], s5_a_re[i], s5_a_im[i], s5_w_glu[i])
        x = _ffn(x.reshape(b * t, d), ffn_norm[layer][None, :], ffn_w_gate[layer].astype(BF16),
                 ffn_w_up[layer].astype(BF16), ffn_w_down[layer].astype(BF16), attn=attn).reshape(b, t, d)
    return x
```
